```python
import math
import jax, jax.numpy as jnp
from jax import lax
import numpy as np

D_MODEL = 1024
BATCH = 16
SEQ = 2048
DEPTH = 2

D_MIX = 2 * D_MODEL
GROUP_W = D_MIX // 4
HEAD_DIM = 64
N_HEADS_DIFF = GROUP_W // (2 * HEAD_DIM)
DIFF_V_DIM = 2 * HEAD_DIM
N_HEADS_SB = GROUP_W // HEAD_DIM
N_HEADS_FOX = GROUP_W // HEAD_DIM
N_HEADS_MEM = 4
MEM_HEAD_DIM = GROUP_W // N_HEADS_MEM
MEM_LEN = 256
NUM_BUCKETS = 32
MAX_DISTANCE = 128
BLOCK_Q = 128
EPS = 1e-6
NEG = -1e30
COL_SIZES = (GROUP_W, GROUP_W, GROUP_W, GROUP_W, GROUP_W, GROUP_W, GROUP_W, GROUP_W, GROUP_W, N_HEADS_FOX, GROUP_W, D_MIX)
D_IN_PROJ = sum(COL_SIZES)
COL_SPLITS = tuple(int(c) for c in np.cumsum(COL_SIZES)[:-1])

kernel_name = 'hybrid_diff_stickbreak_fox_mem_block'


def _rmsnorm(x, g):
    x32 = x.astype(jnp.float32)
    y = x32 * lax.rsqrt(jnp.mean(x32 * x32, axis=-1, keepdims=True) + EPS)
    return y.astype(x.dtype) * g


def _heads(t, n):
    b, s, _ = t.shape
    return t.reshape(b, s, n, -1).transpose(0, 2, 1, 3)


def _merge(t):
    b, h, s, d = t.shape
    return t.transpose(0, 2, 1, 3).reshape(b, s, h * d)


def _to_blocks(t):
    b, h, s = t.shape[:3]
    t = t.reshape((b, h, s // BLOCK_Q, BLOCK_Q) + t.shape[3:])
    return jnp.moveaxis(t, 2, 0)


def _from_blocks(t):
    nb, b, h, bq, d = t.shape
    return jnp.moveaxis(t, 0, 2).reshape(b, h, nb * bq, d)


def _block_dist(i, s):
    q_pos = i * BLOCK_Q + jnp.arange(BLOCK_Q, dtype=jnp.int32)
    k_pos = jnp.arange(s, dtype=jnp.int32)
    return q_pos[:, None] - k_pos[None, :]


def _t5_bucket(n):
    max_exact = NUM_BUCKETS // 2
    nf = jnp.maximum(n, 1).astype(jnp.float32)
    large = max_exact + (jnp.log(nf / max_exact) / math.log(MAX_DISTANCE / max_exact)
                         * (NUM_BUCKETS - max_exact)).astype(jnp.int32)
    large = jnp.minimum(large, NUM_BUCKETS - 1)
    return jnp.where(n < max_exact, n, large)


def _diff_attention(q, k, v, rel_bias, lam):
    s = q.shape[2]
    scale = HEAD_DIM ** -0.5
    v32 = v.astype(jnp.float32)

    def block(args):
        i, qb = args
        dist = _block_dist(i, s)
        bias = rel_bias[_t5_bucket(jnp.maximum(dist, 0))]
        bias = jnp.moveaxis(bias, -1, 0).astype(jnp.float32)
        logits = jnp.einsum('bhqcd,bhkcd->bchqk', qb, k).astype(jnp.float32) * scale + bias
        logits = jnp.where(dist >= 0, logits, NEG)
        p = jax.nn.softmax(logits, axis=-1)
        a = p[:, 0] - lam * p[:, 1]
        return jnp.einsum('bhqk,bhkd->bhqd', a, v32)

    out = lax.map(block, (jnp.arange(s // BLOCK_Q), _to_blocks(q)))
    return _from_blocks(out)


def _stick_breaking(q, k, v):
    s = q.shape[2]
    scale = HEAD_DIM ** -0.5
    v32 = v.astype(jnp.float32)

    def block(args):
        i, qb = args
        strict = _block_dist(i, s) > 0
        z = jnp.einsum('bhqd,bhkd->bhqk', qb, k).astype(jnp.float32) * scale
        log_rest = jnp.where(strict, jax.nn.log_sigmoid(-z), 0.0)
        tail = lax.cumsum(log_rest, axis=3, reverse=True) - log_rest
        log_a = jax.nn.log_sigmoid(z) + tail
        a = jnp.where(strict, jnp.exp(log_a), 0.0)
        return jnp.einsum('bhqk,bhkd->bhqd', a, v32)

    out = lax.map(block, (jnp.arange(s // BLOCK_Q), _to_blocks(q)))
    return _from_blocks(out)


def _forgetting_attention(q, k, v, log_f):
    s = q.shape[2]
    scale = HEAD_DIM ** -0.5
    v32 = v.astype(jnp.float32)
    c = jnp.cumsum(log_f, axis=-1)

    def block(args):
        i, qb, cb = args
        dist = _block_dist(i, s)
        logits = jnp.einsum('bhqd,bhkd->bhqk', qb, k).astype(jnp.float32) * scale
        logits = logits + (cb[..., :, None] - c[..., None, :])
        logits = jnp.where(dist >= 0, logits, NEG)
        p = jax.nn.softmax(logits, axis=-1)
        return jnp.einsum('bhqk,bhkd->bhqd', p, v32)

    out = lax.map(block, (jnp.arange(s // BLOCK_Q), _to_blocks(q), _to_blocks(c)))
    return _from_blocks(out)


def _memory_attention(q, km, vm):
    scale = MEM_HEAD_DIM ** -0.5
    logits = jnp.einsum('bhqd,bhkd->bhqk', q, km).astype(jnp.float32) * scale
    p = jax.nn.softmax(logits, axis=-1)
    return jnp.einsum('bhqk,bhkd->bhqd', p, vm.astype(jnp.float32))


def setup_inputs(seed: int = 0) -> dict:
    key = jax.random.key(seed)
    ks = jax.random.split(key, 12)
    f32 = jnp.float32
    x = jax.random.normal(ks[0], (BATCH, SEQ, D_MODEL), f32)
    mem = jax.random.normal(ks[1], (BATCH, MEM_LEN, D_MODEL), f32)
    w_in = jax.random.normal(ks[2], (DEPTH, D_MODEL, D_IN_PROJ), f32) * D_MODEL ** -0.5
    b_forget = jax.random.uniform(ks[3], (DEPTH, N_HEADS_FOX), f32, minval=1.0, maxval=4.0)
    w_mem_kv = jax.random.normal(ks[4], (DEPTH, D_MODEL, 2 * GROUP_W), f32) * D_MODEL ** -0.5
    w_out = jax.random.normal(ks[5], (DEPTH, D_MIX, D_MODEL), f32) * D_MIX ** -0.5
    pre_norm = 1.0 + 0.05 * jax.random.normal(ks[6], (DEPTH, D_MODEL), f32)
    post_norm = 1.0 + 0.05 * jax.random.normal(ks[7], (DEPTH, D_MODEL), f32)
    mem_norm = 1.0 + 0.05 * jax.random.normal(ks[8], (DEPTH, D_MODEL), f32)
    diff_lambda = 0.1 * jax.random.normal(ks[9], (DEPTH, 4, HEAD_DIM), f32)
    diff_head_norm = 1.0 + 0.05 * jax.random.normal(ks[10], (DEPTH, DIFF_V_DIM), f32)
    rel_bias = 0.5 * jax.random.normal(ks[11], (NUM_BUCKETS, N_HEADS_DIFF), f32)
    return {'x': x, 'mem': mem, 'w_in': w_in, 'b_forget': b_forget, 'w_mem_kv': w_mem_kv,
            'w_out': w_out, 'pre_norm': pre_norm, 'post_norm': post_norm, 'mem_norm': mem_norm,
            'diff_lambda': diff_lambda, 'diff_head_norm': diff_head_norm, 'rel_bias': rel_bias}


def reference(x, mem, w_in, b_forget, w_mem_kv, w_out, pre_norm, post_norm, mem_norm,
              diff_lambda, diff_head_norm, rel_bias):
    dt = x.dtype
    b, s, _ = x.shape
    for l in range(DEPTH):
        h = _rmsnorm(x, pre_norm[l])
        proj = h @ w_in[l]
        (a_q, a_k, a_v, sb_q, sb_k, sb_v, f_q, f_k, f_v, f_logit, m_q, gate) = jnp.split(
            proj, COL_SPLITS, axis=-1)

        lam_init = 0.8 - 0.6 * math.exp(-0.3 * l)
        lam_p = diff_lambda[l].astype(jnp.float32)
        lam = (jnp.exp(jnp.sum(lam_p[0] * lam_p[1])) - jnp.exp(jnp.sum(lam_p[2] * lam_p[3]))
               + lam_init)
        qa = _heads(a_q, N_HEADS_DIFF).reshape(b, N_HEADS_DIFF, s, 2, HEAD_DIM)
        ka = _heads(a_k, N_HEADS_DIFF).reshape(b, N_HEADS_DIFF, s, 2, HEAD_DIM)
        va = _heads(a_v, N_HEADS_DIFF)
        o_a = _diff_attention(qa, ka, va, rel_bias, lam)
        o_a = _rmsnorm(o_a, diff_head_norm[l]) * (1.0 - lam_init)

        o_b = _stick_breaking(_heads(sb_q, N_HEADS_SB), _heads(sb_k, N_HEADS_SB),
                              _heads(sb_v, N_HEADS_SB))

        log_f = jax.nn.log_sigmoid((f_logit + b_forget[l]).astype(jnp.float32)).transpose(0, 2, 1)
        o_c = _forgetting_attention(_heads(f_q, N_HEADS_FOX), _heads(f_k, N_HEADS_FOX),
                                    _heads(f_v, N_HEADS_FOX), log_f)

        kv = _rmsnorm(mem, mem_norm[l]) @ w_mem_kv[l]
        km, vm = jnp.split(kv, 2, axis=-1)
        o_m = _memory_attention(_heads(m_q, N_HEADS_MEM), _heads(km, N_HEADS_MEM),
                                _heads(vm, N_HEADS_MEM))

        y = jnp.concatenate([_merge(o_a).astype(dt), _merge(o_b).astype(dt),
                             _merge(o_c).astype(dt), _merge(o_m).astype(dt)], axis=-1)
        y = y * jax.nn.silu(gate)
        x = x + _rmsnorm(y @ w_out[l], post_norm[l])
    return x
```

```python
import functools
import math

import jax
import jax.numpy as jnp
from jax import lax
from jax.experimental import pallas as pl
from jax.experimental.pallas import tpu as pltpu

F32 = jnp.float32
BF16 = jnp.bfloat16

LANES = 128
HEAD_DIM = 64
GROUP_W = 512
N_FOX_HEADS = GROUP_W // HEAD_DIM
NUM_BUCKETS = 32
MAX_DISTANCE = 128
EPS = 1e-6
NEG = -1e30
VMEM_LIMIT = 48 * 1024 * 1024

CB_AQ, CB_AK, CB_AV = 0, 4, 8
CB_SQ, CB_SK, CB_SV = 12, 16, 20
CB_FQ, CB_FK, CB_FV = 24, 28, 32
CB_MQ, CB_GATE = 36, 40
N_PROJ_CB = 56

NT_DIMS = (((1,), (1,)), ((), ()))


def _params(n_axes):
    return pltpu.CompilerParams(dimension_semantics=("arbitrary",) * n_axes,
                                vmem_limit_bytes=VMEM_LIMIT)


def _split3(x):
    hi = x.astype(BF16)
    r = x - hi.astype(F32)
    mid = r.astype(BF16)
    lo = (r - mid.astype(F32)).astype(BF16)
    return hi, mid, lo


def _split2(x):
    hi = x.astype(BF16)
    lo = (x - hi.astype(F32)).astype(BF16)
    return hi, lo


def _log_sigmoid_pair(z):
    sp = jnp.log1p(jnp.exp(-jnp.abs(z)))
    ls_pos = jnp.minimum(z, 0.0) - sp
    return ls_pos, ls_pos - z


def _silu(g):
    g = g.astype(F32)
    return g / (1.0 + jnp.exp(-g))


def _norm_matmul_kernel(x_ref, g_ref, w_ref, wl_ref, o_ref, ol_ref, h_ref):
    @pl.when(pl.program_id(1) == 0)
    def _():
        x = x_ref[...]
        y = x * lax.rsqrt(jnp.mean(x * x, axis=-1, keepdims=True) + EPS)
        h = (y * g_ref[...]).astype(BF16)
        h_ref[...] = h
        ol_ref[...] = jnp.dot(h, wl_ref[...], preferred_element_type=F32)

    res = jnp.dot(h_ref[...], w_ref[...], preferred_element_type=F32).astype(BF16)
    for c in range(o_ref.shape[0]):
        o_ref[c] = res[:, c * LANES:(c + 1) * LANES]


def _norm_matmul(x2d, g, w, wl, tm, tn):
    m, k = x2d.shape
    n = w.shape[1]
    return pl.pallas_call(
        _norm_matmul_kernel,
        grid=(m // tm, n // tn),
        in_specs=[
            pl.BlockSpec((tm, k), lambda i, j: (i, 0)),
            pl.BlockSpec((1, k), lambda i, j: (0, 0)),
            pl.BlockSpec((k, tn), lambda i, j: (0, j)),
            pl.BlockSpec((k, LANES), lambda i, j: (0, 0)),
        ],
        out_specs=[
            pl.BlockSpec((tn // LANES, tm, LANES), lambda i, j: (j, i, 0)),
            pl.BlockSpec((tm, LANES), lambda i, j: (i, 0)),
        ],
        out_shape=[
            jax.ShapeDtypeStruct((n // LANES, m, LANES), BF16),
            jax.ShapeDtypeStruct((m, LANES), F32),
        ],
        scratch_shapes=[pltpu.VMEM((tm, k), BF16)],
        compiler_params=_params(2),
        name="norm_matmul",
    )(x2d, g, w, wl)


def _forget_cumsum_kernel(logit_ref, b_ref, o_ref, carry_ref):
    @pl.when(pl.program_id(1) == 0)
    def _():
        carry_ref[...] = jnp.zeros_like(carry_ref)

    tb = logit_ref.shape[1]
    log_f, _ = _log_sigmoid_pair(logit_ref[0] + b_ref[...])
    row = lax.broadcasted_iota(jnp.int32, (tb, tb), 0)
    col = lax.broadcasted_iota(jnp.int32, (tb, tb), 1)
    tri = jnp.where(col <= row, 1.0, 0.0).astype(BF16)
    c = carry_ref[...]
    for part in _split3(log_f):
        c = c + jnp.dot(tri, part, preferred_element_type=F32)
    carry_ref[...] = c[tb - 1:tb, :]
    o_ref[0, 0] = -(c.T)


def _forget_cumsum(logit3, b_pad, tb):
    b, s, _ = logit3.shape
    return pl.pallas_call(
        _forget_cumsum_kernel,
        grid=(b, s // tb),
        in_specs=[
            pl.BlockSpec((1, tb, LANES), lambda i, j: (i, j, 0)),
            pl.BlockSpec((1, LANES), lambda i, j: (0, 0)),
        ],
        out_specs=pl.BlockSpec((1, 1, LANES, tb), lambda i, j: (i, j, 0, 0)),
        out_shape=jax.ShapeDtypeStruct((b, s // tb, LANES, tb), F32),
        scratch_shapes=[pltpu.VMEM((1, LANES), F32)],
        compiler_params=_params(2),
        name="forget_cumsum",
    )(logit3, b_pad)


def _bias_tiles_kernel(rb_ref, o_ref):
    h = pl.program_id(0)
    j = pl.program_id(1)
    bq, bk = o_ref.shape[2], o_ref.shape[3]
    row = lax.broadcasted_iota(jnp.int32, (bq, bk), 0)
    col = lax.broadcasted_iota(jnp.int32, (bq, bk), 1)
    dist = row - col + j * bk
    n = jnp.maximum(dist, 0)
    max_exact = NUM_BUCKETS // 2
    nf = jnp.maximum(n, 1).astype(F32)
    large = max_exact + (jnp.log(nf / max_exact) / math.log(MAX_DISTANCE / max_exact)
                         * (NUM_BUCKETS - max_exact)).astype(jnp.int32)
    large = jnp.minimum(large, NUM_BUCKETS - 1)
    bucket = jnp.where(n < max_exact, n, large)
    far = rb_ref[NUM_BUCKETS - 1, h]
    bias = jnp.zeros((bq, bk), F32)
    for b in range(NUM_BUCKETS - 1):
        bias = jnp.where(bucket == b, rb_ref[b, h] - far, bias)
    o_ref[0, 0] = jnp.where(dist >= 0, bias, NEG)


def _bias_tiles(rel_bias, bq, bk):
    n_heads = rel_bias.shape[1]
    return pl.pallas_call(
        _bias_tiles_kernel,
        grid=(n_heads, 2),
        in_specs=[pl.BlockSpec(memory_space=pltpu.SMEM)],
        out_specs=pl.BlockSpec((1, 1, bq, bk), lambda h, j: (h, j, 0, 0)),
        out_shape=jax.ShapeDtypeStruct((n_heads, 2, bq, bk), F32),
        compiler_params=_params(2),
        name="bias_tiles",
    )(rel_bias)


def _stack_halves(q):
    lane = lax.broadcasted_iota(jnp.int32, q.shape, 1)
    zero = jnp.zeros_like(q)
    return jnp.concatenate([jnp.where(lane < HEAD_DIM, q, zero),
                            jnp.where(lane < HEAD_DIM, zero, q)], axis=0)


def _key_block(ref, kb, bk):
    return ref[0, 0, pl.ds(pl.multiple_of(kb * bk, bk), bk), :]


def _causal_mask(bq, bk, strict):
    row = lax.broadcasted_iota(jnp.int32, (2 * bq, bk), 0) & (bq - 1)
    col = lax.broadcasted_iota(jnp.int32, (2 * bq, bk), 1)
    return (col < row) if strict else (col <= row)


def _softmax_first(s, v):
    m = jnp.max(s, axis=-1, keepdims=True)
    p = jnp.exp(s - m)
    l = jnp.sum(p, axis=-1, keepdims=True)
    acc = jnp.dot(p.astype(BF16), v, preferred_element_type=F32)
    return m, l, acc


def _softmax_next(s, v, m, l, acc):
    m_new = jnp.maximum(m, jnp.max(s, axis=-1, keepdims=True))
    alpha = jnp.exp(m - m_new)
    p = jnp.exp(s - m_new)
    l = alpha * l + jnp.sum(p, axis=-1, keepdims=True)
    acc = alpha * acc + jnp.dot(p.astype(BF16), v, preferred_element_type=F32)
    return m_new, l, acc


def _attn_specs(bq, s, cb_q, cb_k, cb_v):
    return [
        pl.BlockSpec((1, 1, bq, LANES), lambda b, h, i: (cb_q + h, b, i, 0)),
        pl.BlockSpec((1, 1, s, LANES), lambda b, h, i: (cb_k + h, b, 0, 0)),
        pl.BlockSpec((1, 1, s, LANES), lambda b, h, i: (cb_v + h, b, 0, 0)),
    ]


def _gate_spec(bq, group):
    return pl.BlockSpec((1, 1, bq, LANES), lambda b, h, i: (CB_GATE + 4 * group + h, b, i, 0))


def _out_spec(bq):
    return pl.BlockSpec((1, bq, LANES), lambda b, h, i: (b, i, h))


def _diff_kernel(lam_init, q_ref, k_ref, v_ref, gate_ref, tile_ref, lamp_ref, hn_ref, o_ref):
    bq = q_ref.shape[2]
    qi = pl.program_id(2)
    q = _stack_halves(q_ref[0, 0] * (HEAD_DIM ** -0.5))

    def scores(kb):
        k = _key_block(k_ref, kb, bq)
        v = _key_block(v_ref, kb, bq)
        return lax.dot_general(q, k, NT_DIMS, preferred_element_type=F32), v

    tile0 = tile_ref[0, 0]
    s, v = scores(qi)
    carry = _softmax_first(s + jnp.concatenate([tile0, tile0], axis=0), v)

    tile1 = jnp.where(qi >= 1, tile_ref[0, 1], NEG)
    s, v = scores(jnp.maximum(qi - 1, 0))
    carry = _softmax_next(s + jnp.concatenate([tile1, tile1], axis=0), v, *carry)

    def body(kb, carry):
        s, v = scores(kb)
        return _softmax_next(s, v, *carry)

    _, l, acc = lax.fori_loop(0, jnp.maximum(qi - 1, 0), body, carry)

    lp = lamp_ref[...]
    lam = (jnp.exp(jnp.sum(lp[0:1] * lp[1:2], axis=-1, keepdims=True))
           - jnp.exp(jnp.sum(lp[2:3] * lp[3:4], axis=-1, keepdims=True)) + lam_init)
    a = acc / l
    o = a[:bq] - lam * a[bq:]
    o = o * lax.rsqrt(jnp.mean(o * o, axis=-1, keepdims=True) + EPS)
    o = o * hn_ref[...] * (1.0 - lam_init)
    o_ref[0] = (o * _silu(gate_ref[0, 0])).astype(o_ref.dtype)


def _diff_attention(proj4, tiles, lam_params, head_norm, lam_init, bq):
    _, b, s, _ = proj4.shape
    return pl.pallas_call(
        functools.partial(_diff_kernel, lam_init),
        grid=(b, 4, s // bq),
        in_specs=_attn_specs(bq, s, CB_AQ, CB_AK, CB_AV) + [
            _gate_spec(bq, 0),
            pl.BlockSpec((1, 2, bq, bq), lambda b, h, i: (h, 0, 0, 0)),
            pl.BlockSpec((4, HEAD_DIM), lambda b, h, i: (0, 0)),
            pl.BlockSpec((1, LANES), lambda b, h, i: (0, 0)),
        ],
        out_specs=_out_spec(bq),
        out_shape=jax.ShapeDtypeStruct((b, s, GROUP_W), BF16),
        compiler_params=_params(3),
        name="diff_attention",
    )(proj4, proj4, proj4, proj4, tiles, lam_params, head_norm)


def _stick_kernel(q_ref, k_ref, v_ref, gate_ref, o_ref):
    bq = q_ref.shape[2]
    qi = pl.program_id(2)
    q = _stack_halves(q_ref[0, 0] * (HEAD_DIM ** -0.5))
    strict = _causal_mask(bq, bq, True)

    row = lax.broadcasted_iota(jnp.int32, (bq, bq + LANES), 0)
    col = lax.broadcasted_iota(jnp.int32, (bq, bq + LANES), 1)
    suffix = jnp.where((row > col) | (col >= bq), 1.0, 0.0).astype(BF16)

    def step(kb, rest, acc, mask):
        k = _key_block(k_ref, kb, bq)
        v = _key_block(v_ref, kb, bq)
        z = lax.dot_general(q, k, NT_DIMS, preferred_element_type=F32)
        ls_pos, ls_neg = _log_sigmoid_pair(z)
        if mask is not None:
            ls_neg = jnp.where(mask, ls_neg, 0.0)
        hi, lo = _split2(ls_neg)
        t = (jnp.dot(hi, suffix, preferred_element_type=F32)
             + jnp.dot(lo, suffix, preferred_element_type=F32))
        tail = t[:, :bq] + jnp.concatenate([rest] * (bq // LANES), axis=1)
        a = jnp.exp(ls_pos + tail)
        if mask is not None:
            a = jnp.where(mask, a, 0.0)
        acc = acc + jnp.dot(a.astype(BF16), v, preferred_element_type=F32)
        return rest + t[:, bq:], acc

    rest = jnp.zeros((2 * bq, LANES), F32)
    acc = jnp.zeros((2 * bq, LANES), F32)
    rest, acc = step(qi, rest, acc, strict)

    def body(i, carry):
        return step(qi - 1 - i, carry[0], carry[1], None)

    _, acc = lax.fori_loop(0, qi, body, (rest, acc))
    lane = lax.broadcasted_iota(jnp.int32, (bq, LANES), 1)
    o = jnp.where(lane < HEAD_DIM, acc[:bq], acc[bq:])
    o_ref[0] = (o * _silu(gate_ref[0, 0])).astype(o_ref.dtype)


def _stick_attention(proj4, bq):
    _, b, s, _ = proj4.shape
    return pl.pallas_call(
        _stick_kernel,
        grid=(b, 4, s // bq),
        in_specs=_attn_specs(bq, s, CB_SQ, CB_SK, CB_SV) + [_gate_spec(bq, 1)],
        out_specs=_out_spec(bq),
        out_shape=jax.ShapeDtypeStruct((b, s, GROUP_W), BF16),
        compiler_params=_params(3),
        name="stick_attention",
    )(proj4, proj4, proj4, proj4)


def _fox_kernel(q_ref, k_ref, v_ref, gate_ref, negc_ref, o_ref):
    bq = q_ref.shape[2]
    hb = pl.program_id(1)
    qi = pl.program_id(2)
    q = _stack_halves(q_ref[0, 0] * (HEAD_DIM ** -0.5))
    upper = lax.broadcasted_iota(jnp.int32, (2 * bq, bq), 0) < bq
    head_row = lax.broadcasted_iota(jnp.int32, (N_FOX_HEADS, bq), 0)

    def scores(kb):
        k = _key_block(k_ref, kb, bq)
        v = _key_block(v_ref, kb, bq)
        s = lax.dot_general(q, k, NT_DIMS, preferred_element_type=F32)
        nc = negc_ref[0, kb]
        nc0 = jnp.sum(jnp.where(head_row == 2 * hb, nc, 0.0), axis=0, keepdims=True)
        nc1 = jnp.sum(jnp.where(head_row == 2 * hb + 1, nc, 0.0), axis=0, keepdims=True)
        return s + jnp.where(upper, nc0, nc1), v

    s, v = scores(qi)
    carry = _softmax_first(jnp.where(_causal_mask(bq, bq, False), s, NEG), v)

    def body(kb, carry):
        s, v = scores(kb)
        return _softmax_next(s, v, *carry)

    _, l, acc = lax.fori_loop(0, qi, body, carry)
    a = acc / l
    lane = lax.broadcasted_iota(jnp.int32, (bq, LANES), 1)
    o = jnp.where(lane < HEAD_DIM, a[:bq], a[bq:])
    o_ref[0] = (o * _silu(gate_ref[0, 0])).astype(o_ref.dtype)


def _fox_attention(proj4, negc_t, bq):
    _, b, s, _ = proj4.shape
    return pl.pallas_call(
        _fox_kernel,
        grid=(b, 4, s // bq),
        in_specs=_attn_specs(bq, s, CB_FQ, CB_FK, CB_FV) + [
            _gate_spec(bq, 2),
            pl.BlockSpec((1, s // bq, N_FOX_HEADS, bq), lambda b, h, i: (b, 0, 0, 0)),
        ],
        out_specs=_out_spec(bq),
        out_shape=jax.ShapeDtypeStruct((b, s, GROUP_W), BF16),
        compiler_params=_params(3),
        name="fox_attention",
    )(proj4, proj4, proj4, proj4, negc_t)


def _mem_kernel(q_ref, km_ref, vm_ref, gate_ref, o_ref):
    s = lax.dot_general(q_ref[0, 0], km_ref[0, 0], NT_DIMS, preferred_element_type=F32)
    s = s * (LANES ** -0.5)
    m = jnp.max(s, axis=-1, keepdims=True)
    p = jnp.exp(s - m)
    l = jnp.sum(p, axis=-1, keepdims=True)
    o = jnp.dot(p.astype(BF16), vm_ref[0, 0], preferred_element_type=F32) / l
    o_ref[0] = (o * _silu(gate_ref[0, 0])).astype(o_ref.dtype)


def _mem_attention(proj4, kv4, bq):
    _, b, s, _ = proj4.shape
    mlen = kv4.shape[2]
    return pl.pallas_call(
        _mem_kernel,
        grid=(b, 4, s // bq),
        in_specs=[
            pl.BlockSpec((1, 1, bq, LANES), lambda b, h, i: (CB_MQ + h, b, i, 0)),
            pl.BlockSpec((1, 1, mlen, LANES), lambda b, h, i: (h, b, 0, 0)),
            pl.BlockSpec((1, 1, mlen, LANES), lambda b, h, i: (4 + h, b, 0, 0)),
            _gate_spec(bq, 3),
        ],
        out_specs=_out_spec(bq),
        out_shape=jax.ShapeDtypeStruct((b, s, GROUP_W), BF16),
        compiler_params=_params(3),
        name="mem_attention",
    )(proj4, kv4, kv4, proj4)


def _out_proj_kernel(ya_ref, yb_ref, yc_ref, ym_ref, w_ref, g_ref, x_ref, o_ref):
    y = jnp.concatenate([ya_ref[...], yb_ref[...], yc_ref[...], ym_ref[...]], axis=-1)
    z = jnp.dot(y, w_ref[...], preferred_element_type=F32)
    z = z * lax.rsqrt(jnp.mean(z * z, axis=-1, keepdims=True) + EPS)
    o_ref[...] = x_ref[...] + z * g_ref[...]


def _out_proj(ys, w, g, x2d, tm):
    m, d = x2d.shape
    y_spec = pl.BlockSpec((tm, GROUP_W), lambda i: (i, 0))
    return pl.pallas_call(
        _out_proj_kernel,
        grid=(m // tm,),
        in_specs=[y_spec] * 4 + [
            pl.BlockSpec(w.shape, lambda i: (0, 0)),
            pl.BlockSpec((1, d), lambda i: (0, 0)),
            pl.BlockSpec((tm, d), lambda i: (i, 0)),
        ],
        out_specs=pl.BlockSpec((tm, d), lambda i: (i, 0)),
        out_shape=jax.ShapeDtypeStruct((m, d), F32),
        compiler_params=_params(1),
        name="out_proj",
    )(*ys, w, g, x2d)


def _pick(n, cands):
    for c in cands:
        if n % c == 0:
            return c
    raise ValueError(f"no tile in {cands} divides {n}")


def kernel(x, mem, w_in, b_forget, w_mem_kv, w_out, pre_norm, post_norm, mem_norm,
           diff_lambda, diff_head_norm, rel_bias):
    b, s, d = x.shape
    mlen = mem.shape[1]
    depth = w_in.shape[0]
    m = b * s
    bq = _pick(s, (256, 128))
    tm = _pick(m, (1024, 512, 256))
    tm_mem = _pick(b * mlen, (1024, 512, 256))

    qkv_w = 9 * GROUP_W
    logit_lo, logit_hi = qkv_w, qkv_w + N_FOX_HEADS
    tiles = _bias_tiles(rel_bias, bq, bq)
    mem2d = mem.reshape(b * mlen, d)
    x2d = x.reshape(m, d)
    zero_w = jnp.zeros((d, LANES), BF16)

    for l in range(depth):
        wl = w_in[l]
        w_main = jnp.concatenate([wl[:, :qkv_w], wl[:, logit_hi:]], axis=1).astype(BF16)
        w_logit = jnp.pad(wl[:, logit_lo:logit_hi], ((0, 0), (0, LANES - N_FOX_HEADS))).astype(BF16)
        b_pad = jnp.pad(b_forget[l], (0, LANES - N_FOX_HEADS)).reshape(1, LANES)

        tn = _pick(w_main.shape[1], (1792, 1024, 512))
        proj, logit = _norm_matmul(x2d, pre_norm[l].reshape(1, d), w_main, w_logit, tm, tn)
        proj4 = proj.reshape(N_PROJ_CB, b, s, LANES)
        negc_t = _forget_cumsum(logit.reshape(b, s, LANES), b_pad, bq)

        kv, _ = _norm_matmul(mem2d, mem_norm[l].reshape(1, d), w_mem_kv[l].astype(BF16), zero_w,
                             tm_mem, _pick(w_mem_kv.shape[2], (1024, 512)))
        kv4 = kv.reshape(kv.shape[0], b, mlen, LANES)

        lam_init = 0.8 - 0.6 * math.exp(-0.3 * l)
        y_a = _diff_attention(proj4, tiles, diff_lambda[l], diff_head_norm[l].reshape(1, LANES),
                              lam_init, bq)
        y_b = _stick_attention(proj4, bq)
        y_c = _fox_attention(proj4, negc_t, bq)
        y_m = _mem_attention(proj4, kv4, _pick(s, (512, 256, 128)))
        ys = [y.reshape(m, GROUP_W) for y in (y_a, y_b, y_c, y_m)]
        x2d = _out_proj(ys, w_out[l].astype(BF16), post_norm[l].reshape(1, d), x2d,
                        _pick(m, (512, 256)))
    return x2d.reshape(b, s, d)
```

```python
import functools
import math

import jax
import jax.numpy as jnp
from jax import lax
from jax.experimental import pallas as pl
from jax.experimental.pallas import tpu as pltpu

F32 = jnp.float32
BF16 = jnp.bfloat16

LANES = 128
BF16_ROWS = 16
HEAD_DIM = 64
GROUP_W = 512
N_CB = GROUP_W // LANES
N_FOX_HEADS = GROUP_W // HEAD_DIM
NUM_BUCKETS = 32
MAX_DISTANCE = 128
EPS = 1e-6
NEG = -1e30
LOG2E = 1.4426950408889634
VMEM_LIMIT = 56 * 1024 * 1024

CB_AQ, CB_AK = 0, 4
CB_SQ, CB_SK, CB_SV = 8, 12, 16
CB_FQ, CB_FK = 20, 24
CB_MQ, CB_GATE = 28, 32
N_PROJ_CB = 48
VT_A, VT_F = 0, 1

NT_DIMS = (((1,), (1,)), ((), ()))


def _params(n_axes):
    return pltpu.CompilerParams(dimension_semantics=("arbitrary",) * n_axes,
                                vmem_limit_bytes=VMEM_LIMIT)


def _split3(x):
    hi = x.astype(BF16)
    r = x - hi.astype(F32)
    mid = r.astype(BF16)
    lo = (r - mid.astype(F32)).astype(BF16)
    return hi, mid, lo


def _split2(x):
    hi = x.astype(BF16)
    lo = (x - hi.astype(F32)).astype(BF16)
    return hi, lo


def _silu(g):
    g = g.astype(F32)
    return g / (1.0 + jnp.exp(-g))


def _rms_rows(x, g_ref):
    y = x * lax.rsqrt(jnp.mean(x * x, axis=-1, keepdims=True) + EPS)
    return (y * g_ref[...]).astype(BF16)


def _store_col_blocks(o_ref, res):
    for c in range(o_ref.shape[0]):
        o_ref[c] = res[:, c * LANES:(c + 1) * LANES]


def _in_proj_kernel(x_ref, g_ref, w_ref, wl_ref, wvt_ref, o_ref, ol_ref, ovt_ref, h_ref):
    @pl.when(pl.program_id(1) == 0)
    def _():
        h = _rms_rows(x_ref[...], g_ref)
        h_ref[...] = h
        ol_ref[...] = jnp.dot(h, wl_ref[...], preferred_element_type=F32)
        vt = lax.dot_general(wvt_ref[...], h, NT_DIMS, preferred_element_type=F32).astype(BF16)
        for c in range(ovt_ref.shape[0]):
            ovt_ref[c] = vt[c * LANES:(c + 1) * LANES, :]

    res = jnp.dot(h_ref[...], w_ref[...], preferred_element_type=F32).astype(BF16)
    _store_col_blocks(o_ref, res)


def _in_proj(x2d, g, w, wl, wvt, tm, tn):
    m, k = x2d.shape
    n = w.shape[1]
    nv = wvt.shape[0]
    return pl.pallas_call(
        _in_proj_kernel,
        grid=(m // tm, n // tn),
        in_specs=[
            pl.BlockSpec((tm, k), lambda i, j: (i, 0)),
            pl.BlockSpec((1, k), lambda i, j: (0, 0)),
            pl.BlockSpec((k, tn), lambda i, j: (0, j)),
            pl.BlockSpec((k, LANES), lambda i, j: (0, 0)),
            pl.BlockSpec((nv, k), lambda i, j: (0, 0)),
        ],
        out_specs=[
            pl.BlockSpec((tn // LANES, tm, LANES), lambda i, j: (j, i, 0)),
            pl.BlockSpec((tm, LANES), lambda i, j: (i, 0)),
            pl.BlockSpec((nv // LANES, LANES, tm), lambda i, j: (0, 0, i)),
        ],
        out_shape=[
            jax.ShapeDtypeStruct((n // LANES, m, LANES), BF16),
            jax.ShapeDtypeStruct((m, LANES), F32),
            jax.ShapeDtypeStruct((nv // LANES, LANES, m), BF16),
        ],
        scratch_shapes=[pltpu.VMEM((tm, k), BF16)],
        compiler_params=_params(2),
        name="in_proj",
    )(x2d, g, w, wl, wvt)


def _norm_matmul_kernel(x_ref, g_ref, w_ref, o_ref):
    res = jnp.dot(_rms_rows(x_ref[...], g_ref), w_ref[...], preferred_element_type=F32)
    _store_col_blocks(o_ref, res.astype(BF16))


def _norm_matmul(x2d, g, w, tm):
    m, k = x2d.shape
    n = w.shape[1]
    return pl.pallas_call(
        _norm_matmul_kernel,
        grid=(m // tm,),
        in_specs=[
            pl.BlockSpec((tm, k), lambda i: (i, 0)),
            pl.BlockSpec((1, k), lambda i: (0, 0)),
            pl.BlockSpec((k, n), lambda i: (0, 0)),
        ],
        out_specs=pl.BlockSpec((n // LANES, tm, LANES), lambda i: (0, i, 0)),
        out_shape=jax.ShapeDtypeStruct((n // LANES, m, LANES), BF16),
        compiler_params=_params(1),
        name="norm_matmul",
    )(x2d, g, w)


def _forget_cumsum_kernel(logit_ref, b_ref, o_ref, carry_ref):
    @pl.when(pl.program_id(1) == 0)
    def _():
        carry_ref[...] = jnp.zeros_like(carry_ref)

    tb = logit_ref.shape[1]
    z = logit_ref[0] + b_ref[...]
    log_f = jnp.minimum(z, 0.0) - jnp.log1p(jnp.exp(-jnp.abs(z)))
    row = lax.broadcasted_iota(jnp.int32, (tb, tb), 0)
    col = lax.broadcasted_iota(jnp.int32, (tb, tb), 1)
    tri = jnp.where(col <= row, 1.0, 0.0).astype(BF16)
    c = carry_ref[...]
    for part in _split3(log_f):
        c = c + jnp.dot(tri, part, preferred_element_type=F32)
    carry_ref[...] = c[tb - 1:tb, :]
    o_ref[0] = -c


def _forget_cumsum(logit3, b_pad, tb):
    b, s, _ = logit3.shape
    return pl.pallas_call(
        _forget_cumsum_kernel,
        grid=(b, s // tb),
        in_specs=[
            pl.BlockSpec((1, tb, LANES), lambda i, j: (i, j, 0)),
            pl.BlockSpec((1, LANES), lambda i, j: (0, 0)),
        ],
        out_specs=pl.BlockSpec((1, tb, LANES), lambda i, j: (i, j, 0)),
        out_shape=jax.ShapeDtypeStruct((b, s, LANES), F32),
        scratch_shapes=[pltpu.VMEM((1, LANES), F32)],
        compiler_params=_params(2),
        name="forget_cumsum",
    )(logit3, b_pad)


def _bias_tiles_kernel(rb_ref, o_ref):
    h = pl.program_id(0)
    j = pl.program_id(1)
    bk, bq = o_ref.shape[2], o_ref.shape[3]
    key = lax.broadcasted_iota(jnp.int32, (bk, bq), 0)
    query = lax.broadcasted_iota(jnp.int32, (bk, bq), 1)
    dist = query - key + j * bk
    n = jnp.maximum(dist, 0)
    max_exact = NUM_BUCKETS // 2
    nf = jnp.maximum(n, 1).astype(F32)
    large = max_exact + (jnp.log(nf / max_exact) / math.log(MAX_DISTANCE / max_exact)
                         * (NUM_BUCKETS - max_exact)).astype(jnp.int32)
    large = jnp.minimum(large, NUM_BUCKETS - 1)
    bucket = jnp.where(n < max_exact, n, large)
    far = rb_ref[NUM_BUCKETS - 1, h]
    bias = jnp.zeros((bk, bq), F32)
    for b in range(NUM_BUCKETS - 1):
        bias = jnp.where(bucket == b, rb_ref[b, h] - far, bias)
    o_ref[0, 0] = jnp.where(dist >= 0, bias, NEG)


def _bias_tiles(rel_bias, bk, bq):
    n_heads = rel_bias.shape[1]
    return pl.pallas_call(
        _bias_tiles_kernel,
        grid=(n_heads, 2),
        in_specs=[pl.BlockSpec(memory_space=pltpu.SMEM)],
        out_specs=pl.BlockSpec((1, 1, bk, bq), lambda h, j: (h, j, 0, 0)),
        out_shape=jax.ShapeDtypeStruct((n_heads, 2, bk, bq), F32),
        compiler_params=_params(2),
        name="bias_tiles",
    )(rel_bias)


def _stack_halves(q):
    lane = lax.broadcasted_iota(jnp.int32, q.shape, 1)
    zero = jnp.zeros_like(q)
    return jnp.concatenate([jnp.where(lane < HEAD_DIM, q, zero),
                            jnp.where(lane < HEAD_DIM, zero, q)], axis=0)


def _key_block(ref, h, kb, bk):
    return ref[h, 0, pl.ds(pl.multiple_of(kb * bk, bk), bk), :]


def _q_spec(bq, cb):
    return pl.BlockSpec((N_CB, 1, bq, LANES), lambda b, i: (cb // N_CB, b, i, 0))


def _kv_spec(s, cb):
    return pl.BlockSpec((N_CB, 1, s, LANES), lambda b, i: (cb // N_CB, b, 0, 0))


def _gate_spec(bq, group):
    return pl.BlockSpec((N_CB, 1, bq, LANES), lambda b, i: (CB_GATE // N_CB + group, b, i, 0))


def _out_spec(bq):
    return pl.BlockSpec((1, bq, GROUP_W), lambda b, i: (b, i, 0))


def _store_gated(o_ref, gate_ref, h, o):
    o_ref[0, :, h * LANES:(h + 1) * LANES] = (o * _silu(gate_ref[h, 0])).astype(o_ref.dtype)


def _vt_block(vt_ref, h, kb, bk):
    vt = vt_ref[h, :, pl.ds(pl.multiple_of(kb * bk, bk), bk)]
    return jnp.concatenate([vt, jnp.ones((BF16_ROWS, bk), BF16)], axis=0)


def _scores_t(q_st, k_ref, h, kb, bk):
    return lax.dot_general(_key_block(k_ref, h, kb, bk), q_st, NT_DIMS,
                           preferred_element_type=F32)


VALUE_LAG = 2


def _softmax_t_blocks(items, carries):
    ms = [None if carries is None else carries[h][0] for h in range(N_CB)]
    accs = [None if carries is None else carries[h][1] for h in range(N_CB)]
    pending = {}

    def score(i):
        h, score_fn, _ = items[i]
        s = score_fn()
        m = jnp.max(s, axis=0, keepdims=True)
        m_new = m if ms[h] is None else jnp.maximum(ms[h], m)
        pending[i] = (ms[h], m_new, jnp.exp(s - m_new).astype(BF16))
        ms[h] = m_new

    def value(i):
        h, _, vt_fn = items[i]
        m_old, m_new, p = pending.pop(i)
        pv = jnp.dot(vt_fn(), p, preferred_element_type=F32)
        accs[h] = pv if accs[h] is None else jnp.exp(m_old - m_new) * accs[h] + pv

    for i in range(len(items) + VALUE_LAG):
        if i < len(items):
            score(i)
        if i >= VALUE_LAG:
            value(i - VALUE_LAG)
    return tuple((ms[h], accs[h]) for h in range(N_CB))


def _pair_loop(n_blocks, block_of, step, carries):
    carries = lax.fori_loop(
        0, n_blocks // 2, lambda i, c: step((block_of(2 * i), block_of(2 * i + 1)), c), carries)
    return lax.fori_loop(
        0, n_blocks % 2, lambda i, c: step((block_of(n_blocks - 1),), c), carries)


def _normalized_t(acc):
    return acc[:LANES] * (1.0 / acc[LANES:LANES + 1])


def _diff_kernel(lam_init, q_ref, k_ref, vt_ref, gate_ref, tile_ref, lamp_ref, hn_ref, o_ref):
    bq = q_ref.shape[2]
    qi = pl.program_id(1)
    qs = [_stack_halves(q_ref[h, 0] * (HEAD_DIM ** -0.5)) for h in range(N_CB)]

    def item(h, kb, tile_fn=None):
        def score():
            s = _scores_t(qs[h], k_ref, h, kb, bq)
            if tile_fn is None:
                return s
            tile = tile_fn(h)
            return s + jnp.concatenate([tile, tile], axis=1)
        return h, score, lambda: _vt_block(vt_ref, h, kb, bq)

    def step(kbs, carries):
        return _softmax_t_blocks([item(h, kb) for kb in kbs for h in range(N_CB)], carries)

    near = ([item(h, qi, lambda h: tile_ref[h, 0]) for h in range(N_CB)]
            + [item(h, jnp.maximum(qi - 1, 0), lambda h: jnp.where(qi >= 1, tile_ref[h, 1], NEG))
               for h in range(N_CB)])
    carries = _softmax_t_blocks(near, None)
    carries = _pair_loop(jnp.maximum(qi - 1, 0), lambda i: i, step, carries)

    lp = lamp_ref[...]
    lam = (jnp.exp(jnp.sum(lp[0:1] * lp[1:2], axis=-1, keepdims=True))
           - jnp.exp(jnp.sum(lp[2:3] * lp[3:4], axis=-1, keepdims=True)) + lam_init)
    for h in range(N_CB):
        a = _normalized_t(carries[h][1])
        o = (a[:, :bq] - lam * a[:, bq:]).T
        o = o * lax.rsqrt(jnp.mean(o * o, axis=-1, keepdims=True) + EPS)
        _store_gated(o_ref, gate_ref, h, o * hn_ref[...] * (1.0 - lam_init))


def _diff_attention(proj4, vt, tiles, lam_params, head_norm, lam_init, bq):
    _, b, s, _ = proj4.shape
    return pl.pallas_call(
        functools.partial(_diff_kernel, lam_init),
        grid=(b, s // bq),
        in_specs=[
            _q_spec(bq, CB_AQ), _kv_spec(s, CB_AK),
            pl.BlockSpec((N_CB, LANES, s), lambda b, i: (VT_A, 0, b)),
            _gate_spec(bq, 0),
            pl.BlockSpec((N_CB, 2, bq, bq), lambda b, i: (0, 0, 0, 0)),
            pl.BlockSpec((4, HEAD_DIM), lambda b, i: (0, 0)),
            pl.BlockSpec((1, LANES), lambda b, i: (0, 0)),
        ],
        out_specs=_out_spec(bq),
        out_shape=jax.ShapeDtypeStruct((b, s, GROUP_W), BF16),
        compiler_params=_params(2),
        name="diff_attention",
    )(proj4, proj4, vt, proj4, tiles, lam_params, head_norm)


def _stick_kernel(q_ref, k_ref, v_ref, gate_ref, o_ref):
    bq = q_ref.shape[2]
    qi = pl.program_id(1)
    qs = [_stack_halves(q_ref[h, 0] * (HEAD_DIM ** -0.5)) for h in range(N_CB)]
    row = lax.broadcasted_iota(jnp.int32, (2 * bq, bq), 0) & (bq - 1)
    col = lax.broadcasted_iota(jnp.int32, (2 * bq, bq), 1)
    strict = col < row

    row = lax.broadcasted_iota(jnp.int32, (bq, bq), 0)
    col = lax.broadcasted_iota(jnp.int32, (bq, bq), 1)
    suffix = jnp.where(row > col, 1.0, 0.0).astype(BF16)
    suffix2 = jnp.concatenate([suffix, suffix], axis=0)

    def step(kbs, carries, mask=None):
        items = [(h, kb) for kb in kbs for h in range(N_CB)]
        rests = [carries[h][0] for h in range(N_CB)]
        accs = [carries[h][1] for h in range(N_CB)]
        pending = {}

        def score(i):
            h, kb = items[i]
            z = lax.dot_general(qs[h], _key_block(k_ref, h, kb, bq), NT_DIMS,
                                preferred_element_type=F32)
            sp = jnp.log(1.0 + jnp.exp2(jnp.abs(z) * -LOG2E))
            ls_pos = jnp.minimum(z, 0.0) - sp
            ls_neg = ls_pos - z
            if mask is not None:
                ls_neg = jnp.where(mask, ls_neg, 0.0)
            pending[i] = (ls_pos, ls_neg[:, 0:1], jnp.concatenate(_split2(ls_neg), axis=1))

        def cumsum(i):
            h, _ = items[i]
            ls_pos, first, hi_lo = pending[i]
            t = jnp.dot(hi_lo, suffix2, preferred_element_type=F32)
            a = jnp.exp2((ls_pos + t) * LOG2E)
            if mask is not None:
                a = jnp.where(mask, a, 0.0)
            pending[i] = (a.astype(BF16), jnp.exp2(rests[h] * LOG2E))
            rests[h] = rests[h] + (t[:, 0:1] + first)

        def value(i):
            h, kb = items[i]
            a, scale = pending.pop(i)
            accs[h] = accs[h] + scale * jnp.dot(a, _key_block(v_ref, h, kb, bq),
                                                preferred_element_type=F32)

        for i in range(len(items) + 2):
            if i < len(items):
                score(i)
            if 1 <= i <= len(items):
                cumsum(i - 1)
            if i >= 2:
                value(i - 2)
        return tuple((rests[h], accs[h]) for h in range(N_CB))

    zero = (jnp.zeros((2 * bq, 1), F32), jnp.zeros((2 * bq, LANES), F32))
    carries = step((qi,), (zero,) * N_CB, strict)
    carries = _pair_loop(qi, lambda i: qi - 1 - i, step, carries)
    lane = lax.broadcasted_iota(jnp.int32, (bq, LANES), 1)
    for h in range(N_CB):
        acc = carries[h][1]
        _store_gated(o_ref, gate_ref, h, jnp.where(lane < HEAD_DIM, acc[:bq], acc[bq:]))


def _stick_attention(proj4, bq):
    _, b, s, _ = proj4.shape
    return pl.pallas_call(
        _stick_kernel,
        grid=(b, s // bq),
        in_specs=[_q_spec(bq, CB_SQ), _kv_spec(s, CB_SK), _kv_spec(s, CB_SV), _gate_spec(bq, 1)],
        out_specs=_out_spec(bq),
        out_shape=jax.ShapeDtypeStruct((b, s, GROUP_W), BF16),
        compiler_params=_params(2),
        name="stick_attention",
    )(proj4, proj4, proj4, proj4)


def _fox_kernel(q_ref, k_ref, vt_ref, gate_ref, negc_ref, o_ref):
    bq = q_ref.shape[2]
    qi = pl.program_id(1)
    qs = [_stack_halves(q_ref[h, 0] * (HEAD_DIM ** -0.5)) for h in range(N_CB)]
    key = lax.broadcasted_iota(jnp.int32, (bq, 2 * bq), 0)
    query = lax.broadcasted_iota(jnp.int32, (bq, 2 * bq), 1) & (bq - 1)
    causal = key <= query

    def item(h, kb, mask):
        def score():
            nc = negc_ref[0, pl.ds(pl.multiple_of(kb * bq, bq), bq), :]
            bias = jnp.concatenate(
                [jnp.broadcast_to(nc[:, 2 * h:2 * h + 1], (bq, bq)),
                 jnp.broadcast_to(nc[:, 2 * h + 1:2 * h + 2], (bq, bq))], axis=1)
            s = _scores_t(qs[h], k_ref, h, kb, bq) + bias
            return s if mask is None else jnp.where(mask, s, NEG)
        return h, score, lambda: _vt_block(vt_ref, h, kb, bq)

    def step(kbs, carries, mask=None):
        return _softmax_t_blocks([item(h, kb, mask) for kb in kbs for h in range(N_CB)], carries)

    carries = _pair_loop(qi, lambda i: i, step, step((qi,), None, causal))
    for h in range(N_CB):
        a = _normalized_t(carries[h][1])
        o = jnp.concatenate([a[:HEAD_DIM, :bq], a[HEAD_DIM:, bq:]], axis=0).T
        _store_gated(o_ref, gate_ref, h, o)


def _fox_attention(proj4, vt, negc, bq):
    _, b, s, _ = proj4.shape
    return pl.pallas_call(
        _fox_kernel,
        grid=(b, s // bq),
        in_specs=[
            _q_spec(bq, CB_FQ), _kv_spec(s, CB_FK),
            pl.BlockSpec((N_CB, LANES, s), lambda b, i: (VT_F, 0, b)),
            _gate_spec(bq, 2),
            pl.BlockSpec((1, s, LANES), lambda b, i: (b, 0, 0)),
        ],
        out_specs=_out_spec(bq),
        out_shape=jax.ShapeDtypeStruct((b, s, GROUP_W), BF16),
        compiler_params=_params(2),
        name="fox_attention",
    )(proj4, proj4, vt, proj4, negc)


def _mem_kernel(q_ref, km_ref, vm_ref, gate_ref, o_ref):
    for h in range(N_CB):
        s = lax.dot_general(q_ref[h, 0], km_ref[h, 0], NT_DIMS, preferred_element_type=F32)
        s = s * (LANES ** -0.5)
        m = jnp.max(s, axis=-1, keepdims=True)
        p = jnp.exp(s - m)
        l = jnp.sum(p, axis=-1, keepdims=True)
        o = jnp.dot(p.astype(BF16), vm_ref[h, 0], preferred_element_type=F32) / l
        _store_gated(o_ref, gate_ref, h, o)


def _mem_attention(proj4, kv4, bq):
    _, b, s, _ = proj4.shape
    mlen = kv4.shape[2]
    return pl.pallas_call(
        _mem_kernel,
        grid=(b, s // bq),
        in_specs=[
            _q_spec(bq, CB_MQ),
            pl.BlockSpec((N_CB, 1, mlen, LANES), lambda b, i: (0, b, 0, 0)),
            pl.BlockSpec((N_CB, 1, mlen, LANES), lambda b, i: (1, b, 0, 0)),
            _gate_spec(bq, 3),
        ],
        out_specs=_out_spec(bq),
        out_shape=jax.ShapeDtypeStruct((b, s, GROUP_W), BF16),
        compiler_params=_params(2),
        name="mem_attention",
    )(proj4, kv4, kv4, proj4)


def _out_proj_kernel(ya_ref, yb_ref, yc_ref, ym_ref, w_ref, g_ref, x_ref, o_ref):
    y = jnp.concatenate([ya_ref[...], yb_ref[...], yc_ref[...], ym_ref[...]], axis=-1)
    z = jnp.dot(y, w_ref[...], preferred_element_type=F32)
    z = z * lax.rsqrt(jnp.mean(z * z, axis=-1, keepdims=True) + EPS)
    o_ref[...] = x_ref[...] + z * g_ref[...]


def _out_proj(ys, w, g, x2d, tm):
    m, d = x2d.shape
    y_spec = pl.BlockSpec((tm, GROUP_W), lambda i: (i, 0))
    return pl.pallas_call(
        _out_proj_kernel,
        grid=(m // tm,),
        in_specs=[y_spec] * 4 + [
            pl.BlockSpec(w.shape, lambda i: (0, 0)),
            pl.BlockSpec((1, d), lambda i: (0, 0)),
            pl.BlockSpec((tm, d), lambda i: (i, 0)),
        ],
        out_specs=pl.BlockSpec((tm, d), lambda i: (i, 0)),
        out_shape=jax.ShapeDtypeStruct((m, d), F32),
        compiler_params=_params(1),
        name="out_proj",
    )(*ys, w, g, x2d)


def _pick(n, cands):
    for c in cands:
        if n % c == 0:
            return c
    raise ValueError(f"no tile in {cands} divides {n}")


def kernel(x, mem, w_in, b_forget, w_mem_kv, w_out, pre_norm, post_norm, mem_norm,
           diff_lambda, diff_head_norm, rel_bias):
    b, s, d = x.shape
    mlen = mem.shape[1]
    depth = w_in.shape[0]
    m = b * s
    bq = _pick(s, (256, 128))
    tm = _pick(m, (1024, 512, 256))

    gw = GROUP_W
    logit_lo, logit_hi = 9 * gw, 9 * gw + N_FOX_HEADS
    tiles = _bias_tiles(rel_bias, bq, bq)
    mem2d = mem.reshape(b * mlen, d)
    x2d = x.reshape(m, d)

    for l in range(depth):
        wl = w_in[l]
        w_main = jnp.concatenate([wl[:, 0:2 * gw], wl[:, 3 * gw:6 * gw], wl[:, 6 * gw:8 * gw],
                                  wl[:, logit_hi:]], axis=1).astype(BF16)
        w_vt = jnp.concatenate([wl[:, 2 * gw:3 * gw], wl[:, 8 * gw:9 * gw]], axis=1).T.astype(BF16)
        w_logit = jnp.pad(wl[:, logit_lo:logit_hi], ((0, 0), (0, LANES - N_FOX_HEADS))).astype(BF16)
        b_pad = jnp.pad(b_forget[l], (0, LANES - N_FOX_HEADS)).reshape(1, LANES)

        tn = _pick(w_main.shape[1], (1536, 1024, 512))
        proj, logit, vt = _in_proj(x2d, pre_norm[l].reshape(1, d), w_main, w_logit, w_vt, tm, tn)
        proj4 = proj.reshape(N_PROJ_CB, b, s, LANES)
        negc = _forget_cumsum(logit.reshape(b, s, LANES), b_pad, _pick(s, (256, 128)))

        kv = _norm_matmul(mem2d, mem_norm[l].reshape(1, d), w_mem_kv[l].astype(BF16),
                          _pick(b * mlen, (1024, 512, 256)))
        kv4 = kv.reshape(kv.shape[0], b, mlen, LANES)

        lam_init = 0.8 - 0.6 * math.exp(-0.3 * l)
        y_a = _diff_attention(proj4, vt, tiles, diff_lambda[l],
                              diff_head_norm[l].reshape(1, LANES), lam_init, bq)
        y_b = _stick_attention(proj4, bq)
        y_c = _fox_attention(proj4, vt, negc, bq)
        y_m = _mem_attention(proj4, kv4, _pick(s, (512, 256, 128)))
        ys = [y.reshape(m, GROUP_W) for y in (y_a, y_b, y_c, y_m)]
        x2d = _out_proj(ys, w_out[l].astype(BF16), post_norm[l].reshape(1, d), x2d,
                        _pick(m, (512, 256)))
    return x2d.reshape(b, s, d)
```

```python
import functools
import math

import jax
import jax.numpy as jnp
from jax import lax
from jax.experimental import pallas as pl
from jax.experimental.pallas import tpu as pltpu

F32 = jnp.float32
BF16 = jnp.bfloat16

LANES = 128
BF16_ROWS = 16
HEAD_DIM = 64
GROUP_W = 512
N_CB = GROUP_W // LANES
N_FOX_HEADS = GROUP_W // HEAD_DIM
NUM_BUCKETS = 32
MAX_DISTANCE = 128
EPS = 1e-6
NEG = -1e30
LOG2E = 1.4426950408889634
VMEM_LIMIT = 56 * 1024 * 1024

CB_AQ, CB_AK = 0, 4
CB_SQ, CB_SK = 8, 12
CB_FQ, CB_FK = 16, 20
CB_MQ, CB_GATE = 24, 28
N_PROJ_CB = 44
VT_A, VT_S, VT_F = 0, 1, 2

NT_DIMS = (((1,), (1,)), ((), ()))


def _params(n_axes):
    return pltpu.CompilerParams(dimension_semantics=("arbitrary",) * n_axes,
                                vmem_limit_bytes=VMEM_LIMIT)


def _split3(x):
    hi = x.astype(BF16)
    r = x - hi.astype(F32)
    mid = r.astype(BF16)
    lo = (r - mid.astype(F32)).astype(BF16)
    return hi, mid, lo


def _split2(x):
    hi = x.astype(BF16)
    lo = (x - hi.astype(F32)).astype(BF16)
    return hi, lo


def _silu(g):
    g = g.astype(F32)
    return g / (1.0 + jnp.exp(-g))


def _rms_rows(x, g_ref):
    y = x * lax.rsqrt(jnp.mean(x * x, axis=-1, keepdims=True) + EPS)
    return (y * g_ref[...]).astype(BF16)


def _store_col_blocks(o_ref, res):
    for c in range(o_ref.shape[0]):
        o_ref[c] = res[:, c * LANES:(c + 1) * LANES]


def _in_proj_kernel(x_ref, g_ref, w_ref, wl_ref, wvt_ref, o_ref, ol_ref, ovt_ref, h_ref):
    @pl.when(pl.program_id(1) == 0)
    def _():
        h = _rms_rows(x_ref[...], g_ref)
        h_ref[...] = h
        ol_ref[...] = jnp.dot(h, wl_ref[...], preferred_element_type=F32)
        vt = lax.dot_general(wvt_ref[...], h, NT_DIMS, preferred_element_type=F32).astype(BF16)
        for c in range(ovt_ref.shape[0]):
            ovt_ref[c] = vt[c * LANES:(c + 1) * LANES, :]

    res = jnp.dot(h_ref[...], w_ref[...], preferred_element_type=F32).astype(BF16)
    _store_col_blocks(o_ref, res)


def _in_proj(x2d, g, w, wl, wvt, tm, tn):
    m, k = x2d.shape
    n = w.shape[1]
    nv = wvt.shape[0]
    return pl.pallas_call(
        _in_proj_kernel,
        grid=(m // tm, n // tn),
        in_specs=[
            pl.BlockSpec((tm, k), lambda i, j: (i, 0)),
            pl.BlockSpec((1, k), lambda i, j: (0, 0)),
            pl.BlockSpec((k, tn), lambda i, j: (0, j)),
            pl.BlockSpec((k, LANES), lambda i, j: (0, 0)),
            pl.BlockSpec((nv, k), lambda i, j: (0, 0)),
        ],
        out_specs=[
            pl.BlockSpec((tn // LANES, tm, LANES), lambda i, j: (j, i, 0)),
            pl.BlockSpec((tm, LANES), lambda i, j: (i, 0)),
            pl.BlockSpec((nv // LANES, LANES, tm), lambda i, j: (0, 0, i)),
        ],
        out_shape=[
            jax.ShapeDtypeStruct((n // LANES, m, LANES), BF16),
            jax.ShapeDtypeStruct((m, LANES), F32),
            jax.ShapeDtypeStruct((nv // LANES, LANES, m), BF16),
        ],
        scratch_shapes=[pltpu.VMEM((tm, k), BF16)],
        compiler_params=_params(2),
        name="in_proj",
    )(x2d, g, w, wl, wvt)


def _norm_matmul_kernel(x_ref, g_ref, w_ref, o_ref):
    res = jnp.dot(_rms_rows(x_ref[...], g_ref), w_ref[...], preferred_element_type=F32)
    _store_col_blocks(o_ref, res.astype(BF16))


def _norm_matmul(x2d, g, w, tm):
    m, k = x2d.shape
    n = w.shape[1]
    return pl.pallas_call(
        _norm_matmul_kernel,
        grid=(m // tm,),
        in_specs=[
            pl.BlockSpec((tm, k), lambda i: (i, 0)),
            pl.BlockSpec((1, k), lambda i: (0, 0)),
            pl.BlockSpec((k, n), lambda i: (0, 0)),
        ],
        out_specs=pl.BlockSpec((n // LANES, tm, LANES), lambda i: (0, i, 0)),
        out_shape=jax.ShapeDtypeStruct((n // LANES, m, LANES), BF16),
        compiler_params=_params(1),
        name="norm_matmul",
    )(x2d, g, w)


def _forget_cumsum_kernel(logit_ref, b_ref, o_ref, carry_ref):
    @pl.when(pl.program_id(1) == 0)
    def _():
        carry_ref[...] = jnp.zeros_like(carry_ref)

    tb = logit_ref.shape[1]
    z = logit_ref[0] + b_ref[...]
    log_f = jnp.minimum(z, 0.0) - jnp.log1p(jnp.exp(-jnp.abs(z)))
    row = lax.broadcasted_iota(jnp.int32, (tb, tb), 0)
    col = lax.broadcasted_iota(jnp.int32, (tb, tb), 1)
    tri = jnp.where(col <= row, 1.0, 0.0).astype(BF16)
    c = carry_ref[...]
    for part in _split3(log_f):
        c = c + jnp.dot(tri, part, preferred_element_type=F32)
    carry_ref[...] = c[tb - 1:tb, :]
    o_ref[0] = -c


def _forget_cumsum(logit3, b_pad, tb):
    b, s, _ = logit3.shape
    return pl.pallas_call(
        _forget_cumsum_kernel,
        grid=(b, s // tb),
        in_specs=[
            pl.BlockSpec((1, tb, LANES), lambda i, j: (i, j, 0)),
            pl.BlockSpec((1, LANES), lambda i, j: (0, 0)),
        ],
        out_specs=pl.BlockSpec((1, tb, LANES), lambda i, j: (i, j, 0)),
        out_shape=jax.ShapeDtypeStruct((b, s, LANES), F32),
        scratch_shapes=[pltpu.VMEM((1, LANES), F32)],
        compiler_params=_params(2),
        name="forget_cumsum",
    )(logit3, b_pad)


def _bias_tiles_kernel(rb_ref, o_ref):
    h = pl.program_id(0)
    j = pl.program_id(1)
    bk, bq = o_ref.shape[2], o_ref.shape[3]
    key = lax.broadcasted_iota(jnp.int32, (bk, bq), 0)
    query = lax.broadcasted_iota(jnp.int32, (bk, bq), 1)
    dist = query - key + j * bk
    n = jnp.maximum(dist, 0)
    max_exact = NUM_BUCKETS // 2
    nf = jnp.maximum(n, 1).astype(F32)
    large = max_exact + (jnp.log(nf / max_exact) / math.log(MAX_DISTANCE / max_exact)
                         * (NUM_BUCKETS - max_exact)).astype(jnp.int32)
    large = jnp.minimum(large, NUM_BUCKETS - 1)
    bucket = jnp.where(n < max_exact, n, large)
    far = rb_ref[NUM_BUCKETS - 1, h]
    bias = jnp.zeros((bk, bq), F32)
    for b in range(NUM_BUCKETS - 1):
        bias = jnp.where(bucket == b, rb_ref[b, h] - far, bias)
    o_ref[0, 0] = jnp.where(dist >= 0, bias, NEG)


def _bias_tiles(rel_bias, bk, bq):
    n_heads = rel_bias.shape[1]
    return pl.pallas_call(
        _bias_tiles_kernel,
        grid=(n_heads, 2),
        in_specs=[pl.BlockSpec(memory_space=pltpu.SMEM)],
        out_specs=pl.BlockSpec((1, 1, bk, bq), lambda h, j: (h, j, 0, 0)),
        out_shape=jax.ShapeDtypeStruct((n_heads, 2, bk, bq), F32),
        compiler_params=_params(2),
        name="bias_tiles",
    )(rel_bias)


def _stack_halves(q):
    lane = lax.broadcasted_iota(jnp.int32, q.shape, 1)
    zero = jnp.zeros_like(q)
    return jnp.concatenate([jnp.where(lane < HEAD_DIM, q, zero),
                            jnp.where(lane < HEAD_DIM, zero, q)], axis=0)


def _key_block(ref, h, kb, bk):
    return ref[h, 0, pl.ds(pl.multiple_of(kb * bk, bk), bk), :]


def _q_spec(bq, cb):
    return pl.BlockSpec((N_CB, 1, bq, LANES), lambda b, i: (cb // N_CB, b, i, 0))


def _kv_spec(s, cb):
    return pl.BlockSpec((N_CB, 1, s, LANES), lambda b, i: (cb // N_CB, b, 0, 0))


def _gate_spec(bq, group):
    return pl.BlockSpec((N_CB, 1, bq, LANES), lambda b, i: (CB_GATE // N_CB + group, b, i, 0))


def _out_spec(bq):
    return pl.BlockSpec((1, bq, GROUP_W), lambda b, i: (b, i, 0))


def _store_gated(o_ref, gate_ref, h, o):
    o_ref[0, :, h * LANES:(h + 1) * LANES] = (o * _silu(gate_ref[h, 0])).astype(o_ref.dtype)


def _vt_block(vt_ref, h, kb, bk):
    vt = vt_ref[h, :, pl.ds(pl.multiple_of(kb * bk, bk), bk)]
    return jnp.concatenate([vt, jnp.ones((BF16_ROWS, bk), BF16)], axis=0)


def _scores_t(q_st, k_ref, h, kb, bk):
    return lax.dot_general(_key_block(k_ref, h, kb, bk), q_st, NT_DIMS,
                           preferred_element_type=F32)


VALUE_LAG = 2


def _softmax_t_blocks(items, carries):
    ms = [None if carries is None else carries[h][0] for h in range(N_CB)]
    accs = [None if carries is None else carries[h][1] for h in range(N_CB)]
    pending = {}

    def score(i):
        h, score_fn, _ = items[i]
        s = score_fn()
        m = jnp.max(s, axis=0, keepdims=True)
        m_new = m if ms[h] is None else jnp.maximum(ms[h], m)
        pending[i] = (ms[h], m_new, jnp.exp(s - m_new).astype(BF16))
        ms[h] = m_new

    def value(i):
        h, _, vt_fn = items[i]
        m_old, m_new, p = pending.pop(i)
        pv = jnp.dot(vt_fn(), p, preferred_element_type=F32)
        accs[h] = pv if accs[h] is None else jnp.exp(m_old - m_new) * accs[h] + pv

    for i in range(len(items) + VALUE_LAG):
        if i < len(items):
            score(i)
        if i >= VALUE_LAG:
            value(i - VALUE_LAG)
    return tuple((ms[h], accs[h]) for h in range(N_CB))


def _pair_loop(n_blocks, block_of, step, carries):
    carries = lax.fori_loop(
        0, n_blocks // 2, lambda i, c: step((block_of(2 * i), block_of(2 * i + 1)), c), carries)
    return lax.fori_loop(
        0, n_blocks % 2, lambda i, c: step((block_of(n_blocks - 1),), c), carries)


def _normalized_t(acc):
    return acc[:LANES] * (1.0 / acc[LANES:LANES + 1])


def _diff_kernel(lam_init, q_ref, k_ref, vt_ref, gate_ref, tile_ref, lamp_ref, hn_ref, o_ref):
    bq = q_ref.shape[2]
    qi = pl.program_id(1)
    qs = [_stack_halves(q_ref[h, 0] * (HEAD_DIM ** -0.5)) for h in range(N_CB)]

    def item(h, kb, tile_fn=None):
        def score():
            s = _scores_t(qs[h], k_ref, h, kb, bq)
            if tile_fn is None:
                return s
            tile = tile_fn(h)
            return s + jnp.concatenate([tile, tile], axis=1)
        return h, score, lambda: _vt_block(vt_ref, h, kb, bq)

    def step(kbs, carries):
        return _softmax_t_blocks([item(h, kb) for kb in kbs for h in range(N_CB)], carries)

    near = ([item(h, qi, lambda h: tile_ref[h, 0]) for h in range(N_CB)]
            + [item(h, jnp.maximum(qi - 1, 0), lambda h: jnp.where(qi >= 1, tile_ref[h, 1], NEG))
               for h in range(N_CB)])
    carries = _softmax_t_blocks(near, None)
    carries = _pair_loop(jnp.maximum(qi - 1, 0), lambda i: i, step, carries)

    lp = lamp_ref[...]
    lam = (jnp.exp(jnp.sum(lp[0:1] * lp[1:2], axis=-1, keepdims=True))
           - jnp.exp(jnp.sum(lp[2:3] * lp[3:4], axis=-1, keepdims=True)) + lam_init)
    for h in range(N_CB):
        a = _normalized_t(carries[h][1])
        o = (a[:, :bq] - lam * a[:, bq:]).T
        o = o * lax.rsqrt(jnp.mean(o * o, axis=-1, keepdims=True) + EPS)
        _store_gated(o_ref, gate_ref, h, o * hn_ref[...] * (1.0 - lam_init))


def _diff_attention(proj4, vt, tiles, lam_params, head_norm, lam_init, bq):
    _, b, s, _ = proj4.shape
    return pl.pallas_call(
        functools.partial(_diff_kernel, lam_init),
        grid=(b, s // bq),
        in_specs=[
            _q_spec(bq, CB_AQ), _kv_spec(s, CB_AK),
            pl.BlockSpec((N_CB, LANES, s), lambda b, i: (VT_A, 0, b)),
            _gate_spec(bq, 0),
            pl.BlockSpec((N_CB, 2, bq, bq), lambda b, i: (0, 0, 0, 0)),
            pl.BlockSpec((4, HEAD_DIM), lambda b, i: (0, 0)),
            pl.BlockSpec((1, LANES), lambda b, i: (0, 0)),
        ],
        out_specs=_out_spec(bq),
        out_shape=jax.ShapeDtypeStruct((b, s, GROUP_W), BF16),
        compiler_params=_params(2),
        name="diff_attention",
    )(proj4, proj4, vt, proj4, tiles, lam_params, head_norm)


def _stick_kernel(q_ref, k_ref, vt_ref, gate_ref, o_ref):
    bq = q_ref.shape[2]
    qi = pl.program_id(1)
    qs = [_stack_halves(q_ref[h, 0] * (HEAD_DIM ** -0.5)) for h in range(N_CB)]
    key = lax.broadcasted_iota(jnp.int32, (bq, 2 * bq), 0)
    query = lax.broadcasted_iota(jnp.int32, (bq, 2 * bq), 1) & (bq - 1)
    strict = key < query

    row = lax.broadcasted_iota(jnp.int32, (bq + BF16_ROWS, bq), 0)
    col = lax.broadcasted_iota(jnp.int32, (bq + BF16_ROWS, bq), 1)
    after = jnp.where((col > row) | (row >= bq), 1.0, 0.0).astype(BF16)
    after2 = jnp.concatenate([after, after], axis=1)

    def step(kbs, carries, mask=None):
        items = [(h, kb) for kb in kbs for h in range(N_CB)]
        rests = [carries[h][0] for h in range(N_CB)]
        accs = [carries[h][1] for h in range(N_CB)]
        pending = {}

        def score(i):
            h, kb = items[i]
            z = _scores_t(qs[h], k_ref, h, kb, bq)
            sp = jnp.log(1.0 + jnp.exp2(jnp.abs(z) * -LOG2E))
            ls_pos = jnp.minimum(z, 0.0) - sp
            ls_neg = ls_pos - z
            if mask is not None:
                ls_neg = jnp.where(mask, ls_neg, 0.0)
            pending[i] = (ls_pos, jnp.concatenate(_split2(ls_neg), axis=0))

        def cumsum(i):
            h, _ = items[i]
            ls_pos, hi_lo = pending[i]
            t = jnp.dot(after2, hi_lo, preferred_element_type=F32)
            a = jnp.exp2((ls_pos + t[:bq]) * LOG2E)
            if mask is not None:
                a = jnp.where(mask, a, 0.0)
            pending[i] = (a.astype(BF16), jnp.exp2(rests[h] * LOG2E))
            rests[h] = rests[h] + t[bq:bq + 1]

        def value(i):
            h, kb = items[i]
            a, scale = pending.pop(i)
            vt = vt_ref[h, :, pl.ds(pl.multiple_of(kb * bq, bq), bq)]
            accs[h] = accs[h] + scale * jnp.dot(vt, a, preferred_element_type=F32)

        for i in range(len(items) + 2):
            if i < len(items):
                score(i)
            if 1 <= i <= len(items):
                cumsum(i - 1)
            if i >= 2:
                value(i - 2)
        return tuple((rests[h], accs[h]) for h in range(N_CB))

    zero = (jnp.zeros((1, 2 * bq), F32), jnp.zeros((LANES, 2 * bq), F32))
    carries = step((qi,), (zero,) * N_CB, strict)
    carries = _pair_loop(qi, lambda i: qi - 1 - i, step, carries)
    for h in range(N_CB):
        acc = carries[h][1]
        o = jnp.concatenate([acc[:HEAD_DIM, :bq], acc[HEAD_DIM:, bq:]], axis=0).T
        _store_gated(o_ref, gate_ref, h, o)


def _stick_attention(proj4, vt, bq):
    _, b, s, _ = proj4.shape
    return pl.pallas_call(
        _stick_kernel,
        grid=(b, s // bq),
        in_specs=[
            _q_spec(bq, CB_SQ), _kv_spec(s, CB_SK),
            pl.BlockSpec((N_CB, LANES, s), lambda b, i: (VT_S, 0, b)),
            _gate_spec(bq, 1),
        ],
        out_specs=_out_spec(bq),
        out_shape=jax.ShapeDtypeStruct((b, s, GROUP_W), BF16),
        compiler_params=_params(2),
        name="stick_attention",
    )(proj4, proj4, vt, proj4)


def _fox_kernel(q_ref, k_ref, vt_ref, gate_ref, negc_ref, o_ref):
    bq = q_ref.shape[2]
    qi = pl.program_id(1)
    qs = [_stack_halves(q_ref[h, 0] * (HEAD_DIM ** -0.5)) for h in range(N_CB)]
    key = lax.broadcasted_iota(jnp.int32, (bq, 2 * bq), 0)
    query = lax.broadcasted_iota(jnp.int32, (bq, 2 * bq), 1) & (bq - 1)
    causal = key <= query

    def item(h, kb, mask):
        def score():
            nc = negc_ref[0, pl.ds(pl.multiple_of(kb * bq, bq), bq), :]
            bias = jnp.concatenate(
                [jnp.broadcast_to(nc[:, 2 * h:2 * h + 1], (bq, bq)),
                 jnp.broadcast_to(nc[:, 2 * h + 1:2 * h + 2], (bq, bq))], axis=1)
            s = _scores_t(qs[h], k_ref, h, kb, bq) + bias
            return s if mask is None else jnp.where(mask, s, NEG)
        return h, score, lambda: _vt_block(vt_ref, h, kb, bq)

    def step(kbs, carries, mask=None):
        return _softmax_t_blocks([item(h, kb, mask) for kb in kbs for h in range(N_CB)], carries)

    carries = _pair_loop(qi, lambda i: i, step, step((qi,), None, causal))
    for h in range(N_CB):
        a = _normalized_t(carries[h][1])
        o = jnp.concatenate([a[:HEAD_DIM, :bq], a[HEAD_DIM:, bq:]], axis=0).T
        _store_gated(o_ref, gate_ref, h, o)


def _fox_attention(proj4, vt, negc, bq):
    _, b, s, _ = proj4.shape
    return pl.pallas_call(
        _fox_kernel,
        grid=(b, s // bq),
        in_specs=[
            _q_spec(bq, CB_FQ), _kv_spec(s, CB_FK),
            pl.BlockSpec((N_CB, LANES, s), lambda b, i: (VT_F, 0, b)),
            _gate_spec(bq, 2),
            pl.BlockSpec((1, s, LANES), lambda b, i: (b, 0, 0)),
        ],
        out_specs=_out_spec(bq),
        out_shape=jax.ShapeDtypeStruct((b, s, GROUP_W), BF16),
        compiler_params=_params(2),
        name="fox_attention",
    )(proj4, proj4, vt, proj4, negc)


def _mem_kernel(q_ref, km_ref, vm_ref, gate_ref, o_ref):
    for h in range(N_CB):
        s = lax.dot_general(q_ref[h, 0], km_ref[h, 0], NT_DIMS, preferred_element_type=F32)
        s = s * (LANES ** -0.5)
        m = jnp.max(s, axis=-1, keepdims=True)
        p = jnp.exp(s - m)
        l = jnp.sum(p, axis=-1, keepdims=True)
        o = jnp.dot(p.astype(BF16), vm_ref[h, 0], preferred_element_type=F32) / l
        _store_gated(o_ref, gate_ref, h, o)


def _mem_attention(proj4, kv4, bq):
    _, b, s, _ = proj4.shape
    mlen = kv4.shape[2]
    return pl.pallas_call(
        _mem_kernel,
        grid=(b, s // bq),
        in_specs=[
            _q_spec(bq, CB_MQ),
            pl.BlockSpec((N_CB, 1, mlen, LANES), lambda b, i: (0, b, 0, 0)),
            pl.BlockSpec((N_CB, 1, mlen, LANES), lambda b, i: (1, b, 0, 0)),
            _gate_spec(bq, 3),
        ],
        out_specs=_out_spec(bq),
        out_shape=jax.ShapeDtypeStruct((b, s, GROUP_W), BF16),
        compiler_params=_params(2),
        name="mem_attention",
    )(proj4, kv4, kv4, proj4)


def _out_proj_kernel(ya_ref, yb_ref, yc_ref, ym_ref, w_ref, g_ref, x_ref, o_ref):
    y = jnp.concatenate([ya_ref[...], yb_ref[...], yc_ref[...], ym_ref[...]], axis=-1)
    z = jnp.dot(y, w_ref[...], preferred_element_type=F32)
    z = z * lax.rsqrt(jnp.mean(z * z, axis=-1, keepdims=True) + EPS)
    o_ref[...] = x_ref[...] + z * g_ref[...]


def _out_proj(ys, w, g, x2d, tm):
    m, d = x2d.shape
    y_spec = pl.BlockSpec((tm, GROUP_W), lambda i: (i, 0))
    return pl.pallas_call(
        _out_proj_kernel,
        grid=(m // tm,),
        in_specs=[y_spec] * 4 + [
            pl.BlockSpec(w.shape, lambda i: (0, 0)),
            pl.BlockSpec((1, d), lambda i: (0, 0)),
            pl.BlockSpec((tm, d), lambda i: (i, 0)),
        ],
        out_specs=pl.BlockSpec((tm, d), lambda i: (i, 0)),
        out_shape=jax.ShapeDtypeStruct((m, d), F32),
        compiler_params=_params(1),
        name="out_proj",
    )(*ys, w, g, x2d)


def _pick(n, cands):
    for c in cands:
        if n % c == 0:
            return c
    raise ValueError(f"no tile in {cands} divides {n}")


def kernel(x, mem, w_in, b_forget, w_mem_kv, w_out, pre_norm, post_norm, mem_norm,
           diff_lambda, diff_head_norm, rel_bias):
    b, s, d = x.shape
    mlen = mem.shape[1]
    depth = w_in.shape[0]
    m = b * s
    bq = _pick(s, (256, 128))
    tm = _pick(m, (512, 256))

    gw = GROUP_W
    logit_lo, logit_hi = 9 * gw, 9 * gw + N_FOX_HEADS
    tiles = _bias_tiles(rel_bias, bq, bq)
    mem2d = mem.reshape(b * mlen, d)
    x2d = x.reshape(m, d)

    for l in range(depth):
        wl = w_in[l]
        w_main = jnp.concatenate([wl[:, 0:2 * gw], wl[:, 3 * gw:5 * gw], wl[:, 6 * gw:8 * gw],
                                  wl[:, logit_hi:]], axis=1).astype(BF16)
        w_vt = jnp.concatenate([wl[:, 2 * gw:3 * gw], wl[:, 5 * gw:6 * gw], wl[:, 8 * gw:9 * gw]],
                               axis=1).T.astype(BF16)
        w_logit = jnp.pad(wl[:, logit_lo:logit_hi], ((0, 0), (0, LANES - N_FOX_HEADS))).astype(BF16)
        b_pad = jnp.pad(b_forget[l], (0, LANES - N_FOX_HEADS)).reshape(1, LANES)

        tn = _pick(w_main.shape[1], (2816, 1408, 512))
        proj, logit, vt = _in_proj(x2d, pre_norm[l].reshape(1, d), w_main, w_logit, w_vt, tm, tn)
        proj4 = proj.reshape(N_PROJ_CB, b, s, LANES)
        negc = _forget_cumsum(logit.reshape(b, s, LANES), b_pad, _pick(s, (256, 128)))

        kv = _norm_matmul(mem2d, mem_norm[l].reshape(1, d), w_mem_kv[l].astype(BF16),
                          _pick(b * mlen, (1024, 512, 256)))
        kv4 = kv.reshape(kv.shape[0], b, mlen, LANES)

        lam_init = 0.8 - 0.6 * math.exp(-0.3 * l)
        y_a = _diff_attention(proj4, vt, tiles, diff_lambda[l],
                              diff_head_norm[l].reshape(1, LANES), lam_init, bq)
        y_b = _stick_attention(proj4, vt, bq)
        y_c = _fox_attention(proj4, vt, negc, bq)
        y_m = _mem_attention(proj4, kv4, _pick(s, (512, 256, 128)))
        ys = [y.reshape(m, GROUP_W) for y in (y_a, y_b, y_c, y_m)]
        x2d = _out_proj(ys, w_out[l].astype(BF16), post_norm[l].reshape(1, d), x2d,
                        _pick(m, (512, 256)))
    return x2d.reshape(b, s, d)
```

```python
import functools
import math

import jax
import jax.numpy as jnp
from jax import lax
from jax.experimental import pallas as pl
from jax.experimental.pallas import tpu as pltpu

F32 = jnp.float32
BF16 = jnp.bfloat16

LANES = 128
BF16_ROWS = 16
HEAD_DIM = 64
GROUP_W = 512
N_CB = GROUP_W // LANES
N_FOX_HEADS = GROUP_W // HEAD_DIM
NUM_BUCKETS = 32
MAX_DISTANCE = 128
EPS = 1e-6
NEG = -1e30
LOG2E = 1.4426950408889634
VMEM_LIMIT = 56 * 1024 * 1024

CB_AQ, CB_AK = 0, 4
CB_SQ, CB_SK = 8, 12
CB_FQ, CB_FK = 16, 20
CB_MQ, CB_GATE = 24, 28
N_PROJ_CB = 44
VT_A, VT_S, VT_F = 0, 1, 2

NT_DIMS = (((1,), (1,)), ((), ()))


def _params(n_axes):
    return pltpu.CompilerParams(dimension_semantics=("arbitrary",) * n_axes,
                                vmem_limit_bytes=VMEM_LIMIT)


def _split3(x):
    hi = x.astype(BF16)
    r = x - hi.astype(F32)
    mid = r.astype(BF16)
    lo = (r - mid.astype(F32)).astype(BF16)
    return hi, mid, lo


def _silu(g):
    g = g.astype(F32)
    return g / (1.0 + jnp.exp(-g))


def _rms_rows(x, g_ref):
    y = x * lax.rsqrt(jnp.mean(x * x, axis=-1, keepdims=True) + EPS)
    return (y * g_ref[...]).astype(BF16)


def _store_col_blocks(o_ref, res):
    for c in range(o_ref.shape[0]):
        o_ref[c] = res[:, c * LANES:(c + 1) * LANES]


def _in_proj_kernel(x_ref, g_ref, w_ref, wl_ref, wvt_ref, o_ref, ol_ref, ovt_ref, h_ref):
    @pl.when(pl.program_id(1) == 0)
    def _():
        h = _rms_rows(x_ref[...], g_ref)
        h_ref[...] = h
        ol_ref[...] = jnp.dot(h, wl_ref[...], preferred_element_type=F32)
        vt = lax.dot_general(wvt_ref[...], h, NT_DIMS, preferred_element_type=F32).astype(BF16)
        for c in range(ovt_ref.shape[0]):
            ovt_ref[c] = vt[c * LANES:(c + 1) * LANES, :]

    res = jnp.dot(h_ref[...], w_ref[...], preferred_element_type=F32).astype(BF16)
    _store_col_blocks(o_ref, res)


def _in_proj(x2d, g, w, wl, wvt, tm, tn):
    m, k = x2d.shape
    n = w.shape[1]
    nv = wvt.shape[0]
    return pl.pallas_call(
        _in_proj_kernel,
        grid=(m // tm, n // tn),
        in_specs=[
            pl.BlockSpec((tm, k), lambda i, j: (i, 0)),
            pl.BlockSpec((1, k), lambda i, j: (0, 0)),
            pl.BlockSpec((k, tn), lambda i, j: (0, j)),
            pl.BlockSpec((k, LANES), lambda i, j: (0, 0)),
            pl.BlockSpec((nv, k), lambda i, j: (0, 0)),
        ],
        out_specs=[
            pl.BlockSpec((tn // LANES, tm, LANES), lambda i, j: (j, i, 0)),
            pl.BlockSpec((tm, LANES), lambda i, j: (i, 0)),
            pl.BlockSpec((nv // LANES, LANES, tm), lambda i, j: (0, 0, i)),
        ],
        out_shape=[
            jax.ShapeDtypeStruct((n // LANES, m, LANES), BF16),
            jax.ShapeDtypeStruct((m, LANES), F32),
            jax.ShapeDtypeStruct((nv // LANES, LANES, m), BF16),
        ],
        scratch_shapes=[pltpu.VMEM((tm, k), BF16)],
        compiler_params=_params(2),
        name="in_proj",
    )(x2d, g, w, wl, wvt)


def _norm_matmul_kernel(x_ref, g_ref, w_ref, o_ref):
    res = jnp.dot(_rms_rows(x_ref[...], g_ref), w_ref[...], preferred_element_type=F32)
    _store_col_blocks(o_ref, res.astype(BF16))


def _norm_matmul(x2d, g, w, tm):
    m, k = x2d.shape
    n = w.shape[1]
    return pl.pallas_call(
        _norm_matmul_kernel,
        grid=(m // tm,),
        in_specs=[
            pl.BlockSpec((tm, k), lambda i: (i, 0)),
            pl.BlockSpec((1, k), lambda i: (0, 0)),
            pl.BlockSpec((k, n), lambda i: (0, 0)),
        ],
        out_specs=pl.BlockSpec((n // LANES, tm, LANES), lambda i: (0, i, 0)),
        out_shape=jax.ShapeDtypeStruct((n // LANES, m, LANES), BF16),
        compiler_params=_params(1),
        name="norm_matmul",
    )(x2d, g, w)


def _forget_cumsum_kernel(logit_ref, b_ref, o_ref, carry_ref):
    @pl.when(pl.program_id(1) == 0)
    def _():
        carry_ref[...] = jnp.zeros_like(carry_ref)

    tb = logit_ref.shape[1]
    z = logit_ref[0] + b_ref[...]
    log_f = jnp.minimum(z, 0.0) - jnp.log1p(jnp.exp(-jnp.abs(z)))
    row = lax.broadcasted_iota(jnp.int32, (tb, tb), 0)
    col = lax.broadcasted_iota(jnp.int32, (tb, tb), 1)
    tri = jnp.where(col <= row, 1.0, 0.0).astype(BF16)
    c = carry_ref[...]
    for part in _split3(log_f):
        c = c + jnp.dot(tri, part, preferred_element_type=F32)
    carry_ref[...] = c[tb - 1:tb, :]
    o_ref[0] = -c


def _forget_cumsum(logit3, b_pad, tb):
    b, s, _ = logit3.shape
    return pl.pallas_call(
        _forget_cumsum_kernel,
        grid=(b, s // tb),
        in_specs=[
            pl.BlockSpec((1, tb, LANES), lambda i, j: (i, j, 0)),
            pl.BlockSpec((1, LANES), lambda i, j: (0, 0)),
        ],
        out_specs=pl.BlockSpec((1, tb, LANES), lambda i, j: (i, j, 0)),
        out_shape=jax.ShapeDtypeStruct((b, s, LANES), F32),
        scratch_shapes=[pltpu.VMEM((1, LANES), F32)],
        compiler_params=_params(2),
        name="forget_cumsum",
    )(logit3, b_pad)


def _bias_tiles_kernel(rb_ref, o_ref):
    h = pl.program_id(0)
    j = pl.program_id(1)
    bk, bq = o_ref.shape[2], o_ref.shape[3]
    key = lax.broadcasted_iota(jnp.int32, (bk, bq), 0)
    query = lax.broadcasted_iota(jnp.int32, (bk, bq), 1)
    dist = query - key + j * bk
    n = jnp.maximum(dist, 0)
    max_exact = NUM_BUCKETS // 2
    nf = jnp.maximum(n, 1).astype(F32)
    large = max_exact + (jnp.log(nf / max_exact) / math.log(MAX_DISTANCE / max_exact)
                         * (NUM_BUCKETS - max_exact)).astype(jnp.int32)
    large = jnp.minimum(large, NUM_BUCKETS - 1)
    bucket = jnp.where(n < max_exact, n, large)
    far = rb_ref[NUM_BUCKETS - 1, h]
    bias = jnp.zeros((bk, bq), F32)
    for b in range(NUM_BUCKETS - 1):
        bias = jnp.where(bucket == b, rb_ref[b, h] - far, bias)
    o_ref[0, 0] = jnp.where(dist >= 0, bias, NEG)


def _bias_tiles(rel_bias, bk, bq):
    n_heads = rel_bias.shape[1]
    return pl.pallas_call(
        _bias_tiles_kernel,
        grid=(n_heads, 2),
        in_specs=[pl.BlockSpec(memory_space=pltpu.SMEM)],
        out_specs=pl.BlockSpec((1, 1, bk, bq), lambda h, j: (h, j, 0, 0)),
        out_shape=jax.ShapeDtypeStruct((n_heads, 2, bk, bq), F32),
        compiler_params=_params(2),
        name="bias_tiles",
    )(rel_bias)


def _stack_halves(q):
    lane = lax.broadcasted_iota(jnp.int32, q.shape, 1)
    zero = jnp.zeros_like(q)
    return jnp.concatenate([jnp.where(lane < HEAD_DIM, q, zero),
                            jnp.where(lane < HEAD_DIM, zero, q)], axis=0)


def _key_block(ref, h, kb, bk):
    return ref[h, 0, pl.ds(pl.multiple_of(kb * bk, bk), bk), :]


def _q_spec(bq, cb):
    return pl.BlockSpec((N_CB, 1, bq, LANES), lambda b, i: (cb // N_CB, b, i, 0))


def _kv_spec(s, cb):
    return pl.BlockSpec((N_CB, 1, s, LANES), lambda b, i: (cb // N_CB, b, 0, 0))


def _gate_spec(bq, group):
    return pl.BlockSpec((N_CB, 1, bq, LANES), lambda b, i: (CB_GATE // N_CB + group, b, i, 0))


def _out_spec(bq):
    return pl.BlockSpec((1, bq, GROUP_W), lambda b, i: (b, i, 0))


def _store_gated(o_ref, gate_ref, h, o):
    o_ref[0, :, h * LANES:(h + 1) * LANES] = (o * _silu(gate_ref[h, 0])).astype(o_ref.dtype)


def _vt_block(vt_ref, h, kb, bk):
    vt = vt_ref[h, :, pl.ds(pl.multiple_of(kb * bk, bk), bk)]
    return jnp.concatenate([vt, jnp.ones((BF16_ROWS, bk), BF16)], axis=0)


def _scores_t(q_st, k_ref, h, kb, bk):
    return lax.dot_general(_key_block(k_ref, h, kb, bk), q_st, NT_DIMS,
                           preferred_element_type=F32)


VALUE_LAG = 2


def _softmax_t_blocks(items, carries):
    ms = [None if carries is None else carries[h][0] for h in range(N_CB)]
    accs = [None if carries is None else carries[h][1] for h in range(N_CB)]
    pending = {}

    def score(i):
        h, score_fn, _ = items[i]
        s = score_fn()
        m = jnp.max(s, axis=0, keepdims=True)
        m_new = m if ms[h] is None else jnp.maximum(ms[h], m)
        pending[i] = (ms[h], m_new, jnp.exp(s - m_new).astype(BF16))
        ms[h] = m_new

    def value(i):
        h, _, vt_fn = items[i]
        m_old, m_new, p = pending.pop(i)
        pv = jnp.dot(vt_fn(), p, preferred_element_type=F32)
        accs[h] = pv if accs[h] is None else jnp.exp(m_old - m_new) * accs[h] + pv

    for i in range(len(items) + VALUE_LAG):
        if i < len(items):
            score(i)
        if i >= VALUE_LAG:
            value(i - VALUE_LAG)
    return tuple((ms[h], accs[h]) for h in range(N_CB))


def _pair_loop(n_blocks, block_of, step, carries):
    carries = lax.fori_loop(
        0, n_blocks // 2, lambda i, c: step((block_of(2 * i), block_of(2 * i + 1)), c), carries)
    return lax.fori_loop(
        0, n_blocks % 2, lambda i, c: step((block_of(n_blocks - 1),), c), carries)


def _normalized_t(acc):
    return acc[:LANES] * (1.0 / acc[LANES:LANES + 1])


def _diff_kernel(lam_init, q_ref, k_ref, vt_ref, gate_ref, tile_ref, lamp_ref, hn_ref, o_ref):
    bq = q_ref.shape[2]
    qi = pl.program_id(1)
    qs = [_stack_halves(q_ref[h, 0] * (HEAD_DIM ** -0.5)) for h in range(N_CB)]

    def item(h, kb, tile_fn=None):
        def score():
            s = _scores_t(qs[h], k_ref, h, kb, bq)
            if tile_fn is None:
                return s
            tile = tile_fn(h)
            return s + jnp.concatenate([tile, tile], axis=1)
        return h, score, lambda: _vt_block(vt_ref, h, kb, bq)

    def step(kbs, carries):
        return _softmax_t_blocks([item(h, kb) for kb in kbs for h in range(N_CB)], carries)

    near = ([item(h, qi, lambda h: tile_ref[h, 0]) for h in range(N_CB)]
            + [item(h, jnp.maximum(qi - 1, 0), lambda h: jnp.where(qi >= 1, tile_ref[h, 1], NEG))
               for h in range(N_CB)])
    carries = _softmax_t_blocks(near, None)
    carries = _pair_loop(jnp.maximum(qi - 1, 0), lambda i: i, step, carries)

    lp = lamp_ref[...]
    lam = (jnp.exp(jnp.sum(lp[0:1] * lp[1:2], axis=-1, keepdims=True))
           - jnp.exp(jnp.sum(lp[2:3] * lp[3:4], axis=-1, keepdims=True)) + lam_init)
    for h in range(N_CB):
        a = _normalized_t(carries[h][1])
        o = (a[:, :bq] - lam * a[:, bq:]).T
        o = o * lax.rsqrt(jnp.mean(o * o, axis=-1, keepdims=True) + EPS)
        _store_gated(o_ref, gate_ref, h, o * hn_ref[...] * (1.0 - lam_init))


def _diff_attention(proj4, vt, tiles, lam_params, head_norm, lam_init, bq):
    _, b, s, _ = proj4.shape
    return pl.pallas_call(
        functools.partial(_diff_kernel, lam_init),
        grid=(b, s // bq),
        in_specs=[
            _q_spec(bq, CB_AQ), _kv_spec(s, CB_AK),
            pl.BlockSpec((N_CB, LANES, s), lambda b, i: (VT_A, 0, b)),
            _gate_spec(bq, 0),
            pl.BlockSpec((N_CB, 2, bq, bq), lambda b, i: (0, 0, 0, 0)),
            pl.BlockSpec((4, HEAD_DIM), lambda b, i: (0, 0)),
            pl.BlockSpec((1, LANES), lambda b, i: (0, 0)),
        ],
        out_specs=_out_spec(bq),
        out_shape=jax.ShapeDtypeStruct((b, s, GROUP_W), BF16),
        compiler_params=_params(2),
        name="diff_attention",
    )(proj4, proj4, vt, proj4, tiles, lam_params, head_norm)


def _stick_kernel(q_ref, k_ref, vt_ref, gate_ref, o_ref):
    bq = q_ref.shape[2]
    qi = pl.program_id(1)
    qs = [_stack_halves(q_ref[h, 0] * (HEAD_DIM ** -0.5)) for h in range(N_CB)]
    key = lax.broadcasted_iota(jnp.int32, (bq, 2 * bq), 0)
    query = lax.broadcasted_iota(jnp.int32, (bq, 2 * bq), 1) & (bq - 1)
    strict = key < query

    row = lax.broadcasted_iota(jnp.int32, (bq + BF16_ROWS, bq), 0)
    col = lax.broadcasted_iota(jnp.int32, (bq + BF16_ROWS, bq), 1)
    after = jnp.where((col > row) | (row >= bq), 1.0, 0.0).astype(BF16)

    def step(kbs, carries, mask=None):
        items = [(h, kb) for kb in kbs for h in range(N_CB)]
        rests = [carries[h][0] for h in range(N_CB)]
        accs = [carries[h][1] for h in range(N_CB)]
        pending = {}

        def score(i):
            h, kb = items[i]
            z = _scores_t(qs[h], k_ref, h, kb, bq)
            sp = jnp.log(1.0 + jnp.exp2(jnp.abs(z) * -LOG2E))
            ls_pos = jnp.minimum(z, 0.0) - sp
            ls_neg = ls_pos - z
            if mask is not None:
                ls_neg = jnp.where(mask, ls_neg, 0.0)
            pending[i] = (ls_pos, ls_neg.astype(BF16))

        def cumsum(i):
            h, _ = items[i]
            ls_pos, log_rest = pending[i]
            t = jnp.dot(after, log_rest, preferred_element_type=F32)
            a = jnp.exp2((ls_pos + t[:bq]) * LOG2E)
            if mask is not None:
                a = jnp.where(mask, a, 0.0)
            pending[i] = (a.astype(BF16), jnp.exp2(rests[h] * LOG2E))
            rests[h] = rests[h] + t[bq:bq + 1]

        def value(i):
            h, kb = items[i]
            a, scale = pending.pop(i)
            vt = vt_ref[h, :, pl.ds(pl.multiple_of(kb * bq, bq), bq)]
            accs[h] = accs[h] + scale * jnp.dot(vt, a, preferred_element_type=F32)

        for i in range(len(items) + 2):
            if i < len(items):
                score(i)
            if 1 <= i <= len(items):
                cumsum(i - 1)
            if i >= 2:
                value(i - 2)
        return tuple((rests[h], accs[h]) for h in range(N_CB))

    zero = (jnp.zeros((1, 2 * bq), F32), jnp.zeros((LANES, 2 * bq), F32))
    carries = step((qi,), (zero,) * N_CB, strict)
    carries = _pair_loop(qi, lambda i: qi - 1 - i, step, carries)
    for h in range(N_CB):
        acc = carries[h][1]
        o = jnp.concatenate([acc[:HEAD_DIM, :bq], acc[HEAD_DIM:, bq:]], axis=0).T
        _store_gated(o_ref, gate_ref, h, o)


def _stick_attention(proj4, vt, bq):
    _, b, s, _ = proj4.shape
    return pl.pallas_call(
        _stick_kernel,
        grid=(b, s // bq),
        in_specs=[
            _q_spec(bq, CB_SQ), _kv_spec(s, CB_SK),
            pl.BlockSpec((N_CB, LANES, s), lambda b, i: (VT_S, 0, b)),
            _gate_spec(bq, 1),
        ],
        out_specs=_out_spec(bq),
        out_shape=jax.ShapeDtypeStruct((b, s, GROUP_W), BF16),
        compiler_params=_params(2),
        name="stick_attention",
    )(proj4, proj4, vt, proj4)


def _fox_kernel(q_ref, k_ref, vt_ref, gate_ref, negc_ref, o_ref):
    bq = q_ref.shape[2]
    qi = pl.program_id(1)
    qs = [_stack_halves(q_ref[h, 0] * (HEAD_DIM ** -0.5)) for h in range(N_CB)]
    key = lax.broadcasted_iota(jnp.int32, (bq, 2 * bq), 0)
    query = lax.broadcasted_iota(jnp.int32, (bq, 2 * bq), 1) & (bq - 1)
    causal = key <= query

    def item(h, kb, mask):
        def score():
            nc = negc_ref[0, pl.ds(pl.multiple_of(kb * bq, bq), bq), :]
            bias = jnp.concatenate(
                [jnp.broadcast_to(nc[:, 2 * h:2 * h + 1], (bq, bq)),
                 jnp.broadcast_to(nc[:, 2 * h + 1:2 * h + 2], (bq, bq))], axis=1)
            s = _scores_t(qs[h], k_ref, h, kb, bq) + bias
            return s if mask is None else jnp.where(mask, s, NEG)
        return h, score, lambda: _vt_block(vt_ref, h, kb, bq)

    def step(kbs, carries, mask=None):
        return _softmax_t_blocks([item(h, kb, mask) for kb in kbs for h in range(N_CB)], carries)

    carries = _pair_loop(qi, lambda i: i, step, step((qi,), None, causal))
    for h in range(N_CB):
        a = _normalized_t(carries[h][1])
        o = jnp.concatenate([a[:HEAD_DIM, :bq], a[HEAD_DIM:, bq:]], axis=0).T
        _store_gated(o_ref, gate_ref, h, o)


def _fox_attention(proj4, vt, negc, bq):
    _, b, s, _ = proj4.shape
    return pl.pallas_call(
        _fox_kernel,
        grid=(b, s // bq),
        in_specs=[
            _q_spec(bq, CB_FQ), _kv_spec(s, CB_FK),
            pl.BlockSpec((N_CB, LANES, s), lambda b, i: (VT_F, 0, b)),
            _gate_spec(bq, 2),
            pl.BlockSpec((1, s, LANES), lambda b, i: (b, 0, 0)),
        ],
        out_specs=_out_spec(bq),
        out_shape=jax.ShapeDtypeStruct((b, s, GROUP_W), BF16),
        compiler_params=_params(2),
        name="fox_attention",
    )(proj4, proj4, vt, proj4, negc)


def _mem_kernel(q_ref, km_ref, vm_ref, gate_ref, o_ref):
    for h in range(N_CB):
        s = lax.dot_general(q_ref[h, 0], km_ref[h, 0], NT_DIMS, preferred_element_type=F32)
        s = s * (LANES ** -0.5)
        m = jnp.max(s, axis=-1, keepdims=True)
        p = jnp.exp(s - m)
        l = jnp.sum(p, axis=-1, keepdims=True)
        o = jnp.dot(p.astype(BF16), vm_ref[h, 0], preferred_element_type=F32) / l
        _store_gated(o_ref, gate_ref, h, o)


def _mem_attention(proj4, kv4, bq):
    _, b, s, _ = proj4.shape
    mlen = kv4.shape[2]
    return pl.pallas_call(
        _mem_kernel,
        grid=(b, s // bq),
        in_specs=[
            _q_spec(bq, CB_MQ),
            pl.BlockSpec((N_CB, 1, mlen, LANES), lambda b, i: (0, b, 0, 0)),
            pl.BlockSpec((N_CB, 1, mlen, LANES), lambda b, i: (1, b, 0, 0)),
            _gate_spec(bq, 3),
        ],
        out_specs=_out_spec(bq),
        out_shape=jax.ShapeDtypeStruct((b, s, GROUP_W), BF16),
        compiler_params=_params(2),
        name="mem_attention",
    )(proj4, kv4, kv4, proj4)


def _out_proj_kernel(ya_ref, yb_ref, yc_ref, ym_ref, w_ref, g_ref, x_ref, o_ref):
    y = jnp.concatenate([ya_ref[...], yb_ref[...], yc_ref[...], ym_ref[...]], axis=-1)
    z = jnp.dot(y, w_ref[...], preferred_element_type=F32)
    z = z * lax.rsqrt(jnp.mean(z * z, axis=-1, keepdims=True) + EPS)
    o_ref[...] = x_ref[...] + z * g_ref[...]


def _out_proj(ys, w, g, x2d, tm):
    m, d = x2d.shape
    y_spec = pl.BlockSpec((tm, GROUP_W), lambda i: (i, 0))
    return pl.pallas_call(
        _out_proj_kernel,
        grid=(m // tm,),
        in_specs=[y_spec] * 4 + [
            pl.BlockSpec(w.shape, lambda i: (0, 0)),
            pl.BlockSpec((1, d), lambda i: (0, 0)),
            pl.BlockSpec((tm, d), lambda i: (i, 0)),
        ],
        out_specs=pl.BlockSpec((tm, d), lambda i: (i, 0)),
        out_shape=jax.ShapeDtypeStruct((m, d), F32),
        compiler_params=_params(1),
        name="out_proj",
    )(*ys, w, g, x2d)


def _pick(n, cands):
    for c in cands:
        if n % c == 0:
            return c
    raise ValueError(f"no tile in {cands} divides {n}")


def kernel(x, mem, w_in, b_forget, w_mem_kv, w_out, pre_norm, post_norm, mem_norm,
           diff_lambda, diff_head_norm, rel_bias):
    b, s, d = x.shape
    mlen = mem.shape[1]
    depth = w_in.shape[0]
    m = b * s
    bq = _pick(s, (256, 128))
    tm = _pick(m, (1024, 512, 256))

    gw = GROUP_W
    logit_lo, logit_hi = 9 * gw, 9 * gw + N_FOX_HEADS
    tiles = _bias_tiles(rel_bias, bq, bq)
    mem2d = mem.reshape(b * mlen, d)
    x2d = x.reshape(m, d)

    for l in range(depth):
        wl = w_in[l]
        w_main = jnp.concatenate([wl[:, 0:2 * gw], wl[:, 3 * gw:5 * gw], wl[:, 6 * gw:8 * gw],
                                  wl[:, logit_hi:]], axis=1).astype(BF16)
        w_vt = jnp.concatenate([wl[:, 2 * gw:3 * gw], wl[:, 5 * gw:6 * gw], wl[:, 8 * gw:9 * gw]],
                               axis=1).T.astype(BF16)
        w_logit = jnp.pad(wl[:, logit_lo:logit_hi], ((0, 0), (0, LANES - N_FOX_HEADS))).astype(BF16)
        b_pad = jnp.pad(b_forget[l], (0, LANES - N_FOX_HEADS)).reshape(1, LANES)

        tn = _pick(w_main.shape[1], (1408, 512))
        proj, logit, vt = _in_proj(x2d, pre_norm[l].reshape(1, d), w_main, w_logit, w_vt, tm, tn)
        proj4 = proj.reshape(N_PROJ_CB, b, s, LANES)
        negc = _forget_cumsum(logit.reshape(b, s, LANES), b_pad, _pick(s, (256, 128)))

        kv = _norm_matmul(mem2d, mem_norm[l].reshape(1, d), w_mem_kv[l].astype(BF16),
                          _pick(b * mlen, (1024, 512, 256)))
        kv4 = kv.reshape(kv.shape[0], b, mlen, LANES)

        lam_init = 0.8 - 0.6 * math.exp(-0.3 * l)
        y_a = _diff_attention(proj4, vt, tiles, diff_lambda[l],
                              diff_head_norm[l].reshape(1, LANES), lam_init, bq)
        y_b = _stick_attention(proj4, vt, bq)
        y_c = _fox_attention(proj4, vt, negc, bq)
        y_m = _mem_attention(proj4, kv4, _pick(s, (512, 256, 128)))
        ys = [y.reshape(m, GROUP_W) for y in (y_a, y_b, y_c, y_m)]
        x2d = _out_proj(ys, w_out[l].astype(BF16), post_norm[l].reshape(1, d), x2d,
                        _pick(m, (512, 256)))
    return x2d.reshape(b, s, d)
```

```python
import functools
import math

import jax
import jax.numpy as jnp
from jax import lax
from jax.experimental import pallas as pl
from jax.experimental.pallas import tpu as pltpu

F32 = jnp.float32
BF16 = jnp.bfloat16

LANES = 128
BF16_ROWS = 16
HEAD_DIM = 64
GROUP_W = 512
N_CB = GROUP_W // LANES
N_FOX_HEADS = GROUP_W // HEAD_DIM
NUM_BUCKETS = 32
MAX_DISTANCE = 128
EPS = 1e-6
NEG = -1e30
LOG2E = 1.4426950408889634
VMEM_LIMIT = 56 * 1024 * 1024

CB_AQ, CB_AK = 0, 4
CB_SQ, CB_SK = 8, 12
CB_FQ, CB_FK = 16, 20
CB_MQ, CB_GATE = 24, 28
N_PROJ_CB = 44
VT_A, VT_S, VT_F = 0, 1, 2

NT_DIMS = (((1,), (1,)), ((), ()))


def _params(n_axes):
    return pltpu.CompilerParams(dimension_semantics=("arbitrary",) * n_axes,
                                vmem_limit_bytes=VMEM_LIMIT)


def _split3(x):
    hi = x.astype(BF16)
    r = x - hi.astype(F32)
    mid = r.astype(BF16)
    lo = (r - mid.astype(F32)).astype(BF16)
    return hi, mid, lo


def _silu(g):
    g = g.astype(F32)
    return g / (1.0 + jnp.exp(-g))


def _rms_rows(x, g_ref):
    y = x * lax.rsqrt(jnp.mean(x * x, axis=-1, keepdims=True) + EPS)
    return (y * g_ref[...]).astype(BF16)


def _store_col_blocks(o_ref, res):
    for c in range(o_ref.shape[0]):
        o_ref[c] = res[:, c * LANES:(c + 1) * LANES]


def _in_proj_kernel(x_ref, g_ref, w_ref, wl_ref, wvt_ref, o_ref, ol_ref, ovt_ref, h_ref):
    @pl.when(pl.program_id(1) == 0)
    def _():
        h = _rms_rows(x_ref[...], g_ref)
        h_ref[...] = h
        ol_ref[...] = jnp.dot(h, wl_ref[...], preferred_element_type=F32)
        vt = lax.dot_general(wvt_ref[...], h, NT_DIMS, preferred_element_type=F32).astype(BF16)
        for c in range(ovt_ref.shape[0]):
            ovt_ref[c] = vt[c * LANES:(c + 1) * LANES, :]

    res = jnp.dot(h_ref[...], w_ref[...], preferred_element_type=F32).astype(BF16)
    _store_col_blocks(o_ref, res)


def _in_proj(x2d, g, w, wl, wvt, tm, tn):
    m, k = x2d.shape
    n = w.shape[1]
    nv = wvt.shape[0]
    return pl.pallas_call(
        _in_proj_kernel,
        grid=(m // tm, n // tn),
        in_specs=[
            pl.BlockSpec((tm, k), lambda i, j: (i, 0)),
            pl.BlockSpec((1, k), lambda i, j: (0, 0)),
            pl.BlockSpec((k, tn), lambda i, j: (0, j)),
            pl.BlockSpec((k, LANES), lambda i, j: (0, 0)),
            pl.BlockSpec((nv, k), lambda i, j: (0, 0)),
        ],
        out_specs=[
            pl.BlockSpec((tn // LANES, tm, LANES), lambda i, j: (j, i, 0)),
            pl.BlockSpec((tm, LANES), lambda i, j: (i, 0)),
            pl.BlockSpec((nv // LANES, LANES, tm), lambda i, j: (0, 0, i)),
        ],
        out_shape=[
            jax.ShapeDtypeStruct((n // LANES, m, LANES), BF16),
            jax.ShapeDtypeStruct((m, LANES), F32),
            jax.ShapeDtypeStruct((nv // LANES, LANES, m), BF16),
        ],
        scratch_shapes=[pltpu.VMEM((tm, k), BF16)],
        compiler_params=_params(2),
        name="in_proj",
    )(x2d, g, w, wl, wvt)


def _norm_matmul_kernel(x_ref, g_ref, w_ref, o_ref):
    res = jnp.dot(_rms_rows(x_ref[...], g_ref), w_ref[...], preferred_element_type=F32)
    _store_col_blocks(o_ref, res.astype(BF16))


def _norm_matmul(x2d, g, w, tm):
    m, k = x2d.shape
    n = w.shape[1]
    return pl.pallas_call(
        _norm_matmul_kernel,
        grid=(m // tm,),
        in_specs=[
            pl.BlockSpec((tm, k), lambda i: (i, 0)),
            pl.BlockSpec((1, k), lambda i: (0, 0)),
            pl.BlockSpec((k, n), lambda i: (0, 0)),
        ],
        out_specs=pl.BlockSpec((n // LANES, tm, LANES), lambda i: (0, i, 0)),
        out_shape=jax.ShapeDtypeStruct((n // LANES, m, LANES), BF16),
        compiler_params=_params(1),
        name="norm_matmul",
    )(x2d, g, w)


def _forget_cumsum_kernel(logit_ref, b_ref, o_ref, carry_ref):
    @pl.when(pl.program_id(1) == 0)
    def _():
        carry_ref[...] = jnp.zeros_like(carry_ref)

    tb = logit_ref.shape[1]
    z = logit_ref[0] + b_ref[...]
    log_f = jnp.minimum(z, 0.0) - jnp.log1p(jnp.exp(-jnp.abs(z)))
    row = lax.broadcasted_iota(jnp.int32, (tb, tb), 0)
    col = lax.broadcasted_iota(jnp.int32, (tb, tb), 1)
    tri = jnp.where(col <= row, 1.0, 0.0).astype(BF16)
    c = carry_ref[...]
    for part in _split3(log_f):
        c = c + jnp.dot(tri, part, preferred_element_type=F32)
    carry_ref[...] = c[tb - 1:tb, :]
    src = lax.broadcasted_iota(jnp.int32, (LANES, LANES), 0)
    dst = lax.broadcasted_iota(jnp.int32, (LANES, LANES), 1)
    parts = _split3(-c)
    for h in range(N_CB):
        ext = jnp.zeros((tb, LANES), F32)
        for term, part in enumerate(parts):
            place = jnp.where((src == 2 * h) & (dst == HEAD_DIM + term)
                              | (src == 2 * h + 1) & (dst == term), 1.0, 0.0).astype(BF16)
            ext = ext + jnp.dot(part, place, preferred_element_type=F32)
        o_ref[h, 0] = ext.astype(BF16)


def _forget_cumsum(logit3, b_pad, tb):
    b, s, _ = logit3.shape
    return pl.pallas_call(
        _forget_cumsum_kernel,
        grid=(b, s // tb),
        in_specs=[
            pl.BlockSpec((1, tb, LANES), lambda i, j: (i, j, 0)),
            pl.BlockSpec((1, LANES), lambda i, j: (0, 0)),
        ],
        out_specs=pl.BlockSpec((N_CB, 1, tb, LANES), lambda i, j: (0, i, j, 0)),
        out_shape=jax.ShapeDtypeStruct((N_CB, b, s, LANES), BF16),
        scratch_shapes=[pltpu.VMEM((1, LANES), F32)],
        compiler_params=_params(2),
        name="forget_cumsum",
    )(logit3, b_pad)


def _bias_tiles_kernel(rb_ref, o_ref):
    h = pl.program_id(0)
    j = pl.program_id(1)
    bk, bq = o_ref.shape[2], o_ref.shape[3]
    key = lax.broadcasted_iota(jnp.int32, (bk, bq), 0)
    query = lax.broadcasted_iota(jnp.int32, (bk, bq), 1)
    dist = query - key + j * bk
    n = jnp.maximum(dist, 0)
    max_exact = NUM_BUCKETS // 2
    nf = jnp.maximum(n, 1).astype(F32)
    large = max_exact + (jnp.log(nf / max_exact) / math.log(MAX_DISTANCE / max_exact)
                         * (NUM_BUCKETS - max_exact)).astype(jnp.int32)
    large = jnp.minimum(large, NUM_BUCKETS - 1)
    bucket = jnp.where(n < max_exact, n, large)
    far = rb_ref[NUM_BUCKETS - 1, h]
    bias = jnp.zeros((bk, bq), F32)
    for b in range(NUM_BUCKETS - 1):
        bias = jnp.where(bucket == b, rb_ref[b, h] - far, bias)
    o_ref[0, 0] = jnp.where(dist >= 0, bias, NEG)


def _bias_tiles(rel_bias, bk, bq):
    n_heads = rel_bias.shape[1]
    return pl.pallas_call(
        _bias_tiles_kernel,
        grid=(n_heads, 2),
        in_specs=[pl.BlockSpec(memory_space=pltpu.SMEM)],
        out_specs=pl.BlockSpec((1, 1, bk, bq), lambda h, j: (h, j, 0, 0)),
        out_shape=jax.ShapeDtypeStruct((n_heads, 2, bk, bq), F32),
        compiler_params=_params(2),
        name="bias_tiles",
    )(rel_bias)


def _stack_halves(q):
    lane = lax.broadcasted_iota(jnp.int32, q.shape, 1)
    zero = jnp.zeros_like(q)
    return jnp.concatenate([jnp.where(lane < HEAD_DIM, q, zero),
                            jnp.where(lane < HEAD_DIM, zero, q)], axis=0)


def _key_block(ref, h, kb, bk):
    return ref[h, 0, pl.ds(pl.multiple_of(kb * bk, bk), bk), :]


def _q_spec(bq, cb):
    return pl.BlockSpec((N_CB, 1, bq, LANES), lambda b, i: (cb // N_CB, b, i, 0))


def _kv_spec(s, cb):
    return pl.BlockSpec((N_CB, 1, s, LANES), lambda b, i: (cb // N_CB, b, 0, 0))


def _gate_spec(bq, group):
    return pl.BlockSpec((N_CB, 1, bq, LANES), lambda b, i: (CB_GATE // N_CB + group, b, i, 0))


def _out_spec(bq):
    return pl.BlockSpec((1, bq, GROUP_W), lambda b, i: (b, i, 0))


def _store_gated(o_ref, gate_ref, h, o):
    o_ref[0, :, h * LANES:(h + 1) * LANES] = (o * _silu(gate_ref[h, 0])).astype(o_ref.dtype)


def _with_ones_rows(vt):
    return jnp.concatenate([vt, jnp.ones((BF16_ROWS, vt.shape[1]), BF16)], axis=0)


def _vt_shared(vt_ref, h, kb, bk):
    vt = _with_ones_rows(vt_ref[h, :, pl.ds(pl.multiple_of(kb * bk, bk), bk)])
    return vt, vt


def _vt_per_head(vt_ref, h, kb, bk):
    vt = vt_ref[h, :, pl.ds(pl.multiple_of(kb * bk, bk), bk)]
    return _with_ones_rows(vt[:HEAD_DIM]), _with_ones_rows(vt[HEAD_DIM:])


def _value_t(vts, p):
    n = p.shape[1] // 2
    return jnp.concatenate([jnp.dot(vts[0], p[:, :n], preferred_element_type=F32),
                            jnp.dot(vts[1], p[:, n:], preferred_element_type=F32)], axis=1)


def _scores_t(q_st, k_ref, h, kb, bk):
    return lax.dot_general(_key_block(k_ref, h, kb, bk), q_st, NT_DIMS,
                           preferred_element_type=F32)


VALUE_LAG = 2


def _softmax_t_blocks(items, carries):
    ms = [None if carries is None else carries[h][0] for h in range(N_CB)]
    accs = [None if carries is None else carries[h][1] for h in range(N_CB)]
    pending = {}

    def score(i):
        h, score_fn, _ = items[i]
        s = score_fn()
        m = jnp.max(s, axis=0, keepdims=True)
        m_new = m if ms[h] is None else jnp.maximum(ms[h], m)
        pending[i] = (ms[h], m_new, jnp.exp(s - m_new).astype(BF16))
        ms[h] = m_new

    def value(i):
        h, _, vt_fn = items[i]
        m_old, m_new, p = pending.pop(i)
        pv = _value_t(vt_fn(), p)
        accs[h] = pv if accs[h] is None else jnp.exp(m_old - m_new) * accs[h] + pv

    for i in range(len(items) + VALUE_LAG):
        if i < len(items):
            score(i)
        if i >= VALUE_LAG:
            value(i - VALUE_LAG)
    return tuple((ms[h], accs[h]) for h in range(N_CB))


def _pair_loop(n_blocks, block_of, step, carries):
    carries = lax.fori_loop(
        0, n_blocks // 2, lambda i, c: step((block_of(2 * i), block_of(2 * i + 1)), c), carries)
    return lax.fori_loop(
        0, n_blocks % 2, lambda i, c: step((block_of(n_blocks - 1),), c), carries)


def _normalized_t(acc):
    d = acc.shape[0] - BF16_ROWS
    return acc[:d] * (1.0 / acc[d:d + 1])


def _diff_kernel(lam_init, q_ref, k_ref, vt_ref, gate_ref, tile_ref, lamp_ref, hn_ref, o_ref):
    bq = q_ref.shape[2]
    qi = pl.program_id(1)
    qs = [_stack_halves(q_ref[h, 0] * (HEAD_DIM ** -0.5)) for h in range(N_CB)]

    def item(h, kb, tile_fn=None):
        def score():
            s = _scores_t(qs[h], k_ref, h, kb, bq)
            if tile_fn is None:
                return s
            tile = tile_fn(h)
            return s + jnp.concatenate([tile, tile], axis=1)
        return h, score, lambda: _vt_shared(vt_ref, h, kb, bq)

    def step(kbs, carries):
        return _softmax_t_blocks([item(h, kb) for kb in kbs for h in range(N_CB)], carries)

    near = ([item(h, qi, lambda h: tile_ref[h, 0]) for h in range(N_CB)]
            + [item(h, jnp.maximum(qi - 1, 0), lambda h: jnp.where(qi >= 1, tile_ref[h, 1], NEG))
               for h in range(N_CB)])
    carries = _softmax_t_blocks(near, None)
    carries = _pair_loop(jnp.maximum(qi - 1, 0), lambda i: i, step, carries)

    lp = lamp_ref[...]
    lam = (jnp.exp(jnp.sum(lp[0:1] * lp[1:2], axis=-1, keepdims=True))
           - jnp.exp(jnp.sum(lp[2:3] * lp[3:4], axis=-1, keepdims=True)) + lam_init)
    for h in range(N_CB):
        a = _normalized_t(carries[h][1])
        o = (a[:, :bq] - lam * a[:, bq:]).T
        o = o * lax.rsqrt(jnp.mean(o * o, axis=-1, keepdims=True) + EPS)
        _store_gated(o_ref, gate_ref, h, o * hn_ref[...] * (1.0 - lam_init))


def _diff_attention(proj4, vt, tiles, lam_params, head_norm, lam_init, bq):
    _, b, s, _ = proj4.shape
    return pl.pallas_call(
        functools.partial(_diff_kernel, lam_init),
        grid=(b, s // bq),
        in_specs=[
            _q_spec(bq, CB_AQ), _kv_spec(s, CB_AK),
            pl.BlockSpec((N_CB, LANES, s), lambda b, i: (VT_A, 0, b)),
            _gate_spec(bq, 0),
            pl.BlockSpec((N_CB, 2, bq, bq), lambda b, i: (0, 0, 0, 0)),
            pl.BlockSpec((4, HEAD_DIM), lambda b, i: (0, 0)),
            pl.BlockSpec((1, LANES), lambda b, i: (0, 0)),
        ],
        out_specs=_out_spec(bq),
        out_shape=jax.ShapeDtypeStruct((b, s, GROUP_W), BF16),
        compiler_params=_params(2),
        name="diff_attention",
    )(proj4, proj4, vt, proj4, tiles, lam_params, head_norm)


def _stick_kernel(q_ref, k_ref, vt_ref, gate_ref, o_ref):
    bq = q_ref.shape[2]
    qi = pl.program_id(1)
    qs = [_stack_halves(q_ref[h, 0] * (HEAD_DIM ** -0.5)) for h in range(N_CB)]
    key = lax.broadcasted_iota(jnp.int32, (bq, 2 * bq), 0)
    query = lax.broadcasted_iota(jnp.int32, (bq, 2 * bq), 1) & (bq - 1)
    strict = key < query

    row = lax.broadcasted_iota(jnp.int32, (bq + BF16_ROWS, bq), 0)
    col = lax.broadcasted_iota(jnp.int32, (bq + BF16_ROWS, bq), 1)
    after = jnp.where((col > row) | (row >= bq), 1.0, 0.0).astype(BF16)

    def step(kbs, carries, mask=None):
        items = [(h, kb) for kb in kbs for h in range(N_CB)]
        rests = [carries[h][0] for h in range(N_CB)]
        accs = [carries[h][1] for h in range(N_CB)]
        pending = {}

        def score(i):
            h, kb = items[i]
            z = _scores_t(qs[h], k_ref, h, kb, bq)
            sp = jnp.log(1.0 + jnp.exp2(jnp.abs(z) * -LOG2E))
            ls_pos = jnp.minimum(z, 0.0) - sp
            ls_neg = ls_pos - z
            if mask is not None:
                ls_neg = jnp.where(mask, ls_neg, 0.0)
            pending[i] = (ls_pos, ls_neg.astype(BF16))

        def cumsum(i):
            h, _ = items[i]
            ls_pos, log_rest = pending[i]
            t = jnp.dot(after, log_rest, preferred_element_type=F32)
            a = jnp.exp2((ls_pos + t[:bq]) * LOG2E)
            if mask is not None:
                a = jnp.where(mask, a, 0.0)
            pending[i] = (a.astype(BF16), jnp.exp2(rests[h] * LOG2E))
            rests[h] = rests[h] + t[bq:bq + 1]

        def value(i):
            h, kb = items[i]
            a, scale = pending.pop(i)
            vt = vt_ref[h, :, pl.ds(pl.multiple_of(kb * bq, bq), bq)]
            accs[h] = accs[h] + scale * _value_t((vt[:HEAD_DIM], vt[HEAD_DIM:]), a)

        for i in range(len(items) + 2):
            if i < len(items):
                score(i)
            if 1 <= i <= len(items):
                cumsum(i - 1)
            if i >= 2:
                value(i - 2)
        return tuple((rests[h], accs[h]) for h in range(N_CB))

    zero = (jnp.zeros((1, 2 * bq), F32), jnp.zeros((HEAD_DIM, 2 * bq), F32))
    carries = step((qi,), (zero,) * N_CB, strict)
    carries = _pair_loop(qi, lambda i: qi - 1 - i, step, carries)
    for h in range(N_CB):
        acc = carries[h][1]
        _store_gated(o_ref, gate_ref, h, jnp.concatenate([acc[:, :bq], acc[:, bq:]], axis=0).T)


def _stick_attention(proj4, vt, bq):
    _, b, s, _ = proj4.shape
    return pl.pallas_call(
        _stick_kernel,
        grid=(b, s // bq),
        in_specs=[
            _q_spec(bq, CB_SQ), _kv_spec(s, CB_SK),
            pl.BlockSpec((N_CB, LANES, s), lambda b, i: (VT_S, 0, b)),
            _gate_spec(bq, 1),
        ],
        out_specs=_out_spec(bq),
        out_shape=jax.ShapeDtypeStruct((b, s, GROUP_W), BF16),
        compiler_params=_params(2),
        name="stick_attention",
    )(proj4, proj4, vt, proj4)


def _fox_kernel(q_ref, k_ref, vt_ref, gate_ref, negc_ref, o_ref):
    bq = q_ref.shape[2]
    qi = pl.program_id(1)
    key = lax.broadcasted_iota(jnp.int32, (bq, 2 * bq), 0)
    query = lax.broadcasted_iota(jnp.int32, (bq, 2 * bq), 1) & (bq - 1)
    causal = key <= query

    lane = lax.broadcasted_iota(jnp.int32, (bq, LANES), 1)
    low = lane < HEAD_DIM
    ones_low = jnp.where(lane < 3, 1.0, 0.0).astype(BF16)
    ones_high = jnp.where(low | (lane >= HEAD_DIM + 3), 0.0, 1.0).astype(BF16)
    qs = []
    for h in range(N_CB):
        q = q_ref[h, 0] * (HEAD_DIM ** -0.5)
        qs.append((jnp.where(low, q, ones_high), jnp.where(low, ones_low, q)))

    def item(h, kb, mask):
        def score():
            start = pl.multiple_of(kb * bq, bq)
            k = k_ref[h, 0, pl.ds(start, bq), :]
            c = negc_ref[h, 0, pl.ds(start, bq), :]
            s = jnp.concatenate(
                [lax.dot_general(jnp.where(low, k, c), qs[h][0], NT_DIMS,
                                 preferred_element_type=F32),
                 lax.dot_general(jnp.where(low, c, k), qs[h][1], NT_DIMS,
                                 preferred_element_type=F32)], axis=1)
            return s if mask is None else jnp.where(mask, s, NEG)
        return h, score, lambda: _vt_per_head(vt_ref, h, kb, bq)

    def step(kbs, carries, mask=None):
        return _softmax_t_blocks([item(h, kb, mask) for kb in kbs for h in range(N_CB)], carries)

    carries = _pair_loop(qi, lambda i: i, step, step((qi,), None, causal))
    for h in range(N_CB):
        a = _normalized_t(carries[h][1])
        _store_gated(o_ref, gate_ref, h, jnp.concatenate([a[:, :bq], a[:, bq:]], axis=0).T)


def _fox_attention(proj4, vt, negc, bq):
    _, b, s, _ = proj4.shape
    return pl.pallas_call(
        _fox_kernel,
        grid=(b, s // bq),
        in_specs=[
            _q_spec(bq, CB_FQ), _kv_spec(s, CB_FK),
            pl.BlockSpec((N_CB, LANES, s), lambda b, i: (VT_F, 0, b)),
            _gate_spec(bq, 2),
            pl.BlockSpec((N_CB, 1, s, LANES), lambda b, i: (0, b, 0, 0)),
        ],
        out_specs=_out_spec(bq),
        out_shape=jax.ShapeDtypeStruct((b, s, GROUP_W), BF16),
        compiler_params=_params(2),
        name="fox_attention",
    )(proj4, proj4, vt, proj4, negc)


def _mem_kernel(q_ref, km_ref, vm_ref, gate_ref, o_ref):
    for h in range(N_CB):
        s = lax.dot_general(q_ref[h, 0], km_ref[h, 0], NT_DIMS, preferred_element_type=F32)
        s = s * (LANES ** -0.5)
        m = jnp.max(s, axis=-1, keepdims=True)
        p = jnp.exp(s - m)
        l = jnp.sum(p, axis=-1, keepdims=True)
        o = jnp.dot(p.astype(BF16), vm_ref[h, 0], preferred_element_type=F32) / l
        _store_gated(o_ref, gate_ref, h, o)


def _mem_attention(proj4, kv4, bq):
    _, b, s, _ = proj4.shape
    mlen = kv4.shape[2]
    return pl.pallas_call(
        _mem_kernel,
        grid=(b, s // bq),
        in_specs=[
            _q_spec(bq, CB_MQ),
            pl.BlockSpec((N_CB, 1, mlen, LANES), lambda b, i: (0, b, 0, 0)),
            pl.BlockSpec((N_CB, 1, mlen, LANES), lambda b, i: (1, b, 0, 0)),
            _gate_spec(bq, 3),
        ],
        out_specs=_out_spec(bq),
        out_shape=jax.ShapeDtypeStruct((b, s, GROUP_W), BF16),
        compiler_params=_params(2),
        name="mem_attention",
    )(proj4, kv4, kv4, proj4)


def _out_proj_kernel(ya_ref, yb_ref, yc_ref, ym_ref, w_ref, g_ref, x_ref, o_ref):
    y = jnp.concatenate([ya_ref[...], yb_ref[...], yc_ref[...], ym_ref[...]], axis=-1)
    z = jnp.dot(y, w_ref[...], preferred_element_type=F32)
    z = z * lax.rsqrt(jnp.mean(z * z, axis=-1, keepdims=True) + EPS)
    o_ref[...] = x_ref[...] + z * g_ref[...]


def _out_proj(ys, w, g, x2d, tm):
    m, d = x2d.shape
    y_spec = pl.BlockSpec((tm, GROUP_W), lambda i: (i, 0))
    return pl.pallas_call(
        _out_proj_kernel,
        grid=(m // tm,),
        in_specs=[y_spec] * 4 + [
            pl.BlockSpec(w.shape, lambda i: (0, 0)),
            pl.BlockSpec((1, d), lambda i: (0, 0)),
            pl.BlockSpec((tm, d), lambda i: (i, 0)),
        ],
        out_specs=pl.BlockSpec((tm, d), lambda i: (i, 0)),
        out_shape=jax.ShapeDtypeStruct((m, d), F32),
        compiler_params=_params(1),
        name="out_proj",
    )(*ys, w, g, x2d)


def _pick(n, cands):
    for c in cands:
        if n % c == 0:
            return c
    raise ValueError(f"no tile in {cands} divides {n}")


def kernel(x, mem, w_in, b_forget, w_mem_kv, w_out, pre_norm, post_norm, mem_norm,
           diff_lambda, diff_head_norm, rel_bias):
    b, s, d = x.shape
    mlen = mem.shape[1]
    depth = w_in.shape[0]
    m = b * s
    bq = _pick(s, (256, 128))
    tm = _pick(m, (1024, 512, 256))

    gw = GROUP_W
    logit_lo, logit_hi = 9 * gw, 9 * gw + N_FOX_HEADS
    tiles = _bias_tiles(rel_bias, bq, bq)
    mem2d = mem.reshape(b * mlen, d)
    x2d = x.reshape(m, d)

    for l in range(depth):
        wl = w_in[l]
        w_main = jnp.concatenate([wl[:, 0:2 * gw], wl[:, 3 * gw:5 * gw], wl[:, 6 * gw:8 * gw],
                                  wl[:, logit_hi:]], axis=1).astype(BF16)
        w_vt = jnp.concatenate([wl[:, 2 * gw:3 * gw], wl[:, 5 * gw:6 * gw], wl[:, 8 * gw:9 * gw]],
                               axis=1).T.astype(BF16)
        w_logit = jnp.pad(wl[:, logit_lo:logit_hi], ((0, 0), (0, LANES - N_FOX_HEADS))).astype(BF16)
        b_pad = jnp.pad(b_forget[l], (0, LANES - N_FOX_HEADS)).reshape(1, LANES)

        tn = _pick(w_main.shape[1], (1408, 512))
        proj, logit, vt = _in_proj(x2d, pre_norm[l].reshape(1, d), w_main, w_logit, w_vt, tm, tn)
        proj4 = proj.reshape(N_PROJ_CB, b, s, LANES)
        negc = _forget_cumsum(logit.reshape(b, s, LANES), b_pad, _pick(s, (256, 128)))

        kv = _norm_matmul(mem2d, mem_norm[l].reshape(1, d), w_mem_kv[l].astype(BF16),
                          _pick(b * mlen, (1024, 512, 256)))
        kv4 = kv.reshape(kv.shape[0], b, mlen, LANES)

        lam_init = 0.8 - 0.6 * math.exp(-0.3 * l)
        y_a = _diff_attention(proj4, vt, tiles, diff_lambda[l],
                              diff_head_norm[l].reshape(1, LANES), lam_init, bq)
        y_b = _stick_attention(proj4, vt, bq)
        y_c = _fox_attention(proj4, vt, negc, bq)
        y_m = _mem_attention(proj4, kv4, _pick(s, (512, 256, 128)))
        ys = [y.reshape(m, GROUP_W) for y in (y_a, y_b, y_c, y_m)]
        x2d = _out_proj(ys, w_out[l].astype(BF16), post_norm[l].reshape(1, d), x2d,
                        _pick(m, (512, 256)))
    return x2d.reshape(b, s, d)
```

```python
import functools
import math

import jax
import jax.numpy as jnp
from jax import lax
from jax.experimental import pallas as pl
from jax.experimental.pallas import tpu as pltpu

F32 = jnp.float32
BF16 = jnp.bfloat16

LANES = 128
BF16_ROWS = 16
HEAD_DIM = 64
GROUP_W = 512
N_CB = GROUP_W // LANES
N_FOX_HEADS = GROUP_W // HEAD_DIM
NUM_BUCKETS = 32
MAX_DISTANCE = 128
EPS = 1e-6
NEG = -1e30
LOG2E = 1.4426950408889634
VMEM_LIMIT = 56 * 1024 * 1024

CB_AQ, CB_AK = 0, 4
CB_SQ, CB_SK = 8, 12
CB_FQ, CB_FK = 16, 20
CB_MQ, CB_GATE = 24, 28
N_PROJ_CB = 44
VT_A, VT_S, VT_F = 0, 1, 2

NT_DIMS = (((1,), (1,)), ((), ()))


def _params(n_axes):
    return pltpu.CompilerParams(dimension_semantics=("arbitrary",) * n_axes,
                                vmem_limit_bytes=VMEM_LIMIT)


def _split3(x):
    hi = x.astype(BF16)
    r = x - hi.astype(F32)
    mid = r.astype(BF16)
    lo = (r - mid.astype(F32)).astype(BF16)
    return hi, mid, lo


def _silu(g):
    g = g.astype(F32)
    return g / (1.0 + jnp.exp(-g))


def _rms_rows(x, g_ref):
    y = x * lax.rsqrt(jnp.mean(x * x, axis=-1, keepdims=True) + EPS)
    return (y * g_ref[...]).astype(BF16)


def _store_col_blocks(o_ref, res):
    for c in range(o_ref.shape[0]):
        o_ref[c] = res[:, c * LANES:(c + 1) * LANES]


def _in_proj_kernel(x_ref, g_ref, w_ref, wl_ref, wvt_ref, o_ref, ol_ref, ovt_ref, h_ref):
    @pl.when(pl.program_id(1) == 0)
    def _():
        h = _rms_rows(x_ref[...], g_ref)
        h_ref[...] = h
        ol_ref[...] = jnp.dot(h, wl_ref[...], preferred_element_type=F32)
        vt = lax.dot_general(wvt_ref[...], h, NT_DIMS, preferred_element_type=F32).astype(BF16)
        for c in range(ovt_ref.shape[0]):
            ovt_ref[c] = vt[c * LANES:(c + 1) * LANES, :]

    res = jnp.dot(h_ref[...], w_ref[...], preferred_element_type=F32).astype(BF16)
    _store_col_blocks(o_ref, res)


def _in_proj(x2d, g, w, wl, wvt, tm, tn):
    m, k = x2d.shape
    n = w.shape[1]
    nv = wvt.shape[0]
    return pl.pallas_call(
        _in_proj_kernel,
        grid=(m // tm, n // tn),
        in_specs=[
            pl.BlockSpec((tm, k), lambda i, j: (i, 0)),
            pl.BlockSpec((1, k), lambda i, j: (0, 0)),
            pl.BlockSpec((k, tn), lambda i, j: (0, j)),
            pl.BlockSpec((k, LANES), lambda i, j: (0, 0)),
            pl.BlockSpec((nv, k), lambda i, j: (0, 0)),
        ],
        out_specs=[
            pl.BlockSpec((tn // LANES, tm, LANES), lambda i, j: (j, i, 0)),
            pl.BlockSpec((tm, LANES), lambda i, j: (i, 0)),
            pl.BlockSpec((nv // LANES, LANES, tm), lambda i, j: (0, 0, i)),
        ],
        out_shape=[
            jax.ShapeDtypeStruct((n // LANES, m, LANES), BF16),
            jax.ShapeDtypeStruct((m, LANES), F32),
            jax.ShapeDtypeStruct((nv // LANES, LANES, m), BF16),
        ],
        scratch_shapes=[pltpu.VMEM((tm, k), BF16)],
        compiler_params=_params(2),
        name="in_proj",
    )(x2d, g, w, wl, wvt)


def _mem_kv_kernel(x_ref, g_ref, wk_ref, wvt_ref, ok_ref, ovt_ref):
    h = _rms_rows(x_ref[...], g_ref)
    _store_col_blocks(ok_ref, jnp.dot(h, wk_ref[...], preferred_element_type=F32).astype(BF16))
    vt = lax.dot_general(wvt_ref[...], h, NT_DIMS, preferred_element_type=F32).astype(BF16)
    for c in range(ovt_ref.shape[0]):
        ovt_ref[c] = vt[c * LANES:(c + 1) * LANES, :]


def _mem_kv(x2d, g, wk, wvt, tm):
    m, k = x2d.shape
    n = wk.shape[1]
    return pl.pallas_call(
        _mem_kv_kernel,
        grid=(m // tm,),
        in_specs=[
            pl.BlockSpec((tm, k), lambda i: (i, 0)),
            pl.BlockSpec((1, k), lambda i: (0, 0)),
            pl.BlockSpec((k, n), lambda i: (0, 0)),
            pl.BlockSpec((n, k), lambda i: (0, 0)),
        ],
        out_specs=[
            pl.BlockSpec((n // LANES, tm, LANES), lambda i: (0, i, 0)),
            pl.BlockSpec((n // LANES, LANES, tm), lambda i: (0, 0, i)),
        ],
        out_shape=[
            jax.ShapeDtypeStruct((n // LANES, m, LANES), BF16),
            jax.ShapeDtypeStruct((n // LANES, LANES, m), BF16),
        ],
        compiler_params=_params(1),
        name="mem_kv",
    )(x2d, g, wk, wvt)


def _forget_cumsum_kernel(logit_ref, b_ref, o_ref, carry_ref):
    @pl.when(pl.program_id(1) == 0)
    def _():
        carry_ref[...] = jnp.zeros_like(carry_ref)

    tb = logit_ref.shape[1]
    z = logit_ref[0] + b_ref[...]
    log_f = jnp.minimum(z, 0.0) - jnp.log1p(jnp.exp(-jnp.abs(z)))
    row = lax.broadcasted_iota(jnp.int32, (tb, tb), 0)
    col = lax.broadcasted_iota(jnp.int32, (tb, tb), 1)
    tri = jnp.where(col <= row, 1.0, 0.0).astype(BF16)
    c = carry_ref[...]
    for part in _split3(log_f):
        c = c + jnp.dot(tri, part, preferred_element_type=F32)
    carry_ref[...] = c[tb - 1:tb, :]
    src = lax.broadcasted_iota(jnp.int32, (LANES, LANES), 0)
    dst = lax.broadcasted_iota(jnp.int32, (LANES, LANES), 1)
    parts = _split3(-c)
    for h in range(N_CB):
        ext = jnp.zeros((tb, LANES), F32)
        for term, part in enumerate(parts):
            place = jnp.where((src == 2 * h) & (dst == HEAD_DIM + term)
                              | (src == 2 * h + 1) & (dst == term), 1.0, 0.0).astype(BF16)
            ext = ext + jnp.dot(part, place, preferred_element_type=F32)
        o_ref[h, 0] = ext.astype(BF16)


def _forget_cumsum(logit3, b_pad, tb):
    b, s, _ = logit3.shape
    return pl.pallas_call(
        _forget_cumsum_kernel,
        grid=(b, s // tb),
        in_specs=[
            pl.BlockSpec((1, tb, LANES), lambda i, j: (i, j, 0)),
            pl.BlockSpec((1, LANES), lambda i, j: (0, 0)),
        ],
        out_specs=pl.BlockSpec((N_CB, 1, tb, LANES), lambda i, j: (0, i, j, 0)),
        out_shape=jax.ShapeDtypeStruct((N_CB, b, s, LANES), BF16),
        scratch_shapes=[pltpu.VMEM((1, LANES), F32)],
        compiler_params=_params(2),
        name="forget_cumsum",
    )(logit3, b_pad)


def _bias_tiles_kernel(rb_ref, o_ref):
    h = pl.program_id(0)
    j = pl.program_id(1)
    bk, bq = o_ref.shape[2], o_ref.shape[3]
    key = lax.broadcasted_iota(jnp.int32, (bk, bq), 0)
    query = lax.broadcasted_iota(jnp.int32, (bk, bq), 1)
    dist = query - key + j * bk
    n = jnp.maximum(dist, 0)
    max_exact = NUM_BUCKETS // 2
    nf = jnp.maximum(n, 1).astype(F32)
    large = max_exact + (jnp.log(nf / max_exact) / math.log(MAX_DISTANCE / max_exact)
                         * (NUM_BUCKETS - max_exact)).astype(jnp.int32)
    large = jnp.minimum(large, NUM_BUCKETS - 1)
    bucket = jnp.where(n < max_exact, n, large)
    far = rb_ref[NUM_BUCKETS - 1, h]
    bias = jnp.zeros((bk, bq), F32)
    for b in range(NUM_BUCKETS - 1):
        bias = jnp.where(bucket == b, rb_ref[b, h] - far, bias)
    o_ref[0, 0] = jnp.where(dist >= 0, bias, NEG)


def _bias_tiles(rel_bias, bk, bq):
    n_heads = rel_bias.shape[1]
    return pl.pallas_call(
        _bias_tiles_kernel,
        grid=(n_heads, 2),
        in_specs=[pl.BlockSpec(memory_space=pltpu.SMEM)],
        out_specs=pl.BlockSpec((1, 1, bk, bq), lambda h, j: (h, j, 0, 0)),
        out_shape=jax.ShapeDtypeStruct((n_heads, 2, bk, bq), F32),
        compiler_params=_params(2),
        name="bias_tiles",
    )(rel_bias)


def _stack_halves(q):
    lane = lax.broadcasted_iota(jnp.int32, q.shape, 1)
    zero = jnp.zeros_like(q)
    return jnp.concatenate([jnp.where(lane < HEAD_DIM, q, zero),
                            jnp.where(lane < HEAD_DIM, zero, q)], axis=0)


def _key_block(ref, h, kb, bk):
    return ref[h, 0, pl.ds(pl.multiple_of(kb * bk, bk), bk), :]


def _q_spec(bq, cb):
    return pl.BlockSpec((N_CB, 1, bq, LANES), lambda b, i: (cb // N_CB, b, i, 0))


def _kv_spec(s, cb):
    return pl.BlockSpec((N_CB, 1, s, LANES), lambda b, i: (cb // N_CB, b, 0, 0))


def _gate_spec(bq, group):
    return pl.BlockSpec((N_CB, 1, bq, LANES), lambda b, i: (CB_GATE // N_CB + group, b, i, 0))


def _out_spec(bq):
    return pl.BlockSpec((1, bq, GROUP_W), lambda b, i: (b, i, 0))


def _store_gated(o_ref, gate_ref, h, o):
    o_ref[0, :, h * LANES:(h + 1) * LANES] = (o * _silu(gate_ref[h, 0])).astype(o_ref.dtype)


def _with_ones_rows(vt):
    return jnp.concatenate([vt, jnp.ones((BF16_ROWS, vt.shape[1]), BF16)], axis=0)


def _vt_shared(vt_ref, h, kb, bk):
    vt = _with_ones_rows(vt_ref[h, :, pl.ds(pl.multiple_of(kb * bk, bk), bk)])
    return vt, vt


def _vt_per_head(vt_ref, h, kb, bk):
    vt = vt_ref[h, :, pl.ds(pl.multiple_of(kb * bk, bk), bk)]
    return _with_ones_rows(vt[:HEAD_DIM]), _with_ones_rows(vt[HEAD_DIM:])


def _value_t(vts, p):
    n = p.shape[1] // 2
    return jnp.concatenate([jnp.dot(vts[0], p[:, :n], preferred_element_type=F32),
                            jnp.dot(vts[1], p[:, n:], preferred_element_type=F32)], axis=1)


def _scores_t(q_st, k_ref, h, kb, bk):
    return lax.dot_general(_key_block(k_ref, h, kb, bk), q_st, NT_DIMS,
                           preferred_element_type=F32)


VALUE_LAG = 2


def _softmax_t_blocks(items, carries):
    ms = [None if carries is None else carries[h][0] for h in range(N_CB)]
    accs = [None if carries is None else carries[h][1] for h in range(N_CB)]
    pending = {}

    def score(i):
        h, score_fn, _ = items[i]
        s = score_fn()
        m = jnp.max(s, axis=0, keepdims=True)
        m_new = m if ms[h] is None else jnp.maximum(ms[h], m)
        pending[i] = (ms[h], m_new, jnp.exp(s - m_new).astype(BF16))
        ms[h] = m_new

    def value(i):
        h, _, vt_fn = items[i]
        m_old, m_new, p = pending.pop(i)
        pv = _value_t(vt_fn(), p)
        accs[h] = pv if accs[h] is None else jnp.exp(m_old - m_new) * accs[h] + pv

    for i in range(len(items) + VALUE_LAG):
        if i < len(items):
            score(i)
        if i >= VALUE_LAG:
            value(i - VALUE_LAG)
    return tuple((ms[h], accs[h]) for h in range(N_CB))


def _pair_loop(n_blocks, block_of, step, carries):
    carries = lax.fori_loop(
        0, n_blocks // 2, lambda i, c: step((block_of(2 * i), block_of(2 * i + 1)), c), carries)
    return lax.fori_loop(
        0, n_blocks % 2, lambda i, c: step((block_of(n_blocks - 1),), c), carries)


def _normalized_t(acc):
    d = acc.shape[0] - BF16_ROWS
    return acc[:d] * (1.0 / acc[d:d + 1])


def _diff_kernel(lam_init, q_ref, k_ref, vt_ref, gate_ref, tile_ref, lamp_ref, hn_ref, o_ref):
    bq = q_ref.shape[2]
    qi = pl.program_id(1)
    qs = [_stack_halves(q_ref[h, 0] * (HEAD_DIM ** -0.5)) for h in range(N_CB)]

    def item(h, kb, tile_fn=None):
        def score():
            s = _scores_t(qs[h], k_ref, h, kb, bq)
            if tile_fn is None:
                return s
            tile = tile_fn(h)
            return s + jnp.concatenate([tile, tile], axis=1)
        return h, score, lambda: _vt_shared(vt_ref, h, kb, bq)

    def step(kbs, carries):
        return _softmax_t_blocks([item(h, kb) for kb in kbs for h in range(N_CB)], carries)

    near = ([item(h, qi, lambda h: tile_ref[h, 0]) for h in range(N_CB)]
            + [item(h, jnp.maximum(qi - 1, 0), lambda h: jnp.where(qi >= 1, tile_ref[h, 1], NEG))
               for h in range(N_CB)])
    carries = _softmax_t_blocks(near, None)
    carries = _pair_loop(jnp.maximum(qi - 1, 0), lambda i: i, step, carries)

    lp = lamp_ref[...]
    lam = (jnp.exp(jnp.sum(lp[0:1] * lp[1:2], axis=-1, keepdims=True))
           - jnp.exp(jnp.sum(lp[2:3] * lp[3:4], axis=-1, keepdims=True)) + lam_init)
    head_gain = jnp.broadcast_to(hn_ref[...] * (1.0 - lam_init), (LANES, bq))
    for h in range(N_CB):
        a = _normalized_t(carries[h][1])
        o = a[:, :bq] - lam * a[:, bq:]
        o = o * lax.rsqrt(jnp.mean(o * o, axis=0, keepdims=True) + EPS)
        _store_gated(o_ref, gate_ref, h, (o * head_gain).T)


def _diff_attention(proj4, vt, tiles, lam_params, head_norm, lam_init, bq):
    _, b, s, _ = proj4.shape
    return pl.pallas_call(
        functools.partial(_diff_kernel, lam_init),
        grid=(b, s // bq),
        in_specs=[
            _q_spec(bq, CB_AQ), _kv_spec(s, CB_AK),
            pl.BlockSpec((N_CB, LANES, s), lambda b, i: (VT_A, 0, b)),
            _gate_spec(bq, 0),
            pl.BlockSpec((N_CB, 2, bq, bq), lambda b, i: (0, 0, 0, 0)),
            pl.BlockSpec((4, HEAD_DIM), lambda b, i: (0, 0)),
            pl.BlockSpec((LANES, 1), lambda b, i: (0, 0)),
        ],
        out_specs=_out_spec(bq),
        out_shape=jax.ShapeDtypeStruct((b, s, GROUP_W), BF16),
        compiler_params=_params(2),
        name="diff_attention",
    )(proj4, proj4, vt, proj4, tiles, lam_params, head_norm)


def _stick_kernel(q_ref, k_ref, vt_ref, gate_ref, o_ref):
    bq = q_ref.shape[2]
    qi = pl.program_id(1)
    qs = [_stack_halves(q_ref[h, 0] * (HEAD_DIM ** -0.5)) for h in range(N_CB)]
    key = lax.broadcasted_iota(jnp.int32, (bq, 2 * bq), 0)
    query = lax.broadcasted_iota(jnp.int32, (bq, 2 * bq), 1) & (bq - 1)
    strict = key < query

    row = lax.broadcasted_iota(jnp.int32, (bq + BF16_ROWS, bq), 0)
    col = lax.broadcasted_iota(jnp.int32, (bq + BF16_ROWS, bq), 1)
    after = jnp.where((col > row) | (row >= bq), 1.0, 0.0).astype(BF16)

    def step(kbs, carries, mask=None):
        items = [(h, kb) for kb in kbs for h in range(N_CB)]
        rests = [carries[h][0] for h in range(N_CB)]
        accs = [carries[h][1] for h in range(N_CB)]
        pending = {}

        def score(i):
            h, kb = items[i]
            z = _scores_t(qs[h], k_ref, h, kb, bq)
            sp = jnp.log(1.0 + jnp.exp2(jnp.abs(z) * -LOG2E))
            ls_pos = jnp.minimum(z, 0.0) - sp
            ls_neg = ls_pos - z
            if mask is not None:
                ls_neg = jnp.where(mask, ls_neg, 0.0)
            pending[i] = (ls_pos, ls_neg.astype(BF16))

        def cumsum(i):
            h, _ = items[i]
            ls_pos, log_rest = pending[i]
            t = jnp.dot(after, log_rest, preferred_element_type=F32)
            a = jnp.exp2((ls_pos + t[:bq]) * LOG2E)
            if mask is not None:
                a = jnp.where(mask, a, 0.0)
            pending[i] = (a.astype(BF16), jnp.exp2(rests[h] * LOG2E))
            rests[h] = rests[h] + t[bq:bq + 1]

        def value(i):
            h, kb = items[i]
            a, scale = pending.pop(i)
            vt = vt_ref[h, :, pl.ds(pl.multiple_of(kb * bq, bq), bq)]
            accs[h] = accs[h] + scale * _value_t((vt[:HEAD_DIM], vt[HEAD_DIM:]), a)

        for i in range(len(items) + 2):
            if i < len(items):
                score(i)
            if 1 <= i <= len(items):
                cumsum(i - 1)
            if i >= 2:
                value(i - 2)
        return tuple((rests[h], accs[h]) for h in range(N_CB))

    zero = (jnp.zeros((1, 2 * bq), F32), jnp.zeros((HEAD_DIM, 2 * bq), F32))
    carries = step((qi,), (zero,) * N_CB, strict)
    carries = _pair_loop(qi, lambda i: qi - 1 - i, step, carries)
    for h in range(N_CB):
        acc = carries[h][1]
        _store_gated(o_ref, gate_ref, h, jnp.concatenate([acc[:, :bq], acc[:, bq:]], axis=0).T)


def _stick_attention(proj4, vt, bq):
    _, b, s, _ = proj4.shape
    return pl.pallas_call(
        _stick_kernel,
        grid=(b, s // bq),
        in_specs=[
            _q_spec(bq, CB_SQ), _kv_spec(s, CB_SK),
            pl.BlockSpec((N_CB, LANES, s), lambda b, i: (VT_S, 0, b)),
            _gate_spec(bq, 1),
        ],
        out_specs=_out_spec(bq),
        out_shape=jax.ShapeDtypeStruct((b, s, GROUP_W), BF16),
        compiler_params=_params(2),
        name="stick_attention",
    )(proj4, proj4, vt, proj4)


def _fox_kernel(q_ref, k_ref, vt_ref, gate_ref, negc_ref, o_ref):
    bq = q_ref.shape[2]
    qi = pl.program_id(1)
    key = lax.broadcasted_iota(jnp.int32, (bq, 2 * bq), 0)
    query = lax.broadcasted_iota(jnp.int32, (bq, 2 * bq), 1) & (bq - 1)
    causal = key <= query

    lane = lax.broadcasted_iota(jnp.int32, (bq, LANES), 1)
    low = lane < HEAD_DIM
    ones_low = jnp.where(lane < 3, 1.0, 0.0).astype(BF16)
    ones_high = jnp.where(low | (lane >= HEAD_DIM + 3), 0.0, 1.0).astype(BF16)
    qs = []
    for h in range(N_CB):
        q = q_ref[h, 0] * (HEAD_DIM ** -0.5)
        qs.append((jnp.where(low, q, ones_high), jnp.where(low, ones_low, q)))

    def item(h, kb, mask):
        def score():
            start = pl.multiple_of(kb * bq, bq)
            k = k_ref[h, 0, pl.ds(start, bq), :]
            c = negc_ref[h, 0, pl.ds(start, bq), :]
            s = jnp.concatenate(
                [lax.dot_general(jnp.where(low, k, c), qs[h][0], NT_DIMS,
                                 preferred_element_type=F32),
                 lax.dot_general(jnp.where(low, c, k), qs[h][1], NT_DIMS,
                                 preferred_element_type=F32)], axis=1)
            return s if mask is None else jnp.where(mask, s, NEG)
        return h, score, lambda: _vt_per_head(vt_ref, h, kb, bq)

    def step(kbs, carries, mask=None):
        return _softmax_t_blocks([item(h, kb, mask) for kb in kbs for h in range(N_CB)], carries)

    carries = _pair_loop(qi, lambda i: i, step, step((qi,), None, causal))
    for h in range(N_CB):
        a = _normalized_t(carries[h][1])
        _store_gated(o_ref, gate_ref, h, jnp.concatenate([a[:, :bq], a[:, bq:]], axis=0).T)


def _fox_attention(proj4, vt, negc, bq):
    _, b, s, _ = proj4.shape
    return pl.pallas_call(
        _fox_kernel,
        grid=(b, s // bq),
        in_specs=[
            _q_spec(bq, CB_FQ), _kv_spec(s, CB_FK),
            pl.BlockSpec((N_CB, LANES, s), lambda b, i: (VT_F, 0, b)),
            _gate_spec(bq, 2),
            pl.BlockSpec((N_CB, 1, s, LANES), lambda b, i: (0, b, 0, 0)),
        ],
        out_specs=_out_spec(bq),
        out_shape=jax.ShapeDtypeStruct((b, s, GROUP_W), BF16),
        compiler_params=_params(2),
        name="fox_attention",
    )(proj4, proj4, vt, proj4, negc)


def _mem_kernel(q_ref, km_ref, vmt_ref, gate_ref, o_ref):
    scale = LANES ** -0.5
    scores = [lax.dot_general(km_ref[h, 0], q_ref[h, 0], NT_DIMS,
                              preferred_element_type=F32) * scale for h in range(N_CB)]
    probs = [jnp.exp(s - jnp.max(s, axis=0, keepdims=True)).astype(BF16) for s in scores]
    for h in range(N_CB):
        acc = jnp.dot(_with_ones_rows(vmt_ref[h]), probs[h], preferred_element_type=F32)
        _store_gated(o_ref, gate_ref, h, _normalized_t(acc).T)


def _mem_attention(proj4, km4, vmt, bq):
    _, b, s, _ = proj4.shape
    mlen = km4.shape[2]
    return pl.pallas_call(
        _mem_kernel,
        grid=(b, s // bq),
        in_specs=[
            _q_spec(bq, CB_MQ),
            pl.BlockSpec((N_CB, 1, mlen, LANES), lambda b, i: (0, b, 0, 0)),
            pl.BlockSpec((N_CB, LANES, mlen), lambda b, i: (0, 0, b)),
            _gate_spec(bq, 3),
        ],
        out_specs=_out_spec(bq),
        out_shape=jax.ShapeDtypeStruct((b, s, GROUP_W), BF16),
        compiler_params=_params(2),
        name="mem_attention",
    )(proj4, km4, vmt, proj4)


def _out_proj_kernel(ya_ref, yb_ref, yc_ref, ym_ref, w_ref, g_ref, x_ref, o_ref):
    y = jnp.concatenate([ya_ref[...], yb_ref[...], yc_ref[...], ym_ref[...]], axis=-1)
    z = jnp.dot(y, w_ref[...], preferred_element_type=F32)
    z = z * lax.rsqrt(jnp.mean(z * z, axis=-1, keepdims=True) + EPS)
    o_ref[...] = x_ref[...] + z * g_ref[...]


def _out_proj(ys, w, g, x2d, tm):
    m, d = x2d.shape
    y_spec = pl.BlockSpec((tm, GROUP_W), lambda i: (i, 0))
    return pl.pallas_call(
        _out_proj_kernel,
        grid=(m // tm,),
        in_specs=[y_spec] * 4 + [
            pl.BlockSpec(w.shape, lambda i: (0, 0)),
            pl.BlockSpec((1, d), lambda i: (0, 0)),
            pl.BlockSpec((tm, d), lambda i: (i, 0)),
        ],
        out_specs=pl.BlockSpec((tm, d), lambda i: (i, 0)),
        out_shape=jax.ShapeDtypeStruct((m, d), F32),
        compiler_params=_params(1),
        name="out_proj",
    )(*ys, w, g, x2d)


def _pick(n, cands):
    for c in cands:
        if n % c == 0:
            return c
    raise ValueError(f"no tile in {cands} divides {n}")


def kernel(x, mem, w_in, b_forget, w_mem_kv, w_out, pre_norm, post_norm, mem_norm,
           diff_lambda, diff_head_norm, rel_bias):
    b, s, d = x.shape
    mlen = mem.shape[1]
    depth = w_in.shape[0]
    m = b * s
    bq = _pick(s, (256, 128))
    tm = _pick(m, (1024, 512, 256))

    gw = GROUP_W
    logit_lo, logit_hi = 9 * gw, 9 * gw + N_FOX_HEADS
    tiles = _bias_tiles(rel_bias, bq, bq)
    mem2d = mem.reshape(b * mlen, d)
    x2d = x.reshape(m, d)

    for l in range(depth):
        wl = w_in[l]
        w_main = jnp.concatenate([wl[:, 0:2 * gw], wl[:, 3 * gw:5 * gw], wl[:, 6 * gw:8 * gw],
                                  wl[:, logit_hi:]], axis=1).astype(BF16)
        w_vt = jnp.concatenate([wl[:, 2 * gw:3 * gw], wl[:, 5 * gw:6 * gw], wl[:, 8 * gw:9 * gw]],
                               axis=1).T.astype(BF16)
        w_logit = jnp.pad(wl[:, logit_lo:logit_hi], ((0, 0), (0, LANES - N_FOX_HEADS))).astype(BF16)
        b_pad = jnp.pad(b_forget[l], (0, LANES - N_FOX_HEADS)).reshape(1, LANES)

        tn = _pick(w_main.shape[1], (1408, 512))
        proj, logit, vt = _in_proj(x2d, pre_norm[l].reshape(1, d), w_main, w_logit, w_vt, tm, tn)
        proj4 = proj.reshape(N_PROJ_CB, b, s, LANES)
        negc = _forget_cumsum(logit.reshape(b, s, LANES), b_pad, _pick(s, (512, 256, 128)))

        w_kv = w_mem_kv[l]
        km, vmt = _mem_kv(mem2d, mem_norm[l].reshape(1, d), w_kv[:, :gw].astype(BF16),
                          w_kv[:, gw:].T.astype(BF16), _pick(b * mlen, (1024, 512, 256)))
        km4 = km.reshape(N_CB, b, mlen, LANES)

        lam_init = 0.8 - 0.6 * math.exp(-0.3 * l)
        y_a = _diff_attention(proj4, vt, tiles, diff_lambda[l],
                              diff_head_norm[l].reshape(LANES, 1), lam_init, bq)
        y_b = _stick_attention(proj4, vt, bq)
        y_c = _fox_attention(proj4, vt, negc, bq)
        y_m = _mem_attention(proj4, km4, vmt, _pick(s, (1024, 512, 256, 128)))
        ys = [y.reshape(m, GROUP_W) for y in (y_a, y_b, y_c, y_m)]
        x2d = _out_proj(ys, w_out[l].astype(BF16), post_norm[l].reshape(1, d), x2d,
                        _pick(m, (1024, 512, 256)))
    return x2d.reshape(b, s, d)
```

```python
import functools
import math

import jax
import jax.numpy as jnp
import numpy as np
from jax import lax
from jax.experimental import pallas as pl
from jax.experimental.pallas import tpu as pltpu

F32 = jnp.float32
BF16 = jnp.bfloat16

LANES = 128
BF16_ROWS = 16
HEAD_DIM = 64
GROUP_W = 512
N_CB = GROUP_W // LANES
N_FOX_HEADS = GROUP_W // HEAD_DIM
NUM_BUCKETS = 32
MAX_DISTANCE = 128
EPS = 1e-6
NEG = -1e30
LOG2E = 1.4426950408889634
VMEM_LIMIT = 56 * 1024 * 1024

CB_AQ, CB_AK = 0, 4
CB_SQ, CB_SK = 8, 12
CB_FQ, CB_FK = 16, 20
CB_MQ, CB_GATE = 24, 28
N_PROJ_CB = 44
VT_A, VT_S, VT_F = 0, 1, 2

NT_DIMS = (((1,), (1,)), ((), ()))


def _params(n_axes):
    return pltpu.CompilerParams(dimension_semantics=("arbitrary",) * n_axes,
                                vmem_limit_bytes=VMEM_LIMIT)


def _split3(x):
    hi = x.astype(BF16)
    r = x - hi.astype(F32)
    mid = r.astype(BF16)
    lo = (r - mid.astype(F32)).astype(BF16)
    return hi, mid, lo


def _silu(g):
    g = g.astype(F32)
    return g / (1.0 + jnp.exp(-g))


def _rms_rows(x, g_ref):
    y = x * lax.rsqrt(jnp.mean(x * x, axis=-1, keepdims=True) + EPS)
    return (y * g_ref[...]).astype(BF16)


def _store_col_blocks(o_ref, res):
    for c in range(o_ref.shape[0]):
        o_ref[c] = res[:, c * LANES:(c + 1) * LANES]


def _in_proj_kernel(x_ref, g_ref, w_ref, wl_ref, wvt_ref, o_ref, ol_ref, ovt_ref, h_ref):
    @pl.when(pl.program_id(1) == 0)
    def _():
        h = _rms_rows(x_ref[...], g_ref)
        h_ref[...] = h
        ol_ref[...] = jnp.dot(h, wl_ref[...], preferred_element_type=F32)
        vt = lax.dot_general(wvt_ref[...], h, NT_DIMS, preferred_element_type=F32).astype(BF16)
        for c in range(ovt_ref.shape[0]):
            ovt_ref[c] = vt[c * LANES:(c + 1) * LANES, :]

    res = jnp.dot(h_ref[...], w_ref[...], preferred_element_type=F32).astype(BF16)
    _store_col_blocks(o_ref, res)


def _in_proj(x2d, g, w, wl, wvt, tm, tn):
    m, k = x2d.shape
    n = w.shape[1]
    nv = wvt.shape[0]
    return pl.pallas_call(
        _in_proj_kernel,
        grid=(m // tm, n // tn),
        in_specs=[
            pl.BlockSpec((tm, k), lambda i, j: (i, 0)),
            pl.BlockSpec((1, k), lambda i, j: (0, 0)),
            pl.BlockSpec((k, tn), lambda i, j: (0, j)),
            pl.BlockSpec((k, LANES), lambda i, j: (0, 0)),
            pl.BlockSpec((nv, k), lambda i, j: (0, 0)),
        ],
        out_specs=[
            pl.BlockSpec((tn // LANES, tm, LANES), lambda i, j: (j, i, 0)),
            pl.BlockSpec((tm, LANES), lambda i, j: (i, 0)),
            pl.BlockSpec((nv // LANES, LANES, tm), lambda i, j: (0, 0, i)),
        ],
        out_shape=[
            jax.ShapeDtypeStruct((n // LANES, m, LANES), BF16),
            jax.ShapeDtypeStruct((m, LANES), F32),
            jax.ShapeDtypeStruct((nv // LANES, LANES, m), BF16),
        ],
        scratch_shapes=[pltpu.VMEM((tm, k), BF16)],
        compiler_params=_params(2),
        name="in_proj",
    )(x2d, g, w, wl, wvt)


def _mem_kv_kernel(x_ref, g_ref, wk_ref, wvt_ref, ok_ref, ovt_ref):
    h = _rms_rows(x_ref[...], g_ref)
    _store_col_blocks(ok_ref, jnp.dot(h, wk_ref[...], preferred_element_type=F32).astype(BF16))
    vt = lax.dot_general(wvt_ref[...], h, NT_DIMS, preferred_element_type=F32).astype(BF16)
    for c in range(ovt_ref.shape[0]):
        ovt_ref[c] = vt[c * LANES:(c + 1) * LANES, :]


def _mem_kv(x2d, g, wk, wvt, tm):
    m, k = x2d.shape
    n = wk.shape[1]
    return pl.pallas_call(
        _mem_kv_kernel,
        grid=(m // tm,),
        in_specs=[
            pl.BlockSpec((tm, k), lambda i: (i, 0)),
            pl.BlockSpec((1, k), lambda i: (0, 0)),
            pl.BlockSpec((k, n), lambda i: (0, 0)),
            pl.BlockSpec((n, k), lambda i: (0, 0)),
        ],
        out_specs=[
            pl.BlockSpec((n // LANES, tm, LANES), lambda i: (0, i, 0)),
            pl.BlockSpec((n // LANES, LANES, tm), lambda i: (0, 0, i)),
        ],
        out_shape=[
            jax.ShapeDtypeStruct((n // LANES, m, LANES), BF16),
            jax.ShapeDtypeStruct((n // LANES, LANES, m), BF16),
        ],
        compiler_params=_params(1),
        name="mem_kv",
    )(x2d, g, wk, wvt)


def _forget_cumsum_kernel(logit_ref, b_ref, tri_ref, place_ref, o_ref, carry_ref):
    @pl.when(pl.program_id(1) == 0)
    def _():
        carry_ref[...] = jnp.zeros_like(carry_ref)

    tb = logit_ref.shape[1]
    z = logit_ref[0] + b_ref[...]
    log_f = jnp.minimum(z, 0.0) - jnp.log1p(jnp.exp(-jnp.abs(z)))
    sums = jnp.dot(tri_ref[...], jnp.concatenate(_split3(log_f), axis=1),
                   preferred_element_type=F32)
    c = carry_ref[...] + (sums[:, :LANES] + sums[:, LANES:2 * LANES] + sums[:, 2 * LANES:])
    carry_ref[...] = c[tb - 1:tb, :]
    ext = jnp.dot(jnp.concatenate(_split3(-c), axis=1), place_ref[...],
                  preferred_element_type=F32)
    for h in range(N_CB):
        o_ref[h, 0] = ext[:, h * LANES:(h + 1) * LANES].astype(BF16)


def _placement():
    place = np.zeros((3 * LANES, N_CB * LANES), np.float32)
    for term in range(3):
        for h in range(N_CB):
            place[term * LANES + 2 * h, h * LANES + HEAD_DIM + term] = 1.0
            place[term * LANES + 2 * h + 1, h * LANES + term] = 1.0
    return jnp.asarray(place, BF16)


def _forget_cumsum(logit3, b_pad, tb):
    b, s, _ = logit3.shape
    tri = jnp.asarray(np.tril(np.ones((tb, tb), np.float32)), BF16)
    place = _placement()
    return pl.pallas_call(
        _forget_cumsum_kernel,
        grid=(b, s // tb),
        in_specs=[
            pl.BlockSpec((1, tb, LANES), lambda i, j: (i, j, 0)),
            pl.BlockSpec((1, LANES), lambda i, j: (0, 0)),
            pl.BlockSpec((tb, tb), lambda i, j: (0, 0)),
            pl.BlockSpec(place.shape, lambda i, j: (0, 0)),
        ],
        out_specs=pl.BlockSpec((N_CB, 1, tb, LANES), lambda i, j: (0, i, j, 0)),
        out_shape=jax.ShapeDtypeStruct((N_CB, b, s, LANES), BF16),
        scratch_shapes=[pltpu.VMEM((1, LANES), F32)],
        compiler_params=_params(2),
        name="forget_cumsum",
    )(logit3, b_pad, tri, place)


def _bias_tiles_kernel(rb_ref, o_ref):
    h = pl.program_id(0)
    j = pl.program_id(1)
    bk, bq = o_ref.shape[2], o_ref.shape[3]
    key = lax.broadcasted_iota(jnp.int32, (bk, bq), 0)
    query = lax.broadcasted_iota(jnp.int32, (bk, bq), 1)
    dist = query - key + j * bk
    n = jnp.maximum(dist, 0)
    max_exact = NUM_BUCKETS // 2
    nf = jnp.maximum(n, 1).astype(F32)
    large = max_exact + (jnp.log(nf / max_exact) / math.log(MAX_DISTANCE / max_exact)
                         * (NUM_BUCKETS - max_exact)).astype(jnp.int32)
    large = jnp.minimum(large, NUM_BUCKETS - 1)
    bucket = jnp.where(n < max_exact, n, large)
    far = rb_ref[NUM_BUCKETS - 1, h]
    bias = jnp.zeros((bk, bq), F32)
    for b in range(NUM_BUCKETS - 1):
        bias = jnp.where(bucket == b, rb_ref[b, h] - far, bias)
    o_ref[0, 0] = jnp.where(dist >= 0, bias, NEG)


def _bias_tiles(rel_bias, bk, bq):
    n_heads = rel_bias.shape[1]
    return pl.pallas_call(
        _bias_tiles_kernel,
        grid=(n_heads, 2),
        in_specs=[pl.BlockSpec(memory_space=pltpu.SMEM)],
        out_specs=pl.BlockSpec((1, 1, bk, bq), lambda h, j: (h, j, 0, 0)),
        out_shape=jax.ShapeDtypeStruct((n_heads, 2, bk, bq), F32),
        compiler_params=_params(2),
        name="bias_tiles",
    )(rel_bias)


def _stack_halves(q):
    lane = lax.broadcasted_iota(jnp.int32, q.shape, 1)
    zero = jnp.zeros_like(q)
    return jnp.concatenate([jnp.where(lane < HEAD_DIM, q, zero),
                            jnp.where(lane < HEAD_DIM, zero, q)], axis=0)


def _key_block(ref, h, kb, bk):
    return ref[h, 0, pl.ds(pl.multiple_of(kb * bk, bk), bk), :]


def _q_spec(bq, cb):
    return pl.BlockSpec((N_CB, 1, bq, LANES), lambda b, i: (cb // N_CB, b, i, 0))


def _kv_spec(s, cb):
    return pl.BlockSpec((N_CB, 1, s, LANES), lambda b, i: (cb // N_CB, b, 0, 0))


def _gate_spec(bq, group):
    return pl.BlockSpec((N_CB, 1, bq, LANES), lambda b, i: (CB_GATE // N_CB + group, b, i, 0))


def _out_spec(bq):
    return pl.BlockSpec((1, bq, GROUP_W), lambda b, i: (b, i, 0))


def _store_gated(o_ref, gate_ref, h, o):
    o_ref[0, :, h * LANES:(h + 1) * LANES] = (o * _silu(gate_ref[h, 0])).astype(o_ref.dtype)


def _with_ones_rows(vt):
    return jnp.concatenate([vt, jnp.ones((BF16_ROWS, vt.shape[1]), BF16)], axis=0)


def _vt_shared(vt_ref, h, kb, bk):
    vt = _with_ones_rows(vt_ref[h, :, pl.ds(pl.multiple_of(kb * bk, bk), bk)])
    return vt, vt


def _vt_per_head(vt_ref, h, kb, bk):
    vt = vt_ref[h, :, pl.ds(pl.multiple_of(kb * bk, bk), bk)]
    return _with_ones_rows(vt[:HEAD_DIM]), _with_ones_rows(vt[HEAD_DIM:])


def _value_t(vts, p):
    n = p.shape[1] // 2
    return jnp.concatenate([jnp.dot(vts[0], p[:, :n], preferred_element_type=F32),
                            jnp.dot(vts[1], p[:, n:], preferred_element_type=F32)], axis=1)


def _scores_t(q_st, k_ref, h, kb, bk):
    return lax.dot_general(_key_block(k_ref, h, kb, bk), q_st, NT_DIMS,
                           preferred_element_type=F32)


VALUE_LAG = 2


def _softmax_t_blocks(items, carries):
    ms = [None if carries is None else carries[h][0] for h in range(N_CB)]
    accs = [None if carries is None else carries[h][1] for h in range(N_CB)]
    pending = {}

    def score(i):
        h, score_fn, _ = items[i]
        s = score_fn()
        m = jnp.max(s, axis=0, keepdims=True)
        m_new = m if ms[h] is None else jnp.maximum(ms[h], m)
        pending[i] = (ms[h], m_new, jnp.exp(s - m_new).astype(BF16))
        ms[h] = m_new

    def value(i):
        h, _, vt_fn = items[i]
        m_old, m_new, p = pending.pop(i)
        pv = _value_t(vt_fn(), p)
        accs[h] = pv if accs[h] is None else jnp.exp(m_old - m_new) * accs[h] + pv

    for i in range(len(items) + VALUE_LAG):
        if i < len(items):
            score(i)
        if i >= VALUE_LAG:
            value(i - VALUE_LAG)
    return tuple((ms[h], accs[h]) for h in range(N_CB))


def _pair_loop(n_blocks, block_of, step, carries):
    carries = lax.fori_loop(
        0, n_blocks // 2, lambda i, c: step((block_of(2 * i), block_of(2 * i + 1)), c), carries)
    return lax.fori_loop(
        0, n_blocks % 2, lambda i, c: step((block_of(n_blocks - 1),), c), carries)


def _normalized_t(acc):
    d = acc.shape[0] - BF16_ROWS
    return acc[:d] * (1.0 / acc[d:d + 1])


def _diff_kernel(lam_init, q_ref, k_ref, vt_ref, gate_ref, tile_ref, lamp_ref, hn_ref, o_ref):
    bq = q_ref.shape[2]
    qi = pl.program_id(1)
    qs = [_stack_halves(q_ref[h, 0] * (HEAD_DIM ** -0.5)) for h in range(N_CB)]

    def item(h, kb, tile_fn=None):
        def score():
            s = _scores_t(qs[h], k_ref, h, kb, bq)
            if tile_fn is None:
                return s
            tile = tile_fn(h)
            return s + jnp.concatenate([tile, tile], axis=1)
        return h, score, lambda: _vt_shared(vt_ref, h, kb, bq)

    def step(kbs, carries):
        return _softmax_t_blocks([item(h, kb) for kb in kbs for h in range(N_CB)], carries)

    near = ([item(h, qi, lambda h: tile_ref[h, 0]) for h in range(N_CB)]
            + [item(h, jnp.maximum(qi - 1, 0), lambda h: jnp.where(qi >= 1, tile_ref[h, 1], NEG))
               for h in range(N_CB)])
    carries = _softmax_t_blocks(near, None)
    carries = _pair_loop(jnp.maximum(qi - 1, 0), lambda i: i, step, carries)

    lp = lamp_ref[...]
    lam = (jnp.exp(jnp.sum(lp[0:1] * lp[1:2], axis=-1, keepdims=True))
           - jnp.exp(jnp.sum(lp[2:3] * lp[3:4], axis=-1, keepdims=True)) + lam_init)
    head_gain = jnp.broadcast_to(hn_ref[...] * (1.0 - lam_init), (LANES, bq))
    for h in range(N_CB):
        a = _normalized_t(carries[h][1])
        o = a[:, :bq] - lam * a[:, bq:]
        o = o * lax.rsqrt(jnp.mean(o * o, axis=0, keepdims=True) + EPS)
        _store_gated(o_ref, gate_ref, h, (o * head_gain).T)


def _diff_attention(proj4, vt, tiles, lam_params, head_norm, lam_init, bq):
    _, b, s, _ = proj4.shape
    return pl.pallas_call(
        functools.partial(_diff_kernel, lam_init),
        grid=(b, s // bq),
        in_specs=[
            _q_spec(bq, CB_AQ), _kv_spec(s, CB_AK),
            pl.BlockSpec((N_CB, LANES, s), lambda b, i: (VT_A, 0, b)),
            _gate_spec(bq, 0),
            pl.BlockSpec((N_CB, 2, bq, bq), lambda b, i: (0, 0, 0, 0)),
            pl.BlockSpec((4, HEAD_DIM), lambda b, i: (0, 0)),
            pl.BlockSpec((LANES, 1), lambda b, i: (0, 0)),
        ],
        out_specs=_out_spec(bq),
        out_shape=jax.ShapeDtypeStruct((b, s, GROUP_W), BF16),
        compiler_params=_params(2),
        name="diff_attention",
    )(proj4, proj4, vt, proj4, tiles, lam_params, head_norm)


def _stick_kernel(q_ref, k_ref, vt_ref, gate_ref, o_ref):
    bq = q_ref.shape[2]
    qi = pl.program_id(1)
    qs = [_stack_halves(q_ref[h, 0] * (HEAD_DIM ** -0.5)) for h in range(N_CB)]
    key = lax.broadcasted_iota(jnp.int32, (bq, 2 * bq), 0)
    query = lax.broadcasted_iota(jnp.int32, (bq, 2 * bq), 1) & (bq - 1)
    strict = key < query

    row = lax.broadcasted_iota(jnp.int32, (bq + BF16_ROWS, bq), 0)
    col = lax.broadcasted_iota(jnp.int32, (bq + BF16_ROWS, bq), 1)
    after = jnp.where((col > row) | (row >= bq), 1.0, 0.0).astype(BF16)

    def step(kbs, carries, mask=None):
        items = [(h, kb) for kb in kbs for h in range(N_CB)]
        rests = [carries[h][0] for h in range(N_CB)]
        accs = [carries[h][1] for h in range(N_CB)]
        pending = {}

        def score(i):
            h, kb = items[i]
            z = _scores_t(qs[h], k_ref, h, kb, bq)
            sp = jnp.log(1.0 + jnp.exp2(jnp.abs(z) * -LOG2E))
            ls_pos = jnp.minimum(z, 0.0) - sp
            ls_neg = ls_pos - z
            if mask is not None:
                ls_neg = jnp.where(mask, ls_neg, 0.0)
            pending[i] = (ls_pos, ls_neg.astype(BF16))

        def cumsum(i):
            h, _ = items[i]
            ls_pos, log_rest = pending[i]
            t = jnp.dot(after, log_rest, preferred_element_type=F32)
            a = jnp.exp2((ls_pos + t[:bq]) * LOG2E)
            if mask is not None:
                a = jnp.where(mask, a, 0.0)
            pending[i] = (a.astype(BF16), jnp.exp2(rests[h] * LOG2E))
            rests[h] = rests[h] + t[bq:bq + 1]

        def value(i):
            h, kb = items[i]
            a, scale = pending.pop(i)
            vt = vt_ref[h, :, pl.ds(pl.multiple_of(kb * bq, bq), bq)]
            accs[h] = accs[h] + scale * _value_t((vt[:HEAD_DIM], vt[HEAD_DIM:]), a)

        for i in range(len(items) + 2):
            if i < len(items):
                score(i)
            if 1 <= i <= len(items):
                cumsum(i - 1)
            if i >= 2:
                value(i - 2)
        return tuple((rests[h], accs[h]) for h in range(N_CB))

    zero = (jnp.zeros((1, 2 * bq), F32), jnp.zeros((HEAD_DIM, 2 * bq), F32))
    carries = step((qi,), (zero,) * N_CB, strict)
    carries = _pair_loop(qi, lambda i: qi - 1 - i, step, carries)
    for h in range(N_CB):
        acc = carries[h][1]
        _store_gated(o_ref, gate_ref, h, jnp.concatenate([acc[:, :bq], acc[:, bq:]], axis=0).T)


def _stick_attention(proj4, vt, bq):
    _, b, s, _ = proj4.shape
    return pl.pallas_call(
        _stick_kernel,
        grid=(b, s // bq),
        in_specs=[
            _q_spec(bq, CB_SQ), _kv_spec(s, CB_SK),
            pl.BlockSpec((N_CB, LANES, s), lambda b, i: (VT_S, 0, b)),
            _gate_spec(bq, 1),
        ],
        out_specs=_out_spec(bq),
        out_shape=jax.ShapeDtypeStruct((b, s, GROUP_W), BF16),
        compiler_params=_params(2),
        name="stick_attention",
    )(proj4, proj4, vt, proj4)


def _fox_kernel(q_ref, k_ref, vt_ref, gate_ref, negc_ref, o_ref):
    bq = q_ref.shape[2]
    qi = pl.program_id(1)
    key = lax.broadcasted_iota(jnp.int32, (bq, 2 * bq), 0)
    query = lax.broadcasted_iota(jnp.int32, (bq, 2 * bq), 1) & (bq - 1)
    causal = key <= query

    lane = lax.broadcasted_iota(jnp.int32, (bq, LANES), 1)
    low = lane < HEAD_DIM
    ones_low = jnp.where(lane < 3, 1.0, 0.0).astype(BF16)
    ones_high = jnp.where(low | (lane >= HEAD_DIM + 3), 0.0, 1.0).astype(BF16)
    qs = []
    for h in range(N_CB):
        q = q_ref[h, 0] * (HEAD_DIM ** -0.5)
        qs.append((jnp.where(low, q, ones_high), jnp.where(low, ones_low, q)))

    def item(h, kb, mask):
        def score():
            start = pl.multiple_of(kb * bq, bq)
            k = k_ref[h, 0, pl.ds(start, bq), :]
            c = negc_ref[h, 0, pl.ds(start, bq), :]
            s = jnp.concatenate(
                [lax.dot_general(jnp.where(low, k, c), qs[h][0], NT_DIMS,
                                 preferred_element_type=F32),
                 lax.dot_general(jnp.where(low, c, k), qs[h][1], NT_DIMS,
                                 preferred_element_type=F32)], axis=1)
            return s if mask is None else jnp.where(mask, s, NEG)
        return h, score, lambda: _vt_per_head(vt_ref, h, kb, bq)

    def step(kbs, carries, mask=None):
        return _softmax_t_blocks([item(h, kb, mask) for kb in kbs for h in range(N_CB)], carries)

    carries = _pair_loop(qi, lambda i: i, step, step((qi,), None, causal))
    for h in range(N_CB):
        a = _normalized_t(carries[h][1])
        _store_gated(o_ref, gate_ref, h, jnp.concatenate([a[:, :bq], a[:, bq:]], axis=0).T)


def _fox_attention(proj4, vt, negc, bq):
    _, b, s, _ = proj4.shape
    return pl.pallas_call(
        _fox_kernel,
        grid=(b, s // bq),
        in_specs=[
            _q_spec(bq, CB_FQ), _kv_spec(s, CB_FK),
            pl.BlockSpec((N_CB, LANES, s), lambda b, i: (VT_F, 0, b)),
            _gate_spec(bq, 2),
            pl.BlockSpec((N_CB, 1, s, LANES), lambda b, i: (0, b, 0, 0)),
        ],
        out_specs=_out_spec(bq),
        out_shape=jax.ShapeDtypeStruct((b, s, GROUP_W), BF16),
        compiler_params=_params(2),
        name="fox_attention",
    )(proj4, proj4, vt, proj4, negc)


def _mem_kernel(q_ref, km_ref, vmt_ref, gate_ref, o_ref):
    scale = LANES ** -0.5
    scores = [lax.dot_general(km_ref[h, 0], q_ref[h, 0], NT_DIMS,
                              preferred_element_type=F32) * scale for h in range(N_CB)]
    probs = [jnp.exp(s - jnp.max(s, axis=0, keepdims=True)).astype(BF16) for s in scores]
    for h in range(N_CB):
        acc = jnp.dot(_with_ones_rows(vmt_ref[h]), probs[h], preferred_element_type=F32)
        _store_gated(o_ref, gate_ref, h, _normalized_t(acc).T)


def _mem_attention(proj4, km4, vmt, bq):
    _, b, s, _ = proj4.shape
    mlen = km4.shape[2]
    return pl.pallas_call(
        _mem_kernel,
        grid=(b, s // bq),
        in_specs=[
            _q_spec(bq, CB_MQ),
            pl.BlockSpec((N_CB, 1, mlen, LANES), lambda b, i: (0, b, 0, 0)),
            pl.BlockSpec((N_CB, LANES, mlen), lambda b, i: (0, 0, b)),
            _gate_spec(bq, 3),
        ],
        out_specs=_out_spec(bq),
        out_shape=jax.ShapeDtypeStruct((b, s, GROUP_W), BF16),
        compiler_params=_params(2),
        name="mem_attention",
    )(proj4, km4, vmt, proj4)


def _out_proj_kernel(ya_ref, yb_ref, yc_ref, ym_ref, w_ref, g_ref, x_ref, o_ref):
    y = jnp.concatenate([ya_ref[...], yb_ref[...], yc_ref[...], ym_ref[...]], axis=-1)
    z = jnp.dot(y, w_ref[...], preferred_element_type=F32)
    z = z * lax.rsqrt(jnp.mean(z * z, axis=-1, keepdims=True) + EPS)
    o_ref[...] = x_ref[...] + z * g_ref[...]


def _out_proj(ys, w, g, x2d, tm):
    m, d = x2d.shape
    y_spec = pl.BlockSpec((tm, GROUP_W), lambda i: (i, 0))
    return pl.pallas_call(
        _out_proj_kernel,
        grid=(m // tm,),
        in_specs=[y_spec] * 4 + [
            pl.BlockSpec(w.shape, lambda i: (0, 0)),
            pl.BlockSpec((1, d), lambda i: (0, 0)),
            pl.BlockSpec((tm, d), lambda i: (i, 0)),
        ],
        out_specs=pl.BlockSpec((tm, d), lambda i: (i, 0)),
        out_shape=jax.ShapeDtypeStruct((m, d), F32),
        compiler_params=_params(1),
        name="out_proj",
    )(*ys, w, g, x2d)


def _pick(n, cands):
    for c in cands:
        if n % c == 0:
            return c
    raise ValueError(f"no tile in {cands} divides {n}")


def kernel(x, mem, w_in, b_forget, w_mem_kv, w_out, pre_norm, post_norm, mem_norm,
           diff_lambda, diff_head_norm, rel_bias):
    b, s, d = x.shape
    mlen = mem.shape[1]
    depth = w_in.shape[0]
    m = b * s
    bq = _pick(s, (256, 128))
    tm = _pick(m, (1024, 512, 256))

    gw = GROUP_W
    logit_lo, logit_hi = 9 * gw, 9 * gw + N_FOX_HEADS
    tiles = _bias_tiles(rel_bias, bq, bq)
    mem2d = mem.reshape(b * mlen, d)
    x2d = x.reshape(m, d)

    for l in range(depth):
        wl = w_in[l]
        w_main = jnp.concatenate([wl[:, 0:2 * gw], wl[:, 3 * gw:5 * gw], wl[:, 6 * gw:8 * gw],
                                  wl[:, logit_hi:]], axis=1).astype(BF16)
        w_vt = jnp.concatenate([wl[:, 2 * gw:3 * gw], wl[:, 5 * gw:6 * gw], wl[:, 8 * gw:9 * gw]],
                               axis=1).T.astype(BF16)
        w_logit = jnp.pad(wl[:, logit_lo:logit_hi], ((0, 0), (0, LANES - N_FOX_HEADS))).astype(BF16)
        b_pad = jnp.pad(b_forget[l], (0, LANES - N_FOX_HEADS)).reshape(1, LANES)

        tn = _pick(w_main.shape[1], (1408, 512))
        proj, logit, vt = _in_proj(x2d, pre_norm[l].reshape(1, d), w_main, w_logit, w_vt, tm, tn)
        proj4 = proj.reshape(N_PROJ_CB, b, s, LANES)
        negc = _forget_cumsum(logit.reshape(b, s, LANES), b_pad, _pick(s, (512, 256, 128)))

        w_kv = w_mem_kv[l]
        km, vmt = _mem_kv(mem2d, mem_norm[l].reshape(1, d), w_kv[:, :gw].astype(BF16),
                          w_kv[:, gw:].T.astype(BF16), _pick(b * mlen, (1024, 512, 256)))
        km4 = km.reshape(N_CB, b, mlen, LANES)

        lam_init = 0.8 - 0.6 * math.exp(-0.3 * l)
        y_a = _diff_attention(proj4, vt, tiles, diff_lambda[l],
                              diff_head_norm[l].reshape(LANES, 1), lam_init, bq)
        y_b = _stick_attention(proj4, vt, bq)
        y_c = _fox_attention(proj4, vt, negc, bq)
        y_m = _mem_attention(proj4, km4, vmt, _pick(s, (1024, 512, 256, 128)))
        ys = [y.reshape(m, GROUP_W) for y in (y_a, y_b, y_c, y_m)]
        x2d = _out_proj(ys, w_out[l].astype(BF16), post_norm[l].reshape(1, d), x2d,
                        _pick(m, (1024, 512, 256)))
    return x2d.reshape(b, s, d)
```

```python
import functools
import math

import jax
import jax.numpy as jnp
import numpy as np
from jax import lax
from jax.experimental import pallas as pl
from jax.experimental.pallas import tpu as pltpu

F32 = jnp.float32
BF16 = jnp.bfloat16

LANES = 128
BF16_ROWS = 16
HEAD_DIM = 64
GROUP_W = 512
N_CB = GROUP_W // LANES
N_FOX_HEADS = GROUP_W // HEAD_DIM
NUM_BUCKETS = 32
MAX_DISTANCE = 128
EPS = 1e-6
NEG = -1e30
LOG2E = 1.4426950408889634
VMEM_LIMIT = 56 * 1024 * 1024

CB_AQ, CB_AK = 0, 4
CB_SQ, CB_SK = 8, 12
CB_FQ, CB_FK = 16, 20
CB_MQ, CB_GATE = 24, 28
N_PROJ_CB = 44
VT_A, VT_S, VT_F = 0, 1, 2

NT_DIMS = (((1,), (1,)), ((), ()))


def _params(n_axes):
    return pltpu.CompilerParams(dimension_semantics=("arbitrary",) * n_axes,
                                vmem_limit_bytes=VMEM_LIMIT)


def _split3(x):
    hi = x.astype(BF16)
    r = x - hi.astype(F32)
    mid = r.astype(BF16)
    lo = (r - mid.astype(F32)).astype(BF16)
    return hi, mid, lo


def _silu(g):
    g = g.astype(F32)
    return g / (1.0 + jnp.exp(-g))


def _rms_rows(x, g_ref):
    y = x * lax.rsqrt(jnp.mean(x * x, axis=-1, keepdims=True) + EPS)
    return (y * g_ref[...]).astype(BF16)


def _store_col_blocks(o_ref, res):
    for c in range(o_ref.shape[0]):
        o_ref[c] = res[:, c * LANES:(c + 1) * LANES]


def _in_proj_kernel(x_ref, g_ref, w_ref, wl_ref, wvt_ref, o_ref, ol_ref, ovt_ref, h_ref):
    @pl.when(pl.program_id(1) == 0)
    def _():
        h = _rms_rows(x_ref[...], g_ref)
        h_ref[...] = h
        ol_ref[...] = jnp.dot(h, wl_ref[...], preferred_element_type=F32)
        vt = lax.dot_general(wvt_ref[...], h, NT_DIMS, preferred_element_type=F32).astype(BF16)
        for c in range(ovt_ref.shape[0]):
            ovt_ref[c] = vt[c * LANES:(c + 1) * LANES, :]

    res = jnp.dot(h_ref[...], w_ref[...], preferred_element_type=F32).astype(BF16)
    _store_col_blocks(o_ref, res)


def _in_proj(x2d, g, w, wl, wvt, tm, tn):
    m, k = x2d.shape
    n = w.shape[1]
    nv = wvt.shape[0]
    return pl.pallas_call(
        _in_proj_kernel,
        grid=(m // tm, n // tn),
        in_specs=[
            pl.BlockSpec((tm, k), lambda i, j: (i, 0)),
            pl.BlockSpec((1, k), lambda i, j: (0, 0)),
            pl.BlockSpec((k, tn), lambda i, j: (0, j)),
            pl.BlockSpec((k, LANES), lambda i, j: (0, 0)),
            pl.BlockSpec((nv, k), lambda i, j: (0, 0)),
        ],
        out_specs=[
            pl.BlockSpec((tn // LANES, tm, LANES), lambda i, j: (j, i, 0)),
            pl.BlockSpec((tm, LANES), lambda i, j: (i, 0)),
            pl.BlockSpec((nv // LANES, LANES, tm), lambda i, j: (0, 0, i)),
        ],
        out_shape=[
            jax.ShapeDtypeStruct((n // LANES, m, LANES), BF16),
            jax.ShapeDtypeStruct((m, LANES), F32),
            jax.ShapeDtypeStruct((nv // LANES, LANES, m), BF16),
        ],
        scratch_shapes=[pltpu.VMEM((tm, k), BF16)],
        compiler_params=_params(2),
        name="in_proj",
    )(x2d, g, w, wl, wvt)


def _mem_kv_kernel(x_ref, g_ref, wk_ref, wvt_ref, ok_ref, ovt_ref):
    h = _rms_rows(x_ref[...], g_ref)
    _store_col_blocks(ok_ref, jnp.dot(h, wk_ref[...], preferred_element_type=F32).astype(BF16))
    vt = lax.dot_general(wvt_ref[...], h, NT_DIMS, preferred_element_type=F32).astype(BF16)
    for c in range(ovt_ref.shape[0]):
        ovt_ref[c] = vt[c * LANES:(c + 1) * LANES, :]


def _mem_kv(x2d, g, wk, wvt, tm):
    m, k = x2d.shape
    n = wk.shape[1]
    return pl.pallas_call(
        _mem_kv_kernel,
        grid=(m // tm,),
        in_specs=[
            pl.BlockSpec((tm, k), lambda i: (i, 0)),
            pl.BlockSpec((1, k), lambda i: (0, 0)),
            pl.BlockSpec((k, n), lambda i: (0, 0)),
            pl.BlockSpec((n, k), lambda i: (0, 0)),
        ],
        out_specs=[
            pl.BlockSpec((n // LANES, tm, LANES), lambda i: (0, i, 0)),
            pl.BlockSpec((n // LANES, LANES, tm), lambda i: (0, 0, i)),
        ],
        out_shape=[
            jax.ShapeDtypeStruct((n // LANES, m, LANES), BF16),
            jax.ShapeDtypeStruct((n // LANES, LANES, m), BF16),
        ],
        compiler_params=_params(1),
        name="mem_kv",
    )(x2d, g, wk, wvt)


def _forget_cumsum_kernel(logit_ref, b_ref, tri_ref, place_ref, o_ref, carry_ref):
    @pl.when(pl.program_id(1) == 0)
    def _():
        carry_ref[...] = jnp.zeros_like(carry_ref)

    tb = logit_ref.shape[1]
    z = logit_ref[0] + b_ref[...]
    log_f = jnp.minimum(z, 0.0) - jnp.log1p(jnp.exp(-jnp.abs(z)))
    sums = jnp.dot(tri_ref[...], jnp.concatenate(_split3(log_f), axis=1),
                   preferred_element_type=F32)
    c = carry_ref[...] + (sums[:, :LANES] + sums[:, LANES:2 * LANES] + sums[:, 2 * LANES:])
    carry_ref[...] = c[tb - 1:tb, :]
    ext = jnp.dot(jnp.concatenate(_split3(-c), axis=1), place_ref[...],
                  preferred_element_type=F32)
    for h in range(N_CB):
        o_ref[h, 0] = ext[:, h * LANES:(h + 1) * LANES].astype(BF16)


def _placement():
    place = np.zeros((3 * LANES, N_CB * LANES), np.float32)
    for term in range(3):
        for h in range(N_CB):
            place[term * LANES + 2 * h, h * LANES + HEAD_DIM + term] = 1.0
            place[term * LANES + 2 * h + 1, h * LANES + term] = 1.0
    return jnp.asarray(place, BF16)


def _forget_cumsum(logit3, b_pad, tb):
    b, s, _ = logit3.shape
    tri = jnp.asarray(np.tril(np.ones((tb, tb), np.float32)), BF16)
    place = _placement()
    return pl.pallas_call(
        _forget_cumsum_kernel,
        grid=(b, s // tb),
        in_specs=[
            pl.BlockSpec((1, tb, LANES), lambda i, j: (i, j, 0)),
            pl.BlockSpec((1, LANES), lambda i, j: (0, 0)),
            pl.BlockSpec((tb, tb), lambda i, j: (0, 0)),
            pl.BlockSpec(place.shape, lambda i, j: (0, 0)),
        ],
        out_specs=pl.BlockSpec((N_CB, 1, tb, LANES), lambda i, j: (0, i, j, 0)),
        out_shape=jax.ShapeDtypeStruct((N_CB, b, s, LANES), BF16),
        scratch_shapes=[pltpu.VMEM((1, LANES), F32)],
        compiler_params=_params(2),
        name="forget_cumsum",
    )(logit3, b_pad, tri, place)


def _bias_tiles_kernel(rb_ref, o_ref):
    h = pl.program_id(0)
    j = pl.program_id(1)
    bk, bq = o_ref.shape[2], o_ref.shape[3]
    key = lax.broadcasted_iota(jnp.int32, (bk, bq), 0)
    query = lax.broadcasted_iota(jnp.int32, (bk, bq), 1)
    dist = query - key + j * bk
    n = jnp.maximum(dist, 0)
    max_exact = NUM_BUCKETS // 2
    nf = jnp.maximum(n, 1).astype(F32)
    large = max_exact + (jnp.log(nf / max_exact) / math.log(MAX_DISTANCE / max_exact)
                         * (NUM_BUCKETS - max_exact)).astype(jnp.int32)
    large = jnp.minimum(large, NUM_BUCKETS - 1)
    bucket = jnp.where(n < max_exact, n, large)
    far = rb_ref[NUM_BUCKETS - 1, h]
    bias = jnp.zeros((bk, bq), F32)
    for b in range(NUM_BUCKETS - 1):
        bias = jnp.where(bucket == b, rb_ref[b, h] - far, bias)
    o_ref[0, 0] = jnp.where(dist >= 0, bias, NEG)


def _bias_tiles(rel_bias, bk, bq):
    n_heads = rel_bias.shape[1]
    return pl.pallas_call(
        _bias_tiles_kernel,
        grid=(n_heads, 2),
        in_specs=[pl.BlockSpec(memory_space=pltpu.SMEM)],
        out_specs=pl.BlockSpec((1, 1, bk, bq), lambda h, j: (h, j, 0, 0)),
        out_shape=jax.ShapeDtypeStruct((n_heads, 2, bk, bq), F32),
        compiler_params=_params(2),
        name="bias_tiles",
    )(rel_bias)


def _stack_halves(q):
    lane = lax.broadcasted_iota(jnp.int32, q.shape, 1)
    zero = jnp.zeros_like(q)
    return jnp.concatenate([jnp.where(lane < HEAD_DIM, q, zero),
                            jnp.where(lane < HEAD_DIM, zero, q)], axis=0)


def _key_block(ref, h, kb, bk):
    return ref[h, 0, pl.ds(pl.multiple_of(kb * bk, bk), bk), :]


def _q_spec(bq, cb):
    return pl.BlockSpec((N_CB, 1, bq, LANES), lambda b, i: (cb // N_CB, b, i, 0))


def _kv_spec(s, cb):
    return pl.BlockSpec((N_CB, 1, s, LANES), lambda b, i: (cb // N_CB, b, 0, 0))


def _gate_spec(bq, group):
    return pl.BlockSpec((N_CB, 1, bq, LANES), lambda b, i: (CB_GATE // N_CB + group, b, i, 0))


def _out_spec(bq):
    return pl.BlockSpec((1, bq, GROUP_W), lambda b, i: (b, i, 0))


def _store_gated(o_ref, gate_ref, h, o):
    o_ref[0, :, h * LANES:(h + 1) * LANES] = (o * _silu(gate_ref[h, 0])).astype(o_ref.dtype)


def _with_ones_rows(vt):
    return jnp.concatenate([vt, jnp.ones((BF16_ROWS, vt.shape[1]), BF16)], axis=0)


def _vt_shared(vt_ref, h, kb, bk):
    vt = _with_ones_rows(vt_ref[h, :, pl.ds(pl.multiple_of(kb * bk, bk), bk)])
    return vt, vt


def _vt_per_head(vt_ref, h, kb, bk):
    vt = vt_ref[h, :, pl.ds(pl.multiple_of(kb * bk, bk), bk)]
    return _with_ones_rows(vt[:HEAD_DIM]), _with_ones_rows(vt[HEAD_DIM:])


def _value_t(vts, p):
    n = p.shape[1] // 2
    return jnp.concatenate([jnp.dot(vts[0], p[:, :n], preferred_element_type=F32),
                            jnp.dot(vts[1], p[:, n:], preferred_element_type=F32)], axis=1)


def _scores_t(q_st, k_ref, h, kb, bk):
    return lax.dot_general(_key_block(k_ref, h, kb, bk), q_st, NT_DIMS,
                           preferred_element_type=F32)


VALUE_LAG = 2


def _softmax_t_blocks(items, carries):
    ms = [None if carries is None else carries[h][0] for h in range(N_CB)]
    accs = [None if carries is None else carries[h][1] for h in range(N_CB)]
    pending = {}

    def score(i):
        h, score_fn, _ = items[i]
        s = score_fn()
        m = jnp.max(s, axis=0, keepdims=True)
        m_new = m if ms[h] is None else jnp.maximum(ms[h], m)
        pending[i] = (ms[h], m_new, jnp.exp(s - m_new).astype(BF16))
        ms[h] = m_new

    def value(i):
        h, _, vt_fn = items[i]
        m_old, m_new, p = pending.pop(i)
        pv = _value_t(vt_fn(), p)
        accs[h] = pv if accs[h] is None else jnp.exp(m_old - m_new) * accs[h] + pv

    for i in range(len(items) + VALUE_LAG):
        if i < len(items):
            score(i)
        if i >= VALUE_LAG:
            value(i - VALUE_LAG)
    return tuple((ms[h], accs[h]) for h in range(N_CB))


def _pair_loop(n_blocks, block_of, step, carries, state_refs=None):
    if state_refs is None:
        carries = lax.fori_loop(
            0, n_blocks // 2, lambda i, c: step((block_of(2 * i), block_of(2 * i + 1)), c),
            carries)
        return lax.fori_loop(
            0, n_blocks % 2, lambda i, c: step((block_of(n_blocks - 1),), c), carries)

    def load():
        return tuple(tuple(ref[h] for ref in state_refs) for h in range(N_CB))

    def run(kbs):
        for h, chain in enumerate(step(kbs, load())):
            for ref, value in zip(state_refs, chain):
                ref[h] = value

    for h, chain in enumerate(carries):
        for ref, value in zip(state_refs, chain):
            ref[h] = value

    def pair(i, _):
        run((block_of(2 * i), block_of(2 * i + 1)))
        return 0

    lax.fori_loop(0, n_blocks // 2, pair, 0)

    @pl.when(n_blocks % 2 == 1)
    def _():
        run((block_of(n_blocks - 1),))

    return load()


def _normalized_t(acc):
    d = acc.shape[0] - BF16_ROWS
    return acc[:d] * (1.0 / acc[d:d + 1])


def _diff_kernel(lam_init, q_ref, k_ref, vt_ref, gate_ref, tile_ref, lamp_ref, hn_ref, o_ref,
                 m_ref, acc_ref):
    bq = q_ref.shape[2]
    qi = pl.program_id(1)
    qs = [_stack_halves(q_ref[h, 0] * (HEAD_DIM ** -0.5)) for h in range(N_CB)]

    def item(h, kb, tile_fn=None):
        def score():
            s = _scores_t(qs[h], k_ref, h, kb, bq)
            if tile_fn is None:
                return s
            tile = tile_fn(h)
            return s + jnp.concatenate([tile, tile], axis=1)
        return h, score, lambda: _vt_shared(vt_ref, h, kb, bq)

    def step(kbs, carries):
        return _softmax_t_blocks([item(h, kb) for kb in kbs for h in range(N_CB)], carries)

    near = ([item(h, qi, lambda h: tile_ref[h, 0]) for h in range(N_CB)]
            + [item(h, jnp.maximum(qi - 1, 0), lambda h: jnp.where(qi >= 1, tile_ref[h, 1], NEG))
               for h in range(N_CB)])
    carries = _softmax_t_blocks(near, None)
    carries = _pair_loop(jnp.maximum(qi - 1, 0), lambda i: i, step, carries, (m_ref, acc_ref))

    lp = lamp_ref[...]
    lam = (jnp.exp(jnp.sum(lp[0:1] * lp[1:2], axis=-1, keepdims=True))
           - jnp.exp(jnp.sum(lp[2:3] * lp[3:4], axis=-1, keepdims=True)) + lam_init)
    head_gain = jnp.broadcast_to(hn_ref[...] * (1.0 - lam_init), (LANES, bq))
    for h in range(N_CB):
        a = _normalized_t(carries[h][1])
        o = a[:, :bq] - lam * a[:, bq:]
        o = o * lax.rsqrt(jnp.mean(o * o, axis=0, keepdims=True) + EPS)
        _store_gated(o_ref, gate_ref, h, (o * head_gain).T)


def _diff_attention(proj4, vt, tiles, lam_params, head_norm, lam_init, bq):
    _, b, s, _ = proj4.shape
    return pl.pallas_call(
        functools.partial(_diff_kernel, lam_init),
        grid=(b, s // bq),
        in_specs=[
            _q_spec(bq, CB_AQ), _kv_spec(s, CB_AK),
            pl.BlockSpec((N_CB, LANES, s), lambda b, i: (VT_A, 0, b)),
            _gate_spec(bq, 0),
            pl.BlockSpec((N_CB, 2, bq, bq), lambda b, i: (0, 0, 0, 0)),
            pl.BlockSpec((4, HEAD_DIM), lambda b, i: (0, 0)),
            pl.BlockSpec((LANES, 1), lambda b, i: (0, 0)),
        ],
        out_specs=_out_spec(bq),
        out_shape=jax.ShapeDtypeStruct((b, s, GROUP_W), BF16),
        scratch_shapes=[pltpu.VMEM((N_CB, 1, 2 * bq), F32),
                        pltpu.VMEM((N_CB, LANES + BF16_ROWS, 2 * bq), F32)],
        compiler_params=_params(2),
        name="diff_attention",
    )(proj4, proj4, vt, proj4, tiles, lam_params, head_norm)


def _stick_kernel(q_ref, k_ref, vt_ref, gate_ref, o_ref, rest_ref, acc_ref):
    bq = q_ref.shape[2]
    qi = pl.program_id(1)
    qs = [_stack_halves(q_ref[h, 0] * (HEAD_DIM ** -0.5)) for h in range(N_CB)]
    key = lax.broadcasted_iota(jnp.int32, (bq, 2 * bq), 0)
    query = lax.broadcasted_iota(jnp.int32, (bq, 2 * bq), 1) & (bq - 1)
    strict = key < query

    row = lax.broadcasted_iota(jnp.int32, (bq + BF16_ROWS, bq), 0)
    col = lax.broadcasted_iota(jnp.int32, (bq + BF16_ROWS, bq), 1)
    after = jnp.where((col > row) | (row >= bq), 1.0, 0.0).astype(BF16)

    def step(kbs, carries, mask=None):
        items = [(h, kb) for kb in kbs for h in range(N_CB)]
        rests = [carries[h][0] for h in range(N_CB)]
        accs = [carries[h][1] for h in range(N_CB)]
        pending = {}

        def score(i):
            h, kb = items[i]
            z = _scores_t(qs[h], k_ref, h, kb, bq)
            sp = jnp.log(1.0 + jnp.exp2(jnp.abs(z) * -LOG2E))
            ls_pos = jnp.minimum(z, 0.0) - sp
            ls_neg = ls_pos - z
            if mask is not None:
                ls_neg = jnp.where(mask, ls_neg, 0.0)
            pending[i] = (ls_pos, ls_neg.astype(BF16))

        def cumsum(i):
            h, _ = items[i]
            ls_pos, log_rest = pending[i]
            t = jnp.dot(after, log_rest, preferred_element_type=F32)
            a = jnp.exp2((ls_pos + t[:bq]) * LOG2E)
            if mask is not None:
                a = jnp.where(mask, a, 0.0)
            pending[i] = (a.astype(BF16), jnp.exp2(rests[h] * LOG2E))
            rests[h] = rests[h] + t[bq:bq + 1]

        def value(i):
            h, kb = items[i]
            a, scale = pending.pop(i)
            vt = vt_ref[h, :, pl.ds(pl.multiple_of(kb * bq, bq), bq)]
            accs[h] = accs[h] + scale * _value_t((vt[:HEAD_DIM], vt[HEAD_DIM:]), a)

        for i in range(len(items) + 2):
            if i < len(items):
                score(i)
            if 1 <= i <= len(items):
                cumsum(i - 1)
            if i >= 2:
                value(i - 2)
        return tuple((rests[h], accs[h]) for h in range(N_CB))

    zero = (jnp.zeros((1, 2 * bq), F32), jnp.zeros((HEAD_DIM, 2 * bq), F32))
    carries = step((qi,), (zero,) * N_CB, strict)
    carries = _pair_loop(qi, lambda i: qi - 1 - i, step, carries, (rest_ref, acc_ref))
    for h in range(N_CB):
        acc = carries[h][1]
        _store_gated(o_ref, gate_ref, h, jnp.concatenate([acc[:, :bq], acc[:, bq:]], axis=0).T)


def _stick_attention(proj4, vt, bq):
    _, b, s, _ = proj4.shape
    return pl.pallas_call(
        _stick_kernel,
        grid=(b, s // bq),
        in_specs=[
            _q_spec(bq, CB_SQ), _kv_spec(s, CB_SK),
            pl.BlockSpec((N_CB, LANES, s), lambda b, i: (VT_S, 0, b)),
            _gate_spec(bq, 1),
        ],
        out_specs=_out_spec(bq),
        out_shape=jax.ShapeDtypeStruct((b, s, GROUP_W), BF16),
        scratch_shapes=[pltpu.VMEM((N_CB, 1, 2 * bq), F32),
                        pltpu.VMEM((N_CB, HEAD_DIM, 2 * bq), F32)],
        compiler_params=_params(2),
        name="stick_attention",
    )(proj4, proj4, vt, proj4)


def _fox_kernel(q_ref, k_ref, vt_ref, gate_ref, negc_ref, o_ref, m_ref, acc_ref):
    bq = q_ref.shape[2]
    qi = pl.program_id(1)
    key = lax.broadcasted_iota(jnp.int32, (bq, 2 * bq), 0)
    query = lax.broadcasted_iota(jnp.int32, (bq, 2 * bq), 1) & (bq - 1)
    causal = key <= query

    lane = lax.broadcasted_iota(jnp.int32, (bq, LANES), 1)
    low = lane < HEAD_DIM
    ones_low = jnp.where(lane < 3, 1.0, 0.0).astype(BF16)
    ones_high = jnp.where(low | (lane >= HEAD_DIM + 3), 0.0, 1.0).astype(BF16)
    qs = []
    for h in range(N_CB):
        q = q_ref[h, 0] * (HEAD_DIM ** -0.5)
        qs.append((jnp.where(low, q, ones_high), jnp.where(low, ones_low, q)))

    def item(h, kb, mask):
        def score():
            start = pl.multiple_of(kb * bq, bq)
            k = k_ref[h, 0, pl.ds(start, bq), :]
            c = negc_ref[h, 0, pl.ds(start, bq), :]
            s = jnp.concatenate(
                [lax.dot_general(jnp.where(low, k, c), qs[h][0], NT_DIMS,
                                 preferred_element_type=F32),
                 lax.dot_general(jnp.where(low, c, k), qs[h][1], NT_DIMS,
                                 preferred_element_type=F32)], axis=1)
            return s if mask is None else jnp.where(mask, s, NEG)
        return h, score, lambda: _vt_per_head(vt_ref, h, kb, bq)

    def step(kbs, carries, mask=None):
        return _softmax_t_blocks([item(h, kb, mask) for kb in kbs for h in range(N_CB)], carries)

    carries = _pair_loop(qi, lambda i: i, step, step((qi,), None, causal), (m_ref, acc_ref))
    for h in range(N_CB):
        a = _normalized_t(carries[h][1])
        _store_gated(o_ref, gate_ref, h, jnp.concatenate([a[:, :bq], a[:, bq:]], axis=0).T)


def _fox_attention(proj4, vt, negc, bq):
    _, b, s, _ = proj4.shape
    return pl.pallas_call(
        _fox_kernel,
        grid=(b, s // bq),
        in_specs=[
            _q_spec(bq, CB_FQ), _kv_spec(s, CB_FK),
            pl.BlockSpec((N_CB, LANES, s), lambda b, i: (VT_F, 0, b)),
            _gate_spec(bq, 2),
            pl.BlockSpec((N_CB, 1, s, LANES), lambda b, i: (0, b, 0, 0)),
        ],
        out_specs=_out_spec(bq),
        out_shape=jax.ShapeDtypeStruct((b, s, GROUP_W), BF16),
        scratch_shapes=[pltpu.VMEM((N_CB, 1, 2 * bq), F32),
                        pltpu.VMEM((N_CB, HEAD_DIM + BF16_ROWS, 2 * bq), F32)],
        compiler_params=_params(2),
        name="fox_attention",
    )(proj4, proj4, vt, proj4, negc)


def _mem_kernel(q_ref, km_ref, vmt_ref, gate_ref, o_ref):
    scale = LANES ** -0.5
    scores = [lax.dot_general(km_ref[h, 0], q_ref[h, 0], NT_DIMS,
                              preferred_element_type=F32) * scale for h in range(N_CB)]
    probs = [jnp.exp(s - jnp.max(s, axis=0, keepdims=True)).astype(BF16) for s in scores]
    for h in range(N_CB):
        acc = jnp.dot(_with_ones_rows(vmt_ref[h]), probs[h], preferred_element_type=F32)
        _store_gated(o_ref, gate_ref, h, _normalized_t(acc).T)


def _mem_attention(proj4, km4, vmt, bq):
    _, b, s, _ = proj4.shape
    mlen = km4.shape[2]
    return pl.pallas_call(
        _mem_kernel,
        grid=(b, s // bq),
        in_specs=[
            _q_spec(bq, CB_MQ),
            pl.BlockSpec((N_CB, 1, mlen, LANES), lambda b, i: (0, b, 0, 0)),
            pl.BlockSpec((N_CB, LANES, mlen), lambda b, i: (0, 0, b)),
            _gate_spec(bq, 3),
        ],
        out_specs=_out_spec(bq),
        out_shape=jax.ShapeDtypeStruct((b, s, GROUP_W), BF16),
        compiler_params=_params(2),
        name="mem_attention",
    )(proj4, km4, vmt, proj4)


def _out_proj_kernel(ya_ref, yb_ref, yc_ref, ym_ref, w_ref, g_ref, x_ref, o_ref):
    y = jnp.concatenate([ya_ref[...], yb_ref[...], yc_ref[...], ym_ref[...]], axis=-1)
    z = jnp.dot(y, w_ref[...], preferred_element_type=F32)
    z = z * lax.rsqrt(jnp.mean(z * z, axis=-1, keepdims=True) + EPS)
    o_ref[...] = x_ref[...] + z * g_ref[...]


def _out_proj(ys, w, g, x2d, tm):
    m, d = x2d.shape
    y_spec = pl.BlockSpec((tm, GROUP_W), lambda i: (i, 0))
    return pl.pallas_call(
        _out_proj_kernel,
        grid=(m // tm,),
        in_specs=[y_spec] * 4 + [
            pl.BlockSpec(w.shape, lambda i: (0, 0)),
            pl.BlockSpec((1, d), lambda i: (0, 0)),
            pl.BlockSpec((tm, d), lambda i: (i, 0)),
        ],
        out_specs=pl.BlockSpec((tm, d), lambda i: (i, 0)),
        out_shape=jax.ShapeDtypeStruct((m, d), F32),
        compiler_params=_params(1),
        name="out_proj",
    )(*ys, w, g, x2d)


def _pick(n, cands):
    for c in cands:
        if n % c == 0:
            return c
    raise ValueError(f"no tile in {cands} divides {n}")


def kernel(x, mem, w_in, b_forget, w_mem_kv, w_out, pre_norm, post_norm, mem_norm,
           diff_lambda, diff_head_norm, rel_bias):
    b, s, d = x.shape
    mlen = mem.shape[1]
    depth = w_in.shape[0]
    m = b * s
    bq = _pick(s, (256, 128))
    tm = _pick(m, (1024, 512, 256))

    gw = GROUP_W
    logit_lo, logit_hi = 9 * gw, 9 * gw + N_FOX_HEADS
    tiles = _bias_tiles(rel_bias, bq, bq)
    mem2d = mem.reshape(b * mlen, d)
    x2d = x.reshape(m, d)

    for l in range(depth):
        wl = w_in[l].astype(BF16)
        w_main = jnp.concatenate([wl[:, 0:2 * gw], wl[:, 3 * gw:5 * gw], wl[:, 6 * gw:8 * gw],
                                  wl[:, logit_hi:]], axis=1)
        w_vt = jnp.concatenate([wl[:, 2 * gw:3 * gw], wl[:, 5 * gw:6 * gw], wl[:, 8 * gw:9 * gw]],
                               axis=1).T
        w_logit = jnp.pad(wl[:, logit_lo:logit_hi], ((0, 0), (0, LANES - N_FOX_HEADS)))
        b_pad = jnp.pad(b_forget[l], (0, LANES - N_FOX_HEADS)).reshape(1, LANES)

        tn = _pick(w_main.shape[1], (1408, 512))
        proj, logit, vt = _in_proj(x2d, pre_norm[l].reshape(1, d), w_main, w_logit, w_vt, tm, tn)
        proj4 = proj.reshape(N_PROJ_CB, b, s, LANES)
        negc = _forget_cumsum(logit.reshape(b, s, LANES), b_pad, _pick(s, (512, 256, 128)))

        w_kv = w_mem_kv[l].astype(BF16)
        km, vmt = _mem_kv(mem2d, mem_norm[l].reshape(1, d), w_kv[:, :gw], w_kv[:, gw:].T,
                          _pick(b * mlen, (1024, 512, 256)))
        km4 = km.reshape(N_CB, b, mlen, LANES)

        lam_init = 0.8 - 0.6 * math.exp(-0.3 * l)
        y_a = _diff_attention(proj4, vt, tiles, diff_lambda[l],
                              diff_head_norm[l].reshape(LANES, 1), lam_init, bq)
        y_b = _stick_attention(proj4, vt, bq)
        y_c = _fox_attention(proj4, vt, negc, bq)
        y_m = _mem_attention(proj4, km4, vmt, _pick(s, (1024, 512, 256, 128)))
        ys = [y.reshape(m, GROUP_W) for y in (y_a, y_b, y_c, y_m)]
        x2d = _out_proj(ys, w_out[l].astype(BF16), post_norm[l].reshape(1, d), x2d,
                        _pick(m, (1024, 512, 256)))
    return x2d.reshape(b, s, d)
```

```python
import functools
import math

import jax
import jax.numpy as jnp
import numpy as np
from jax import lax
from jax.experimental import pallas as pl
from jax.experimental.pallas import tpu as pltpu

F32 = jnp.float32
BF16 = jnp.bfloat16

LANES = 128
BF16_ROWS = 16
HEAD_DIM = 64
GROUP_W = 512
N_CB = GROUP_W // LANES
N_FOX_HEADS = GROUP_W // HEAD_DIM
NUM_BUCKETS = 32
MAX_DISTANCE = 128
EPS = 1e-6
NEG = -1e30
LOG2E = 1.4426950408889634
VMEM_LIMIT = 56 * 1024 * 1024

CB_AQ, CB_AK = 0, 4
CB_SQ, CB_SK = 8, 12
CB_FQ, CB_FK = 16, 20
CB_MQ, CB_GATE = 24, 28
N_PROJ_CB = 44
VT_A, VT_S, VT_F = 0, 1, 2

NT_DIMS = (((1,), (1,)), ((), ()))


def _params(n_axes):
    return pltpu.CompilerParams(dimension_semantics=("arbitrary",) * n_axes,
                                vmem_limit_bytes=VMEM_LIMIT)


def _split3(x):
    hi = x.astype(BF16)
    r = x - hi.astype(F32)
    mid = r.astype(BF16)
    lo = (r - mid.astype(F32)).astype(BF16)
    return hi, mid, lo


def _silu(g):
    g = g.astype(F32)
    return g / (1.0 + jnp.exp(-g))


def _rms_rows(x, g_ref):
    y = x * lax.rsqrt(jnp.mean(x * x, axis=-1, keepdims=True) + EPS)
    return (y * g_ref[...]).astype(BF16)


def _store_col_blocks(o_ref, res):
    for c in range(o_ref.shape[0]):
        o_ref[c] = res[:, c * LANES:(c + 1) * LANES]


def _in_proj_kernel(x_ref, g_ref, w_ref, wl_ref, wvt_ref, o_ref, ol_ref, ovt_ref, h_ref):
    @pl.when(pl.program_id(1) == 0)
    def _():
        h = _rms_rows(x_ref[...], g_ref)
        h_ref[...] = h
        ol_ref[...] = jnp.dot(h, wl_ref[...], preferred_element_type=F32)
        vt = lax.dot_general(wvt_ref[...], h, NT_DIMS, preferred_element_type=F32).astype(BF16)
        for c in range(ovt_ref.shape[0]):
            ovt_ref[c] = vt[c * LANES:(c + 1) * LANES, :]

    res = jnp.dot(h_ref[...], w_ref[...], preferred_element_type=F32).astype(BF16)
    _store_col_blocks(o_ref, res)


def _in_proj(x2d, g, w, wl, wvt, tm, tn):
    m, k = x2d.shape
    n = w.shape[1]
    nv = wvt.shape[0]
    return pl.pallas_call(
        _in_proj_kernel,
        grid=(m // tm, n // tn),
        in_specs=[
            pl.BlockSpec((tm, k), lambda i, j: (i, 0)),
            pl.BlockSpec((1, k), lambda i, j: (0, 0)),
            pl.BlockSpec((k, tn), lambda i, j: (0, j)),
            pl.BlockSpec((k, LANES), lambda i, j: (0, 0)),
            pl.BlockSpec((nv, k), lambda i, j: (0, 0)),
        ],
        out_specs=[
            pl.BlockSpec((tn // LANES, tm, LANES), lambda i, j: (j, i, 0)),
            pl.BlockSpec((tm, LANES), lambda i, j: (i, 0)),
            pl.BlockSpec((nv // LANES, LANES, tm), lambda i, j: (0, 0, i)),
        ],
        out_shape=[
            jax.ShapeDtypeStruct((n // LANES, m, LANES), BF16),
            jax.ShapeDtypeStruct((m, LANES), F32),
            jax.ShapeDtypeStruct((nv // LANES, LANES, m), BF16),
        ],
        scratch_shapes=[pltpu.VMEM((tm, k), BF16)],
        compiler_params=_params(2),
        name="in_proj",
    )(x2d, g, w, wl, wvt)


def _mem_kv_kernel(x_ref, g_ref, wk_ref, wvt_ref, ok_ref, ovt_ref):
    h = _rms_rows(x_ref[...], g_ref)
    _store_col_blocks(ok_ref, jnp.dot(h, wk_ref[...], preferred_element_type=F32).astype(BF16))
    vt = lax.dot_general(wvt_ref[...], h, NT_DIMS, preferred_element_type=F32).astype(BF16)
    for c in range(ovt_ref.shape[0]):
        ovt_ref[c] = vt[c * LANES:(c + 1) * LANES, :]


def _mem_kv(x2d, g, wk, wvt, tm):
    m, k = x2d.shape
    n = wk.shape[1]
    return pl.pallas_call(
        _mem_kv_kernel,
        grid=(m // tm,),
        in_specs=[
            pl.BlockSpec((tm, k), lambda i: (i, 0)),
            pl.BlockSpec((1, k), lambda i: (0, 0)),
            pl.BlockSpec((k, n), lambda i: (0, 0)),
            pl.BlockSpec((n, k), lambda i: (0, 0)),
        ],
        out_specs=[
            pl.BlockSpec((n // LANES, tm, LANES), lambda i: (0, i, 0)),
            pl.BlockSpec((n // LANES, LANES, tm), lambda i: (0, 0, i)),
        ],
        out_shape=[
            jax.ShapeDtypeStruct((n // LANES, m, LANES), BF16),
            jax.ShapeDtypeStruct((n // LANES, LANES, m), BF16),
        ],
        compiler_params=_params(1),
        name="mem_kv",
    )(x2d, g, wk, wvt)


def _forget_cumsum_kernel(logit_ref, b_ref, tri_ref, place_ref, o_ref, carry_ref):
    @pl.when(pl.program_id(1) == 0)
    def _():
        carry_ref[...] = jnp.zeros_like(carry_ref)

    tb = logit_ref.shape[1]
    z = logit_ref[0] + b_ref[...]
    log_f = jnp.minimum(z, 0.0) - jnp.log1p(jnp.exp(-jnp.abs(z)))
    sums = jnp.dot(tri_ref[...], jnp.concatenate(_split3(log_f), axis=1),
                   preferred_element_type=F32)
    c = carry_ref[...] + (sums[:, :LANES] + sums[:, LANES:2 * LANES] + sums[:, 2 * LANES:])
    carry_ref[...] = c[tb - 1:tb, :]
    ext = jnp.dot(jnp.concatenate(_split3(-c), axis=1), place_ref[...],
                  preferred_element_type=F32)
    for h in range(N_CB):
        o_ref[h, 0] = ext[:, h * LANES:(h + 1) * LANES].astype(BF16)


def _placement():
    place = np.zeros((3 * LANES, N_CB * LANES), np.float32)
    for term in range(3):
        for h in range(N_CB):
            place[term * LANES + 2 * h, h * LANES + HEAD_DIM + term] = 1.0
            place[term * LANES + 2 * h + 1, h * LANES + term] = 1.0
    return jnp.asarray(place, BF16)


def _forget_cumsum(logit3, b_pad, tb):
    b, s, _ = logit3.shape
    tri = jnp.asarray(np.tril(np.ones((tb, tb), np.float32)), BF16)
    place = _placement()
    return pl.pallas_call(
        _forget_cumsum_kernel,
        grid=(b, s // tb),
        in_specs=[
            pl.BlockSpec((1, tb, LANES), lambda i, j: (i, j, 0)),
            pl.BlockSpec((1, LANES), lambda i, j: (0, 0)),
            pl.BlockSpec((tb, tb), lambda i, j: (0, 0)),
            pl.BlockSpec(place.shape, lambda i, j: (0, 0)),
        ],
        out_specs=pl.BlockSpec((N_CB, 1, tb, LANES), lambda i, j: (0, i, j, 0)),
        out_shape=jax.ShapeDtypeStruct((N_CB, b, s, LANES), BF16),
        scratch_shapes=[pltpu.VMEM((1, LANES), F32)],
        compiler_params=_params(2),
        name="forget_cumsum",
    )(logit3, b_pad, tri, place)


def _bias_tiles_kernel(rb_ref, o_ref):
    h = pl.program_id(0)
    j = pl.program_id(1)
    bk, bq = o_ref.shape[2], o_ref.shape[3]
    key = lax.broadcasted_iota(jnp.int32, (bk, bq), 0)
    query = lax.broadcasted_iota(jnp.int32, (bk, bq), 1)
    dist = query - key + j * bk
    n = jnp.maximum(dist, 0)
    max_exact = NUM_BUCKETS // 2
    nf = jnp.maximum(n, 1).astype(F32)
    large = max_exact + (jnp.log(nf / max_exact) / math.log(MAX_DISTANCE / max_exact)
                         * (NUM_BUCKETS - max_exact)).astype(jnp.int32)
    large = jnp.minimum(large, NUM_BUCKETS - 1)
    bucket = jnp.where(n < max_exact, n, large)
    far = rb_ref[NUM_BUCKETS - 1, h]
    bias = jnp.zeros((bk, bq), F32)
    for b in range(NUM_BUCKETS - 1):
        bias = jnp.where(bucket == b, rb_ref[b, h] - far, bias)
    o_ref[0, 0] = jnp.where(dist >= 0, bias, NEG)


def _bias_tiles(rel_bias, bk, bq):
    n_heads = rel_bias.shape[1]
    return pl.pallas_call(
        _bias_tiles_kernel,
        grid=(n_heads, 2),
        in_specs=[pl.BlockSpec(memory_space=pltpu.SMEM)],
        out_specs=pl.BlockSpec((1, 1, bk, bq), lambda h, j: (h, j, 0, 0)),
        out_shape=jax.ShapeDtypeStruct((n_heads, 2, bk, bq), F32),
        compiler_params=_params(2),
        name="bias_tiles",
    )(rel_bias)


def _stack_halves(q):
    lane = lax.broadcasted_iota(jnp.int32, q.shape, 1)
    zero = jnp.zeros_like(q)
    return jnp.concatenate([jnp.where(lane < HEAD_DIM, q, zero),
                            jnp.where(lane < HEAD_DIM, zero, q)], axis=0)


def _key_block(ref, h, kb, bk):
    return ref[h, 0, pl.ds(pl.multiple_of(kb * bk, bk), bk), :]


def _q_spec(bq, cb):
    return pl.BlockSpec((N_CB, 1, bq, LANES), lambda b, i: (cb // N_CB, b, i, 0))


def _kv_spec(s, cb):
    return pl.BlockSpec((N_CB, 1, s, LANES), lambda b, i: (cb // N_CB, b, 0, 0))


def _gate_spec(bq, group):
    return pl.BlockSpec((N_CB, 1, bq, LANES), lambda b, i: (CB_GATE // N_CB + group, b, i, 0))


def _out_spec(bq):
    return pl.BlockSpec((1, bq, GROUP_W), lambda b, i: (b, i, 0))


def _store_gated(o_ref, gate_ref, h, o):
    o_ref[0, :, h * LANES:(h + 1) * LANES] = (o * _silu(gate_ref[h, 0])).astype(o_ref.dtype)


def _with_ones_rows(vt):
    return jnp.concatenate([vt, jnp.ones((BF16_ROWS, vt.shape[1]), BF16)], axis=0)


def _value_t(vts, p):
    n = p.shape[1] // 2
    return jnp.concatenate([jnp.dot(vts[0], p[:, :n], preferred_element_type=F32),
                            jnp.dot(vts[1], p[:, n:], preferred_element_type=F32)], axis=1)


def _scores_t(q_st, k_ref, h, kb, bk):
    return lax.dot_general(_key_block(k_ref, h, kb, bk), q_st, NT_DIMS,
                           preferred_element_type=F32)


VALUE_LAG = 2


def _softmax_t_blocks(items, carries):
    ms = [None if carries is None else carries[h][0] for h in range(N_CB)]
    accs = [None if carries is None else carries[h][1] for h in range(N_CB)]
    pending = {}

    def score(i):
        h, score_fn, _ = items[i]
        s = score_fn()
        m = jnp.max(s, axis=0, keepdims=True)
        m_new = m if ms[h] is None else jnp.maximum(ms[h], m)
        pending[i] = (ms[h], m_new, jnp.exp(s - m_new).astype(BF16))
        ms[h] = m_new

    def value(i):
        h, _, vt_fn = items[i]
        m_old, m_new, p = pending.pop(i)
        pv = _value_t(vt_fn(), p)
        accs[h] = pv if accs[h] is None else jnp.exp(m_old - m_new) * accs[h] + pv

    for i in range(len(items) + VALUE_LAG):
        if i < len(items):
            score(i)
        if i >= VALUE_LAG:
            value(i - VALUE_LAG)
    return tuple((ms[h], accs[h]) for h in range(N_CB))


LOOP_BLOCKS = 4


def _pair_loop(n_blocks, block_of, step, carries, state_refs):
    def load():
        return tuple(tuple(ref[h] for ref in state_refs) for h in range(N_CB))

    def store(carries):
        for h, chain in enumerate(carries):
            for ref, value in zip(state_refs, chain):
                ref[h] = value

    def run(first, width):
        store(step(tuple(block_of(first + j) for j in range(width)), load()))

    store(carries)

    def trip(i, _):
        run(LOOP_BLOCKS * i, LOOP_BLOCKS)
        return 0

    lax.fori_loop(0, n_blocks // LOOP_BLOCKS, trip, 0)
    done = n_blocks - n_blocks % LOOP_BLOCKS

    @pl.when(n_blocks - done >= 2)
    def _():
        run(done, 2)

    @pl.when(n_blocks % 2 == 1)
    def _():
        run(n_blocks - 1, 1)

    return load()


def _normalized_t(acc):
    d = acc.shape[0] - BF16_ROWS
    return acc[:d] * (1.0 / acc[d:d + 1])


def _diff_kernel(lam_init, q_ref, k_ref, vt_ref, gate_ref, tile_ref, lamp_ref, hn_ref, o_ref,
                 m_ref, acc_ref):
    bq = q_ref.shape[2]
    qi = pl.program_id(1)
    qs = [_stack_halves(q_ref[h, 0] * (HEAD_DIM ** -0.5)) for h in range(N_CB)]

    def item(h, kb, tile_fn=None, width=1):
        rows = width * bq

        def score():
            k = k_ref[h, 0, pl.ds(pl.multiple_of(kb * bq, bq), rows), :]
            s = lax.dot_general(k, qs[h], NT_DIMS, preferred_element_type=F32)
            if tile_fn is None:
                return s
            tile = tile_fn(h)
            return s + jnp.concatenate([tile, tile], axis=1)

        def values():
            vt = _with_ones_rows(vt_ref[h, :, pl.ds(pl.multiple_of(kb * bq, bq), rows)])
            return vt, vt
        return h, score, values

    def step(kbs, carries):
        if len(kbs) % 2 == 0:
            items = [item(h, kb, None, 2) for kb in kbs[::2] for h in range(N_CB)]
        else:
            items = [item(h, kb) for kb in kbs for h in range(N_CB)]
        return _softmax_t_blocks(items, carries)

    near = ([item(h, qi, lambda h: tile_ref[h, 0]) for h in range(N_CB)]
            + [item(h, jnp.maximum(qi - 1, 0), lambda h: jnp.where(qi >= 1, tile_ref[h, 1], NEG))
               for h in range(N_CB)])
    carries = _softmax_t_blocks(near, None)
    carries = _pair_loop(jnp.maximum(qi - 1, 0), lambda i: i, step, carries, (m_ref, acc_ref))

    lp = lamp_ref[...]
    lam = (jnp.exp(jnp.sum(lp[0:1] * lp[1:2], axis=-1, keepdims=True))
           - jnp.exp(jnp.sum(lp[2:3] * lp[3:4], axis=-1, keepdims=True)) + lam_init)
    head_gain = jnp.broadcast_to(hn_ref[...] * (1.0 - lam_init), (LANES, bq))
    for h in range(N_CB):
        a = _normalized_t(carries[h][1])
        o = a[:, :bq] - lam * a[:, bq:]
        o = o * lax.rsqrt(jnp.mean(o * o, axis=0, keepdims=True) + EPS)
        _store_gated(o_ref, gate_ref, h, (o * head_gain).T)


def _diff_attention(proj4, vt, tiles, lam_params, head_norm, lam_init, bq):
    _, b, s, _ = proj4.shape
    return pl.pallas_call(
        functools.partial(_diff_kernel, lam_init),
        grid=(b, s // bq),
        in_specs=[
            _q_spec(bq, CB_AQ), _kv_spec(s, CB_AK),
            pl.BlockSpec((N_CB, LANES, s), lambda b, i: (VT_A, 0, b)),
            _gate_spec(bq, 0),
            pl.BlockSpec((N_CB, 2, bq, bq), lambda b, i: (0, 0, 0, 0)),
            pl.BlockSpec((4, HEAD_DIM), lambda b, i: (0, 0)),
            pl.BlockSpec((LANES, 1), lambda b, i: (0, 0)),
        ],
        out_specs=_out_spec(bq),
        out_shape=jax.ShapeDtypeStruct((b, s, GROUP_W), BF16),
        scratch_shapes=[pltpu.VMEM((N_CB, 1, 2 * bq), F32),
                        pltpu.VMEM((N_CB, LANES + BF16_ROWS, 2 * bq), F32)],
        compiler_params=_params(2),
        name="diff_attention",
    )(proj4, proj4, vt, proj4, tiles, lam_params, head_norm)


def _stick_kernel(q_ref, k_ref, vt_ref, gate_ref, o_ref, rest_ref, acc_ref):
    bq = q_ref.shape[2]
    qi = pl.program_id(1)
    qs = [_stack_halves(q_ref[h, 0] * (HEAD_DIM ** -0.5)) for h in range(N_CB)]
    key = lax.broadcasted_iota(jnp.int32, (bq, 2 * bq), 0)
    query = lax.broadcasted_iota(jnp.int32, (bq, 2 * bq), 1) & (bq - 1)
    strict = key < query

    row = lax.broadcasted_iota(jnp.int32, (bq + BF16_ROWS, bq), 0)
    col = lax.broadcasted_iota(jnp.int32, (bq + BF16_ROWS, bq), 1)
    after = jnp.where((col > row) | (row >= bq), 1.0, 0.0).astype(BF16)

    def step(kbs, carries, mask=None):
        items = [(h, kb) for kb in kbs for h in range(N_CB)]
        rests = [carries[h][0] for h in range(N_CB)]
        accs = [carries[h][1] for h in range(N_CB)]
        pending = {}

        def score(i):
            h, kb = items[i]
            z = _scores_t(qs[h], k_ref, h, kb, bq)
            sp = jnp.log(1.0 + jnp.exp2(jnp.abs(z) * -LOG2E))
            ls_pos = jnp.minimum(z, 0.0) - sp
            ls_neg = ls_pos - z
            if mask is not None:
                ls_neg = jnp.where(mask, ls_neg, 0.0)
            pending[i] = (ls_pos, ls_neg.astype(BF16))

        def cumsum(i):
            h, _ = items[i]
            ls_pos, log_rest = pending[i]
            t = jnp.dot(after, log_rest, preferred_element_type=F32)
            a = jnp.exp2((ls_pos + t[:bq]) * LOG2E)
            if mask is not None:
                a = jnp.where(mask, a, 0.0)
            pending[i] = (a.astype(BF16), jnp.exp2(rests[h] * LOG2E))
            rests[h] = rests[h] + t[bq:bq + 1]

        def value(i):
            h, kb = items[i]
            a, scale = pending.pop(i)
            vt = vt_ref[h, :, pl.ds(pl.multiple_of(kb * bq, bq), bq)]
            accs[h] = accs[h] + scale * _value_t((vt[:HEAD_DIM], vt[HEAD_DIM:]), a)

        for i in range(len(items) + 2):
            if i < len(items):
                score(i)
            if 1 <= i <= len(items):
                cumsum(i - 1)
            if i >= 2:
                value(i - 2)
        return tuple((rests[h], accs[h]) for h in range(N_CB))

    zero = (jnp.zeros((1, 2 * bq), F32), jnp.zeros((HEAD_DIM, 2 * bq), F32))
    carries = step((qi,), (zero,) * N_CB, strict)
    carries = _pair_loop(qi, lambda i: qi - 1 - i, step, carries, (rest_ref, acc_ref))
    for h in range(N_CB):
        acc = carries[h][1]
        _store_gated(o_ref, gate_ref, h, jnp.concatenate([acc[:, :bq], acc[:, bq:]], axis=0).T)


def _stick_attention(proj4, vt, bq):
    _, b, s, _ = proj4.shape
    return pl.pallas_call(
        _stick_kernel,
        grid=(b, s // bq),
        in_specs=[
            _q_spec(bq, CB_SQ), _kv_spec(s, CB_SK),
            pl.BlockSpec((N_CB, LANES, s), lambda b, i: (VT_S, 0, b)),
            _gate_spec(bq, 1),
        ],
        out_specs=_out_spec(bq),
        out_shape=jax.ShapeDtypeStruct((b, s, GROUP_W), BF16),
        scratch_shapes=[pltpu.VMEM((N_CB, 1, 2 * bq), F32),
                        pltpu.VMEM((N_CB, HEAD_DIM, 2 * bq), F32)],
        compiler_params=_params(2),
        name="stick_attention",
    )(proj4, proj4, vt, proj4)


def _fox_kernel(q_ref, k_ref, vt_ref, gate_ref, negc_ref, o_ref, m_ref, acc_ref):
    bq = q_ref.shape[2]
    qi = pl.program_id(1)
    key = lax.broadcasted_iota(jnp.int32, (bq, 2 * bq), 0)
    query = lax.broadcasted_iota(jnp.int32, (bq, 2 * bq), 1) & (bq - 1)
    causal = key <= query

    lane = lax.broadcasted_iota(jnp.int32, (1, LANES), 1)
    low = lane < HEAD_DIM
    ones_low = jnp.where(lane < 3, 1.0, 0.0).astype(BF16)
    ones_high = jnp.where(low | (lane >= HEAD_DIM + 3), 0.0, 1.0).astype(BF16)
    qs = []
    for h in range(N_CB):
        q = q_ref[h, 0] * (HEAD_DIM ** -0.5)
        qs.append((jnp.where(low, q, ones_high), jnp.where(low, ones_low, q)))

    def item(h, kb, mask, width=1):
        rows = width * bq

        def score():
            start = pl.multiple_of(kb * bq, bq)
            k = k_ref[h, 0, pl.ds(start, rows), :]
            c = negc_ref[h, 0, pl.ds(start, rows), :]
            s = jnp.concatenate(
                [lax.dot_general(jnp.where(low, k, c), qs[h][0], NT_DIMS,
                                 preferred_element_type=F32),
                 lax.dot_general(jnp.where(low, c, k), qs[h][1], NT_DIMS,
                                 preferred_element_type=F32)], axis=1)
            return s if mask is None else jnp.where(mask, s, NEG)

        def values():
            vt = vt_ref[h, :, pl.ds(pl.multiple_of(kb * bq, bq), rows)]
            return _with_ones_rows(vt[:HEAD_DIM]), _with_ones_rows(vt[HEAD_DIM:])
        return h, score, values

    def step(kbs, carries, mask=None):
        if len(kbs) % 2 == 0:
            items = [item(h, kb, mask, 2) for kb in kbs[::2] for h in range(N_CB)]
        else:
            items = [item(h, kb, mask) for kb in kbs for h in range(N_CB)]
        return _softmax_t_blocks(items, carries)

    carries = _pair_loop(qi, lambda i: i, step, step((qi,), None, causal), (m_ref, acc_ref))
    for h in range(N_CB):
        a = _normalized_t(carries[h][1])
        _store_gated(o_ref, gate_ref, h, jnp.concatenate([a[:, :bq], a[:, bq:]], axis=0).T)


def _fox_attention(proj4, vt, negc, bq):
    _, b, s, _ = proj4.shape
    return pl.pallas_call(
        _fox_kernel,
        grid=(b, s // bq),
        in_specs=[
            _q_spec(bq, CB_FQ), _kv_spec(s, CB_FK),
            pl.BlockSpec((N_CB, LANES, s), lambda b, i: (VT_F, 0, b)),
            _gate_spec(bq, 2),
            pl.BlockSpec((N_CB, 1, s, LANES), lambda b, i: (0, b, 0, 0)),
        ],
        out_specs=_out_spec(bq),
        out_shape=jax.ShapeDtypeStruct((b, s, GROUP_W), BF16),
        scratch_shapes=[pltpu.VMEM((N_CB, 1, 2 * bq), F32),
                        pltpu.VMEM((N_CB, HEAD_DIM + BF16_ROWS, 2 * bq), F32)],
        compiler_params=_params(2),
        name="fox_attention",
    )(proj4, proj4, vt, proj4, negc)


def _mem_kernel(q_ref, km_ref, vmt_ref, gate_ref, o_ref):
    scale = LANES ** -0.5
    scores = [lax.dot_general(km_ref[h, 0], q_ref[h, 0], NT_DIMS,
                              preferred_element_type=F32) * scale for h in range(N_CB)]
    probs = [jnp.exp(s - jnp.max(s, axis=0, keepdims=True)).astype(BF16) for s in scores]
    for h in range(N_CB):
        acc = jnp.dot(_with_ones_rows(vmt_ref[h]), probs[h], preferred_element_type=F32)
        _store_gated(o_ref, gate_ref, h, _normalized_t(acc).T)


def _mem_attention(proj4, km4, vmt, bq):
    _, b, s, _ = proj4.shape
    mlen = km4.shape[2]
    return pl.pallas_call(
        _mem_kernel,
        grid=(b, s // bq),
        in_specs=[
            _q_spec(bq, CB_MQ),
            pl.BlockSpec((N_CB, 1, mlen, LANES), lambda b, i: (0, b, 0, 0)),
            pl.BlockSpec((N_CB, LANES, mlen), lambda b, i: (0, 0, b)),
            _gate_spec(bq, 3),
        ],
        out_specs=_out_spec(bq),
        out_shape=jax.ShapeDtypeStruct((b, s, GROUP_W), BF16),
        compiler_params=_params(2),
        name="mem_attention",
    )(proj4, km4, vmt, proj4)


def _out_proj_kernel(ya_ref, yb_ref, yc_ref, ym_ref, w_ref, g_ref, x_ref, o_ref):
    y = jnp.concatenate([ya_ref[...], yb_ref[...], yc_ref[...], ym_ref[...]], axis=-1)
    z = jnp.dot(y, w_ref[...], preferred_element_type=F32)
    z = z * lax.rsqrt(jnp.mean(z * z, axis=-1, keepdims=True) + EPS)
    o_ref[...] = x_ref[...] + z * g_ref[...]


def _out_proj(ys, w, g, x2d, tm):
    m, d = x2d.shape
    y_spec = pl.BlockSpec((tm, GROUP_W), lambda i: (i, 0))
    return pl.pallas_call(
        _out_proj_kernel,
        grid=(m // tm,),
        in_specs=[y_spec] * 4 + [
            pl.BlockSpec(w.shape, lambda i: (0, 0)),
            pl.BlockSpec((1, d), lambda i: (0, 0)),
            pl.BlockSpec((tm, d), lambda i: (i, 0)),
        ],
        out_specs=pl.BlockSpec((tm, d), lambda i: (i, 0)),
        out_shape=jax.ShapeDtypeStruct((m, d), F32),
        compiler_params=_params(1),
        name="out_proj",
    )(*ys, w, g, x2d)


def _pick(n, cands):
    for c in cands:
        if n % c == 0:
            return c
    raise ValueError(f"no tile in {cands} divides {n}")


def kernel(x, mem, w_in, b_forget, w_mem_kv, w_out, pre_norm, post_norm, mem_norm,
           diff_lambda, diff_head_norm, rel_bias):
    b, s, d = x.shape
    mlen = mem.shape[1]
    depth = w_in.shape[0]
    m = b * s
    bq = _pick(s, (256, 128))
    tm = _pick(m, (1024, 512, 256))

    gw = GROUP_W
    logit_lo, logit_hi = 9 * gw, 9 * gw + N_FOX_HEADS
    tiles = _bias_tiles(rel_bias, bq, bq)
    mem2d = mem.reshape(b * mlen, d)
    x2d = x.reshape(m, d)

    for l in range(depth):
        wl = w_in[l].astype(BF16)
        w_main = jnp.concatenate([wl[:, 0:2 * gw], wl[:, 3 * gw:5 * gw], wl[:, 6 * gw:8 * gw],
                                  wl[:, logit_hi:]], axis=1)
        w_vt = jnp.concatenate([wl[:, 2 * gw:3 * gw], wl[:, 5 * gw:6 * gw], wl[:, 8 * gw:9 * gw]],
                               axis=1).T
        w_logit = jnp.pad(wl[:, logit_lo:logit_hi], ((0, 0), (0, LANES - N_FOX_HEADS)))
        b_pad = jnp.pad(b_forget[l], (0, LANES - N_FOX_HEADS)).reshape(1, LANES)

        tn = _pick(w_main.shape[1], (1408, 512))
        proj, logit, vt = _in_proj(x2d, pre_norm[l].reshape(1, d), w_main, w_logit, w_vt, tm, tn)
        proj4 = proj.reshape(N_PROJ_CB, b, s, LANES)
        negc = _forget_cumsum(logit.reshape(b, s, LANES), b_pad, _pick(s, (512, 256, 128)))

        w_kv = w_mem_kv[l].astype(BF16)
        km, vmt = _mem_kv(mem2d, mem_norm[l].reshape(1, d), w_kv[:, :gw], w_kv[:, gw:].T,
                          _pick(b * mlen, (1024, 512, 256)))
        km4 = km.reshape(N_CB, b, mlen, LANES)

        lam_init = 0.8 - 0.6 * math.exp(-0.3 * l)
        y_a = _diff_attention(proj4, vt, tiles, diff_lambda[l],
                              diff_head_norm[l].reshape(LANES, 1), lam_init, bq)
        y_b = _stick_attention(proj4, vt, bq)
        y_c = _fox_attention(proj4, vt, negc, bq)
        y_m = _mem_attention(proj4, km4, vmt, _pick(s, (1024, 512, 256, 128)))
        ys = [y.reshape(m, GROUP_W) for y in (y_a, y_b, y_c, y_m)]
        x2d = _out_proj(ys, w_out[l].astype(BF16), post_norm[l].reshape(1, d), x2d,
                        _pick(m, (1024, 512, 256)))
    return x2d.reshape(b, s, d)
```

```python
import functools
import math

import jax
import jax.numpy as jnp
import numpy as np
from jax import lax
from jax.experimental import pallas as pl
from jax.experimental.pallas import tpu as pltpu

F32 = jnp.float32
BF16 = jnp.bfloat16

LANES = 128
BF16_ROWS = 16
HEAD_DIM = 64
GROUP_W = 512
N_CB = GROUP_W // LANES
N_FOX_HEADS = GROUP_W // HEAD_DIM
NUM_BUCKETS = 32
MAX_DISTANCE = 128
EPS = 1e-6
NEG = -1e30
LOG2E = 1.4426950408889634
VMEM_LIMIT = 56 * 1024 * 1024

CB_AQ, CB_AK = 0, 4
CB_SQ, CB_SK = 8, 12
CB_FQ, CB_FK = 16, 20
CB_MQ, CB_GATE = 24, 28
N_PROJ_CB = 44
IN_PROJ_CHUNKS = 4
VT_A, VT_S, VT_F = 0, 1, 2

NT_DIMS = (((1,), (1,)), ((), ()))


def _params(n_axes):
    return pltpu.CompilerParams(dimension_semantics=("arbitrary",) * n_axes,
                                vmem_limit_bytes=VMEM_LIMIT)


def _split3(x):
    hi = x.astype(BF16)
    r = x - hi.astype(F32)
    mid = r.astype(BF16)
    lo = (r - mid.astype(F32)).astype(BF16)
    return hi, mid, lo


def _silu(g):
    g = g.astype(F32)
    return g / (1.0 + jnp.exp(-g))


def _rms_rows(x, g_ref):
    y = x * lax.rsqrt(jnp.mean(x * x, axis=-1, keepdims=True) + EPS)
    return (y * g_ref[...]).astype(BF16)


def _store_col_blocks(o_ref, res):
    for c in range(o_ref.shape[0]):
        o_ref[c] = res[:, c * LANES:(c + 1) * LANES]


def _in_proj_kernel(n_chunks, x0_ref, xn_ref, g_ref, w_ref, wl_ref, wvt_ref, o_ref, ol_ref,
                    ovt_ref, h_ref):
    i = pl.program_id(0)

    @pl.when(i == 0)
    def _():
        h_ref[0] = _rms_rows(x0_ref[...], g_ref)

    slot = lax.rem(i, 2)
    h_ref[1 - slot] = _rms_rows(xn_ref[...], g_ref)
    h = h_ref[slot]
    ol_ref[...] = jnp.dot(h, wl_ref[...], preferred_element_type=F32)
    for c in range(ovt_ref.shape[0] // N_CB):
        rows = slice(c * GROUP_W, (c + 1) * GROUP_W)
        vt = lax.dot_general(wvt_ref[rows, :], h, NT_DIMS, preferred_element_type=F32).astype(BF16)
        for r in range(N_CB):
            ovt_ref[c * N_CB + r] = vt[r * LANES:(r + 1) * LANES, :]
    cb = o_ref.shape[0] // n_chunks
    for c in range(n_chunks):
        res = jnp.dot(h, w_ref[:, c * cb * LANES:(c + 1) * cb * LANES],
                      preferred_element_type=F32).astype(BF16)
        for r in range(cb):
            o_ref[c * cb + r] = res[:, r * LANES:(r + 1) * LANES]


def _in_proj(x2d, g, w, wl, wvt, tm, n_chunks):
    m, k = x2d.shape
    n = w.shape[1]
    nv = wvt.shape[0]
    last = m // tm - 1
    resident = pl.Buffered(1)
    return pl.pallas_call(
        functools.partial(_in_proj_kernel, n_chunks),
        grid=(m // tm,),
        in_specs=[
            pl.BlockSpec((tm, k), lambda i: (0, 0), pipeline_mode=resident),
            pl.BlockSpec((tm, k), lambda i: (jnp.minimum(i + 1, last), 0)),
            pl.BlockSpec((1, k), lambda i: (0, 0)),
            pl.BlockSpec((k, n), lambda i: (0, 0), pipeline_mode=resident),
            pl.BlockSpec((k, LANES), lambda i: (0, 0)),
            pl.BlockSpec((nv, k), lambda i: (0, 0), pipeline_mode=resident),
        ],
        out_specs=[
            pl.BlockSpec((n // LANES, tm, LANES), lambda i: (0, i, 0)),
            pl.BlockSpec((tm, LANES), lambda i: (i, 0)),
            pl.BlockSpec((nv // LANES, LANES, tm), lambda i: (0, 0, i)),
        ],
        out_shape=[
            jax.ShapeDtypeStruct((n // LANES, m, LANES), BF16),
            jax.ShapeDtypeStruct((m, LANES), F32),
            jax.ShapeDtypeStruct((nv // LANES, LANES, m), BF16),
        ],
        scratch_shapes=[pltpu.VMEM((2, tm, k), BF16)],
        compiler_params=_params(1),
        name="in_proj",
    )(x2d, x2d, g, w, wl, wvt)


def _mem_kv_kernel(x_ref, g_ref, wk_ref, wvt_ref, ok_ref, ovt_ref):
    h = _rms_rows(x_ref[...], g_ref)
    _store_col_blocks(ok_ref, jnp.dot(h, wk_ref[...], preferred_element_type=F32).astype(BF16))
    vt = lax.dot_general(wvt_ref[...], h, NT_DIMS, preferred_element_type=F32).astype(BF16)
    for c in range(ovt_ref.shape[0]):
        ovt_ref[c] = vt[c * LANES:(c + 1) * LANES, :]


def _mem_kv(x2d, g, wk, wvt, tm):
    m, k = x2d.shape
    n = wk.shape[1]
    return pl.pallas_call(
        _mem_kv_kernel,
        grid=(m // tm,),
        in_specs=[
            pl.BlockSpec((tm, k), lambda i: (i, 0)),
            pl.BlockSpec((1, k), lambda i: (0, 0)),
            pl.BlockSpec((k, n), lambda i: (0, 0)),
            pl.BlockSpec((n, k), lambda i: (0, 0)),
        ],
        out_specs=[
            pl.BlockSpec((n // LANES, tm, LANES), lambda i: (0, i, 0)),
            pl.BlockSpec((n // LANES, LANES, tm), lambda i: (0, 0, i)),
        ],
        out_shape=[
            jax.ShapeDtypeStruct((n // LANES, m, LANES), BF16),
            jax.ShapeDtypeStruct((n // LANES, LANES, m), BF16),
        ],
        compiler_params=_params(1),
        name="mem_kv",
    )(x2d, g, wk, wvt)


def _forget_cumsum_kernel(logit_ref, b_ref, tri_ref, place_ref, o_ref, carry_ref):
    @pl.when(pl.program_id(1) == 0)
    def _():
        carry_ref[...] = jnp.zeros_like(carry_ref)

    tb = logit_ref.shape[1]
    z = logit_ref[0] + b_ref[...]
    log_f = jnp.minimum(z, 0.0) - jnp.log1p(jnp.exp(-jnp.abs(z)))
    sums = jnp.dot(tri_ref[...], jnp.concatenate(_split3(log_f), axis=1),
                   preferred_element_type=F32)
    c = carry_ref[...] + (sums[:, :LANES] + sums[:, LANES:2 * LANES] + sums[:, 2 * LANES:])
    carry_ref[...] = c[tb - 1:tb, :]
    ext = jnp.dot(jnp.concatenate(_split3(-c), axis=1), place_ref[...],
                  preferred_element_type=F32)
    for h in range(N_CB):
        o_ref[h, 0] = ext[:, h * LANES:(h + 1) * LANES].astype(BF16)


def _placement():
    place = np.zeros((3 * LANES, N_CB * LANES), np.float32)
    for term in range(3):
        for h in range(N_CB):
            place[term * LANES + 2 * h, h * LANES + HEAD_DIM + term] = 1.0
            place[term * LANES + 2 * h + 1, h * LANES + term] = 1.0
    return jnp.asarray(place, BF16)


def _forget_cumsum(logit3, b_pad, tb):
    b, s, _ = logit3.shape
    tri = jnp.asarray(np.tril(np.ones((tb, tb), np.float32)), BF16)
    place = _placement()
    return pl.pallas_call(
        _forget_cumsum_kernel,
        grid=(b, s // tb),
        in_specs=[
            pl.BlockSpec((1, tb, LANES), lambda i, j: (i, j, 0)),
            pl.BlockSpec((1, LANES), lambda i, j: (0, 0)),
            pl.BlockSpec((tb, tb), lambda i, j: (0, 0)),
            pl.BlockSpec(place.shape, lambda i, j: (0, 0)),
        ],
        out_specs=pl.BlockSpec((N_CB, 1, tb, LANES), lambda i, j: (0, i, j, 0)),
        out_shape=jax.ShapeDtypeStruct((N_CB, b, s, LANES), BF16),
        scratch_shapes=[pltpu.VMEM((1, LANES), F32)],
        compiler_params=_params(2),
        name="forget_cumsum",
    )(logit3, b_pad, tri, place)


def _bias_tiles_kernel(rb_ref, o_ref):
    h = pl.program_id(0)
    j = pl.program_id(1)
    bk, bq = o_ref.shape[2], o_ref.shape[3]
    key = lax.broadcasted_iota(jnp.int32, (bk, bq), 0)
    query = lax.broadcasted_iota(jnp.int32, (bk, bq), 1)
    dist = query - key + j * bq
    n = jnp.maximum(dist, 0)
    max_exact = NUM_BUCKETS // 2
    nf = jnp.maximum(n, 1).astype(F32)
    large = max_exact + (jnp.log(nf / max_exact) / math.log(MAX_DISTANCE / max_exact)
                         * (NUM_BUCKETS - max_exact)).astype(jnp.int32)
    large = jnp.minimum(large, NUM_BUCKETS - 1)
    bucket = jnp.where(n < max_exact, n, large)
    far = rb_ref[NUM_BUCKETS - 1, h]
    bias = jnp.zeros((bk, bq), F32)
    for b in range(NUM_BUCKETS - 1):
        bias = jnp.where(bucket == b, rb_ref[b, h] - far, bias)
    o_ref[0, 0] = jnp.where(dist >= 0, bias, NEG)


def _bias_tiles(rel_bias, bk, bq):
    n_heads = rel_bias.shape[1]
    return pl.pallas_call(
        _bias_tiles_kernel,
        grid=(n_heads, 2),
        in_specs=[pl.BlockSpec(memory_space=pltpu.SMEM)],
        out_specs=pl.BlockSpec((1, 1, bk, bq), lambda h, j: (h, j, 0, 0)),
        out_shape=jax.ShapeDtypeStruct((n_heads, 2, bk, bq), F32),
        compiler_params=_params(2),
        name="bias_tiles",
    )(rel_bias)


def _stack_halves(q):
    lane = lax.broadcasted_iota(jnp.int32, q.shape, 1)
    zero = jnp.zeros_like(q)
    return jnp.concatenate([jnp.where(lane < HEAD_DIM, q, zero),
                            jnp.where(lane < HEAD_DIM, zero, q)], axis=0)


def _key_block(ref, h, kb, bk):
    return ref[h, 0, pl.ds(pl.multiple_of(kb * bk, bk), bk), :]


def _q_spec(bq, cb):
    return pl.BlockSpec((N_CB, 1, bq, LANES), lambda b, i: (cb // N_CB, b, i, 0))


def _kv_spec(s, cb):
    return pl.BlockSpec((N_CB, 1, s, LANES), lambda b, i: (cb // N_CB, b, 0, 0))


def _gate_spec(bq, group):
    return pl.BlockSpec((N_CB, 1, bq, LANES), lambda b, i: (CB_GATE // N_CB + group, b, i, 0))


def _out_spec(bq):
    return pl.BlockSpec((1, bq, GROUP_W), lambda b, i: (b, i, 0))


def _store_gated(o_ref, gate_ref, h, o):
    o_ref[0, :, h * LANES:(h + 1) * LANES] = (o * _silu(gate_ref[h, 0])).astype(o_ref.dtype)


def _with_ones_rows(vt):
    return jnp.concatenate([vt, jnp.ones((BF16_ROWS, vt.shape[1]), BF16)], axis=0)


def _value_t(vts, p):
    n = p.shape[1] // 2
    return jnp.concatenate([jnp.dot(vts[0], p[:, :n], preferred_element_type=F32),
                            jnp.dot(vts[1], p[:, n:], preferred_element_type=F32)], axis=1)


def _scores_t(q_st, k_ref, h, kb, bk):
    return lax.dot_general(_key_block(k_ref, h, kb, bk), q_st, NT_DIMS,
                           preferred_element_type=F32)


VALUE_LAG = 2


def _softmax_t_blocks(items, carries):
    ms = [None if carries is None else carries[h][0] for h in range(N_CB)]
    accs = [None if carries is None else carries[h][1] for h in range(N_CB)]
    pending = {}

    def score(i):
        h, score_fn, _ = items[i]
        s = score_fn()
        m = jnp.max(s, axis=0, keepdims=True)
        m_new = m if ms[h] is None else jnp.maximum(ms[h], m)
        pending[i] = (ms[h], m_new, jnp.exp(s - m_new).astype(BF16))
        ms[h] = m_new

    def value(i):
        h, _, vt_fn = items[i]
        m_old, m_new, p = pending.pop(i)
        pv = _value_t(vt_fn(), p)
        accs[h] = pv if accs[h] is None else jnp.exp(m_old - m_new) * accs[h] + pv

    for i in range(len(items) + VALUE_LAG):
        if i < len(items):
            score(i)
        if i >= VALUE_LAG:
            value(i - VALUE_LAG)
    return tuple((ms[h], accs[h]) for h in range(N_CB))


LOOP_BLOCKS = 4


def _pair_loop(n_blocks, block_of, step, carries, state_refs):
    def load():
        return tuple(tuple(ref[h] for ref in state_refs) for h in range(N_CB))

    def store(carries):
        for h, chain in enumerate(carries):
            for ref, value in zip(state_refs, chain):
                ref[h] = value

    def run(first, width):
        store(step(tuple(block_of(first + j) for j in range(width)), load()))

    store(carries)

    def trip(i, _):
        run(LOOP_BLOCKS * i, LOOP_BLOCKS)
        return 0

    lax.fori_loop(0, n_blocks // LOOP_BLOCKS, trip, 0)
    done = n_blocks - n_blocks % LOOP_BLOCKS

    @pl.when(n_blocks - done >= 2)
    def _():
        run(done, 2)

    @pl.when(n_blocks % 2 == 1)
    def _():
        run(n_blocks - 1, 1)

    return load()


def _normalized_t(acc):
    d = acc.shape[0] - BF16_ROWS
    return acc[:d] * (1.0 / acc[d:d + 1])


def _diff_kernel(lam_init, q_ref, k_ref, vt_ref, gate_ref, tile_ref, lamp_ref, hn_ref, o_ref,
                 m_ref, acc_ref):
    bq = q_ref.shape[2]
    qi = pl.program_id(1)
    qs = [_stack_halves(q_ref[h, 0] * (HEAD_DIM ** -0.5)) for h in range(N_CB)]

    def item(h, kb, tile_fn=None, width=1):
        rows = width * bq

        def score():
            k = k_ref[h, 0, pl.ds(pl.multiple_of(kb * bq, bq), rows), :]
            s = lax.dot_general(k, qs[h], NT_DIMS, preferred_element_type=F32)
            if tile_fn is None:
                return s
            tile = tile_fn(h)
            return s + jnp.concatenate([tile, tile], axis=1)

        def values():
            vt = _with_ones_rows(vt_ref[h, :, pl.ds(pl.multiple_of(kb * bq, bq), rows)])
            return vt, vt
        return h, score, values

    def step(kbs, carries):
        if len(kbs) % 2 == 0:
            items = [item(h, kb, None, 2) for kb in kbs[::2] for h in range(N_CB)]
        else:
            items = [item(h, kb) for kb in kbs for h in range(N_CB)]
        return _softmax_t_blocks(items, carries)

    near = [item(h, jnp.maximum(qi - 1, 0), lambda h: tile_ref[h, jnp.minimum(qi, 1)], 2)
            for h in range(N_CB)]
    carries = _softmax_t_blocks(near, None)
    carries = _pair_loop(jnp.maximum(qi - 1, 0), lambda i: i, step, carries, (m_ref, acc_ref))

    lp = lamp_ref[...]
    lam = (jnp.exp(jnp.sum(lp[0:1] * lp[1:2], axis=-1, keepdims=True))
           - jnp.exp(jnp.sum(lp[2:3] * lp[3:4], axis=-1, keepdims=True)) + lam_init)
    head_gain = jnp.broadcast_to(hn_ref[...] * (1.0 - lam_init), (LANES, bq))
    for h in range(N_CB):
        a = _normalized_t(carries[h][1])
        o = a[:, :bq] - lam * a[:, bq:]
        o = o * lax.rsqrt(jnp.mean(o * o, axis=0, keepdims=True) + EPS)
        _store_gated(o_ref, gate_ref, h, (o * head_gain).T)


def _diff_attention(proj4, vt, tiles, lam_params, head_norm, lam_init, bq):
    _, b, s, _ = proj4.shape
    return pl.pallas_call(
        functools.partial(_diff_kernel, lam_init),
        grid=(b, s // bq),
        in_specs=[
            _q_spec(bq, CB_AQ), _kv_spec(s, CB_AK),
            pl.BlockSpec((N_CB, LANES, s), lambda b, i: (VT_A, 0, b)),
            _gate_spec(bq, 0),
            pl.BlockSpec((N_CB, 2, 2 * bq, bq), lambda b, i: (0, 0, 0, 0)),
            pl.BlockSpec((4, HEAD_DIM), lambda b, i: (0, 0)),
            pl.BlockSpec((LANES, 1), lambda b, i: (0, 0)),
        ],
        out_specs=_out_spec(bq),
        out_shape=jax.ShapeDtypeStruct((b, s, GROUP_W), BF16),
        scratch_shapes=[pltpu.VMEM((N_CB, 1, 2 * bq), F32),
                        pltpu.VMEM((N_CB, LANES + BF16_ROWS, 2 * bq), F32)],
        compiler_params=_params(2),
        name="diff_attention",
    )(proj4, proj4, vt, proj4, tiles, lam_params, head_norm)


def _stick_kernel(q_ref, k_ref, vt_ref, gate_ref, o_ref, rest_ref, acc_ref):
    bq = q_ref.shape[2]
    qi = pl.program_id(1)
    qs = [_stack_halves(q_ref[h, 0] * (HEAD_DIM ** -0.5)) for h in range(N_CB)]
    key = lax.broadcasted_iota(jnp.int32, (bq, 2 * bq), 0)
    query = lax.broadcasted_iota(jnp.int32, (bq, 2 * bq), 1) & (bq - 1)
    strict = key < query

    row = lax.broadcasted_iota(jnp.int32, (bq + BF16_ROWS, bq), 0)
    col = lax.broadcasted_iota(jnp.int32, (bq + BF16_ROWS, bq), 1)
    after = jnp.where((col > row) | (row >= bq), 1.0, 0.0).astype(BF16)

    def step(kbs, carries, mask=None):
        items = [(h, kb) for kb in kbs for h in range(N_CB)]
        rests = [carries[h][0] for h in range(N_CB)]
        accs = [carries[h][1] for h in range(N_CB)]
        pending = {}

        def score(i):
            h, kb = items[i]
            z = _scores_t(qs[h], k_ref, h, kb, bq)
            sp = jnp.log(1.0 + jnp.exp2(jnp.abs(z) * -LOG2E))
            ls_pos = jnp.minimum(z, 0.0) - sp
            ls_neg = ls_pos - z
            if mask is not None:
                ls_neg = jnp.where(mask, ls_neg, 0.0)
            pending[i] = (ls_pos, ls_neg.astype(BF16))

        def cumsum(i):
            h, _ = items[i]
            ls_pos, log_rest = pending[i]
            t = jnp.dot(after, log_rest, preferred_element_type=F32)
            a = jnp.exp2((ls_pos + t[:bq]) * LOG2E)
            if mask is not None:
                a = jnp.where(mask, a, 0.0)
            pending[i] = (a.astype(BF16), jnp.exp2(rests[h] * LOG2E))
            rests[h] = rests[h] + t[bq:bq + 1]

        def value(i):
            h, kb = items[i]
            a, scale = pending.pop(i)
            vt = vt_ref[h, :, pl.ds(pl.multiple_of(kb * bq, bq), bq)]
            accs[h] = accs[h] + scale * _value_t((vt[:HEAD_DIM], vt[HEAD_DIM:]), a)

        for i in range(len(items) + 2):
            if i < len(items):
                score(i)
            if 1 <= i <= len(items):
                cumsum(i - 1)
            if i >= 2:
                value(i - 2)
        return tuple((rests[h], accs[h]) for h in range(N_CB))

    zero = (jnp.zeros((1, 2 * bq), F32), jnp.zeros((HEAD_DIM, 2 * bq), F32))
    carries = step((qi,), (zero,) * N_CB, strict)
    carries = _pair_loop(qi, lambda i: qi - 1 - i, step, carries, (rest_ref, acc_ref))
    for h in range(N_CB):
        acc = carries[h][1]
        _store_gated(o_ref, gate_ref, h, jnp.concatenate([acc[:, :bq], acc[:, bq:]], axis=0).T)


def _stick_attention(proj4, vt, bq):
    _, b, s, _ = proj4.shape
    return pl.pallas_call(
        _stick_kernel,
        grid=(b, s // bq),
        in_specs=[
            _q_spec(bq, CB_SQ), _kv_spec(s, CB_SK),
            pl.BlockSpec((N_CB, LANES, s), lambda b, i: (VT_S, 0, b)),
            _gate_spec(bq, 1),
        ],
        out_specs=_out_spec(bq),
        out_shape=jax.ShapeDtypeStruct((b, s, GROUP_W), BF16),
        scratch_shapes=[pltpu.VMEM((N_CB, 1, 2 * bq), F32),
                        pltpu.VMEM((N_CB, HEAD_DIM, 2 * bq), F32)],
        compiler_params=_params(2),
        name="stick_attention",
    )(proj4, proj4, vt, proj4)


def _fox_kernel(q_ref, k_ref, vt_ref, gate_ref, negc_ref, o_ref, m_ref, acc_ref):
    bq = q_ref.shape[2]
    qi = pl.program_id(1)
    key = lax.broadcasted_iota(jnp.int32, (bq, 2 * bq), 0)
    query = lax.broadcasted_iota(jnp.int32, (bq, 2 * bq), 1) & (bq - 1)
    causal = key <= query

    lane = lax.broadcasted_iota(jnp.int32, (1, LANES), 1)
    low = lane < HEAD_DIM
    ones_low = jnp.where(lane < 3, 1.0, 0.0).astype(BF16)
    ones_high = jnp.where(low | (lane >= HEAD_DIM + 3), 0.0, 1.0).astype(BF16)
    qs = []
    for h in range(N_CB):
        q = q_ref[h, 0] * (HEAD_DIM ** -0.5)
        qs.append((jnp.where(low, q, ones_high), jnp.where(low, ones_low, q)))

    def item(h, kb, mask, width=1):
        rows = width * bq

        def score():
            start = pl.multiple_of(kb * bq, bq)
            k = k_ref[h, 0, pl.ds(start, rows), :]
            c = negc_ref[h, 0, pl.ds(start, rows), :]
            s = jnp.concatenate(
                [lax.dot_general(jnp.where(low, k, c), qs[h][0], NT_DIMS,
                                 preferred_element_type=F32),
                 lax.dot_general(jnp.where(low, c, k), qs[h][1], NT_DIMS,
                                 preferred_element_type=F32)], axis=1)
            return s if mask is None else jnp.where(mask, s, NEG)

        def values():
            vt = vt_ref[h, :, pl.ds(pl.multiple_of(kb * bq, bq), rows)]
            return _with_ones_rows(vt[:HEAD_DIM]), _with_ones_rows(vt[HEAD_DIM:])
        return h, score, values

    def step(kbs, carries, mask=None):
        if len(kbs) % 2 == 0:
            items = [item(h, kb, mask, 2) for kb in kbs[::2] for h in range(N_CB)]
        else:
            items = [item(h, kb, mask) for kb in kbs for h in range(N_CB)]
        return _softmax_t_blocks(items, carries)

    carries = _pair_loop(qi, lambda i: i, step, step((qi,), None, causal), (m_ref, acc_ref))
    for h in range(N_CB):
        a = _normalized_t(carries[h][1])
        _store_gated(o_ref, gate_ref, h, jnp.concatenate([a[:, :bq], a[:, bq:]], axis=0).T)


def _fox_attention(proj4, vt, negc, bq):
    _, b, s, _ = proj4.shape
    return pl.pallas_call(
        _fox_kernel,
        grid=(b, s // bq),
        in_specs=[
            _q_spec(bq, CB_FQ), _kv_spec(s, CB_FK),
            pl.BlockSpec((N_CB, LANES, s), lambda b, i: (VT_F, 0, b)),
            _gate_spec(bq, 2),
            pl.BlockSpec((N_CB, 1, s, LANES), lambda b, i: (0, b, 0, 0)),
        ],
        out_specs=_out_spec(bq),
        out_shape=jax.ShapeDtypeStruct((b, s, GROUP_W), BF16),
        scratch_shapes=[pltpu.VMEM((N_CB, 1, 2 * bq), F32),
                        pltpu.VMEM((N_CB, HEAD_DIM + BF16_ROWS, 2 * bq), F32)],
        compiler_params=_params(2),
        name="fox_attention",
    )(proj4, proj4, vt, proj4, negc)


def _mem_kernel(q_ref, km_ref, vmt_ref, gate_ref, o_ref):
    scale = LANES ** -0.5
    scores = [lax.dot_general(km_ref[h, 0], q_ref[h, 0], NT_DIMS,
                              preferred_element_type=F32) * scale for h in range(N_CB)]
    probs = [jnp.exp(s - jnp.max(s, axis=0, keepdims=True)).astype(BF16) for s in scores]
    for h in range(N_CB):
        acc = jnp.dot(_with_ones_rows(vmt_ref[h]), probs[h], preferred_element_type=F32)
        _store_gated(o_ref, gate_ref, h, _normalized_t(acc).T)


def _mem_attention(proj4, km4, vmt, bq):
    _, b, s, _ = proj4.shape
    mlen = km4.shape[2]
    return pl.pallas_call(
        _mem_kernel,
        grid=(b, s // bq),
        in_specs=[
            _q_spec(bq, CB_MQ),
            pl.BlockSpec((N_CB, 1, mlen, LANES), lambda b, i: (0, b, 0, 0)),
            pl.BlockSpec((N_CB, LANES, mlen), lambda b, i: (0, 0, b)),
            _gate_spec(bq, 3),
        ],
        out_specs=_out_spec(bq),
        out_shape=jax.ShapeDtypeStruct((b, s, GROUP_W), BF16),
        compiler_params=_params(2),
        name="mem_attention",
    )(proj4, km4, vmt, proj4)


def _out_proj_kernel(ya_ref, yb_ref, yc_ref, ym_ref, w_ref, g_ref, x_ref, o_ref):
    y = jnp.concatenate([ya_ref[...], yb_ref[...], yc_ref[...], ym_ref[...]], axis=-1)
    z = jnp.dot(y, w_ref[...], preferred_element_type=F32)
    z = z * lax.rsqrt(jnp.mean(z * z, axis=-1, keepdims=True) + EPS)
    o_ref[...] = x_ref[...] + z * g_ref[...]


def _out_proj(ys, w, g, x2d, tm):
    m, d = x2d.shape
    y_spec = pl.BlockSpec((tm, GROUP_W), lambda i: (i, 0))
    return pl.pallas_call(
        _out_proj_kernel,
        grid=(m // tm,),
        in_specs=[y_spec] * 4 + [
            pl.BlockSpec(w.shape, lambda i: (0, 0)),
            pl.BlockSpec((1, d), lambda i: (0, 0)),
            pl.BlockSpec((tm, d), lambda i: (i, 0)),
        ],
        out_specs=pl.BlockSpec((tm, d), lambda i: (i, 0)),
        out_shape=jax.ShapeDtypeStruct((m, d), F32),
        compiler_params=_params(1),
        name="out_proj",
    )(*ys, w, g, x2d)


def _pick(n, cands):
    for c in cands:
        if n % c == 0:
            return c
    raise ValueError(f"no tile in {cands} divides {n}")


def kernel(x, mem, w_in, b_forget, w_mem_kv, w_out, pre_norm, post_norm, mem_norm,
           diff_lambda, diff_head_norm, rel_bias):
    b, s, d = x.shape
    mlen = mem.shape[1]
    depth = w_in.shape[0]
    m = b * s
    bq = _pick(s, (256, 128))
    tm = _pick(m, (512, 256))

    gw = GROUP_W
    logit_lo, logit_hi = 9 * gw, 9 * gw + N_FOX_HEADS
    tiles = _bias_tiles(rel_bias, 2 * bq, bq)
    mem2d = mem.reshape(b * mlen, d)
    x2d = x.reshape(m, d)

    for l in range(depth):
        wl = w_in[l].astype(BF16)
        w_main = jnp.concatenate([wl[:, 0:2 * gw], wl[:, 3 * gw:5 * gw], wl[:, 6 * gw:8 * gw],
                                  wl[:, logit_hi:]], axis=1)
        w_vt = jnp.concatenate([wl[:, 2 * gw:3 * gw], wl[:, 5 * gw:6 * gw], wl[:, 8 * gw:9 * gw]],
                               axis=1).T
        w_logit = jnp.pad(wl[:, logit_lo:logit_hi], ((0, 0), (0, LANES - N_FOX_HEADS)))
        b_pad = jnp.pad(b_forget[l], (0, LANES - N_FOX_HEADS)).reshape(1, LANES)

        proj, logit, vt = _in_proj(x2d, pre_norm[l].reshape(1, d), w_main, w_logit, w_vt, tm,
                                   IN_PROJ_CHUNKS)
        proj4 = proj.reshape(N_PROJ_CB, b, s, LANES)
        negc = _forget_cumsum(logit.reshape(b, s, LANES), b_pad, _pick(s, (512, 256, 128)))

        w_kv = w_mem_kv[l].astype(BF16)
        km, vmt = _mem_kv(mem2d, mem_norm[l].reshape(1, d), w_kv[:, :gw], w_kv[:, gw:].T,
                          _pick(b * mlen, (1024, 512, 256)))
        km4 = km.reshape(N_CB, b, mlen, LANES)

        lam_init = 0.8 - 0.6 * math.exp(-0.3 * l)
        y_a = _diff_attention(proj4, vt, tiles, diff_lambda[l],
                              diff_head_norm[l].reshape(LANES, 1), lam_init, bq)
        y_b = _stick_attention(proj4, vt, bq)
        y_c = _fox_attention(proj4, vt, negc, bq)
        y_m = _mem_attention(proj4, km4, vmt, _pick(s, (1024, 512, 256, 128)))
        ys = [y.reshape(m, GROUP_W) for y in (y_a, y_b, y_c, y_m)]
        x2d = _out_proj(ys, w_out[l].astype(BF16), post_norm[l].reshape(1, d), x2d,
                        _pick(m, (1024, 512, 256)))
    return x2d.reshape(b, s, d)
```

```python
import functools
import math

import jax
import jax.numpy as jnp
import numpy as np
from jax import lax
from jax.experimental import pallas as pl
from jax.experimental.pallas import tpu as pltpu

F32 = jnp.float32
BF16 = jnp.bfloat16

LANES = 128
BF16_ROWS = 16
HEAD_DIM = 64
GROUP_W = 512
N_CB = GROUP_W // LANES
N_FOX_HEADS = GROUP_W // HEAD_DIM
NUM_BUCKETS = 32
MAX_DISTANCE = 128
EPS = 1e-6
NEG = -1e30
LOG2E = 1.4426950408889634
VMEM_LIMIT = 56 * 1024 * 1024

CB_AQ, CB_AK = 0, 4
CB_SQ, CB_SK = 8, 12
CB_FQ, CB_FK = 16, 20
CB_MQ, CB_GATE = 24, 28
N_PROJ_CB = 44
VT_A, VT_S, VT_F = 0, 1, 2

NT_DIMS = (((1,), (1,)), ((), ()))


def _params(n_axes):
    return pltpu.CompilerParams(dimension_semantics=("arbitrary",) * n_axes,
                                vmem_limit_bytes=VMEM_LIMIT)


def _split3(x):
    hi = x.astype(BF16)
    r = x - hi.astype(F32)
    mid = r.astype(BF16)
    lo = (r - mid.astype(F32)).astype(BF16)
    return hi, mid, lo


def _silu(g):
    g = g.astype(F32)
    return g / (1.0 + jnp.exp(-g))


def _rms_rows(x, g_ref):
    y = x * lax.rsqrt(jnp.mean(x * x, axis=-1, keepdims=True) + EPS)
    return (y * g_ref[...]).astype(BF16)


def _store_col_blocks(o_ref, res):
    for c in range(o_ref.shape[0]):
        o_ref[c] = res[:, c * LANES:(c + 1) * LANES]


MAX_DOT_COLUMNS = 1536


def _in_proj_kernel(n_w, x0_ref, xn_ref, g_ref, wl_ref, *refs):
    w_refs, wvt_refs = refs[:n_w], refs[n_w:-4]
    o_ref, ol_ref, ovt_ref, h_ref = refs[-4:]
    i = pl.program_id(0)

    @pl.when(i == 0)
    def _():
        h_ref[0] = _rms_rows(x0_ref[...], g_ref)

    slot = lax.rem(i, 2)
    h_ref[1 - slot] = _rms_rows(xn_ref[...], g_ref)
    h = h_ref[slot]
    ol_ref[...] = jnp.dot(h, wl_ref[...], preferred_element_type=F32)
    block = 0
    for wvt_ref in wvt_refs:
        vt = lax.dot_general(wvt_ref[...], h, NT_DIMS, preferred_element_type=F32).astype(BF16)
        for r in range(vt.shape[0] // LANES):
            ovt_ref[block + r] = vt[r * LANES:(r + 1) * LANES, :]
        block += vt.shape[0] // LANES
    block = 0
    for w_ref in w_refs:
        width = w_ref.shape[1]
        n_chunks = -(-width // MAX_DOT_COLUMNS)
        chunk = width // n_chunks
        for c in range(n_chunks):
            res = jnp.dot(h, w_ref[:, c * chunk:(c + 1) * chunk],
                          preferred_element_type=F32).astype(BF16)
            for r in range(chunk // LANES):
                o_ref[block + r] = res[:, r * LANES:(r + 1) * LANES]
            block += chunk // LANES


def _in_proj(x2d, g, ws, wl, wvts, tm):
    m, k = x2d.shape
    n = sum(w.shape[1] for w in ws)
    nv = sum(w.shape[0] for w in wvts)
    last = m // tm - 1
    resident = pl.Buffered(1)
    return pl.pallas_call(
        functools.partial(_in_proj_kernel, len(ws)),
        grid=(m // tm,),
        in_specs=[
            pl.BlockSpec((tm, k), lambda i: (0, 0), pipeline_mode=resident),
            pl.BlockSpec((tm, k), lambda i: (jnp.minimum(i + 1, last), 0)),
            pl.BlockSpec((1, k), lambda i: (0, 0)),
            pl.BlockSpec((k, LANES), lambda i: (0, 0)),
        ] + [pl.BlockSpec(w.shape, lambda i: (0, 0), pipeline_mode=resident) for w in ws + wvts],
        out_specs=[
            pl.BlockSpec((n // LANES, tm, LANES), lambda i: (0, i, 0)),
            pl.BlockSpec((tm, LANES), lambda i: (i, 0)),
            pl.BlockSpec((nv // LANES, LANES, tm), lambda i: (0, 0, i)),
        ],
        out_shape=[
            jax.ShapeDtypeStruct((n // LANES, m, LANES), BF16),
            jax.ShapeDtypeStruct((m, LANES), F32),
            jax.ShapeDtypeStruct((nv // LANES, LANES, m), BF16),
        ],
        scratch_shapes=[pltpu.VMEM((2, tm, k), BF16)],
        compiler_params=_params(1),
        name="in_proj",
    )(x2d, x2d, g, wl, *ws, *wvts)


def _mem_kv_kernel(x_ref, g_ref, wk_ref, wvt_ref, ok_ref, ovt_ref):
    h = _rms_rows(x_ref[...], g_ref)
    _store_col_blocks(ok_ref, jnp.dot(h, wk_ref[...], preferred_element_type=F32).astype(BF16))
    vt = lax.dot_general(wvt_ref[...], h, NT_DIMS, preferred_element_type=F32).astype(BF16)
    for c in range(ovt_ref.shape[0]):
        ovt_ref[c] = vt[c * LANES:(c + 1) * LANES, :]


def _mem_kv(x2d, g, wk, wvt, tm):
    m, k = x2d.shape
    n = wk.shape[1]
    return pl.pallas_call(
        _mem_kv_kernel,
        grid=(m // tm,),
        in_specs=[
            pl.BlockSpec((tm, k), lambda i: (i, 0)),
            pl.BlockSpec((1, k), lambda i: (0, 0)),
            pl.BlockSpec((k, n), lambda i: (0, 0)),
            pl.BlockSpec((n, k), lambda i: (0, 0)),
        ],
        out_specs=[
            pl.BlockSpec((n // LANES, tm, LANES), lambda i: (0, i, 0)),
            pl.BlockSpec((n // LANES, LANES, tm), lambda i: (0, 0, i)),
        ],
        out_shape=[
            jax.ShapeDtypeStruct((n // LANES, m, LANES), BF16),
            jax.ShapeDtypeStruct((n // LANES, LANES, m), BF16),
        ],
        compiler_params=_params(1),
        name="mem_kv",
    )(x2d, g, wk, wvt)


def _forget_cumsum_kernel(logit_ref, b_ref, tri_ref, place_ref, o_ref, carry_ref):
    @pl.when(pl.program_id(1) == 0)
    def _():
        carry_ref[...] = jnp.zeros_like(carry_ref)

    tb = logit_ref.shape[1]
    z = logit_ref[0] + b_ref[...]
    log_f = jnp.minimum(z, 0.0) - jnp.log1p(jnp.exp(-jnp.abs(z)))
    sums = jnp.dot(tri_ref[...], jnp.concatenate(_split3(log_f), axis=1),
                   preferred_element_type=F32)
    c = carry_ref[...] + (sums[:, :LANES] + sums[:, LANES:2 * LANES] + sums[:, 2 * LANES:])
    carry_ref[...] = c[tb - 1:tb, :]
    ext = jnp.dot(jnp.concatenate(_split3(-c), axis=1), place_ref[...],
                  preferred_element_type=F32)
    for h in range(N_CB):
        o_ref[h, 0] = ext[:, h * LANES:(h + 1) * LANES].astype(BF16)


def _placement():
    place = np.zeros((3 * LANES, N_CB * LANES), np.float32)
    for term in range(3):
        for h in range(N_CB):
            place[term * LANES + 2 * h, h * LANES + HEAD_DIM + term] = 1.0
            place[term * LANES + 2 * h + 1, h * LANES + term] = 1.0
    return jnp.asarray(place, BF16)


def _forget_cumsum(logit3, b_pad, tb):
    b, s, _ = logit3.shape
    tri = jnp.asarray(np.tril(np.ones((tb, tb), np.float32)), BF16)
    place = _placement()
    return pl.pallas_call(
        _forget_cumsum_kernel,
        grid=(b, s // tb),
        in_specs=[
            pl.BlockSpec((1, tb, LANES), lambda i, j: (i, j, 0)),
            pl.BlockSpec((1, LANES), lambda i, j: (0, 0)),
            pl.BlockSpec((tb, tb), lambda i, j: (0, 0)),
            pl.BlockSpec(place.shape, lambda i, j: (0, 0)),
        ],
        out_specs=pl.BlockSpec((N_CB, 1, tb, LANES), lambda i, j: (0, i, j, 0)),
        out_shape=jax.ShapeDtypeStruct((N_CB, b, s, LANES), BF16),
        scratch_shapes=[pltpu.VMEM((1, LANES), F32)],
        compiler_params=_params(2),
        name="forget_cumsum",
    )(logit3, b_pad, tri, place)


def _bias_tiles_kernel(rb_ref, o_ref):
    h = pl.program_id(0)
    j = pl.program_id(1)
    bk, bq = o_ref.shape[2], o_ref.shape[3]
    key = lax.broadcasted_iota(jnp.int32, (bk, bq), 0)
    query = lax.broadcasted_iota(jnp.int32, (bk, bq), 1)
    dist = query - key + j * bq
    n = jnp.maximum(dist, 0)
    max_exact = NUM_BUCKETS // 2
    nf = jnp.maximum(n, 1).astype(F32)
    large = max_exact + (jnp.log(nf / max_exact) / math.log(MAX_DISTANCE / max_exact)
                         * (NUM_BUCKETS - max_exact)).astype(jnp.int32)
    large = jnp.minimum(large, NUM_BUCKETS - 1)
    bucket = jnp.where(n < max_exact, n, large)
    far = rb_ref[NUM_BUCKETS - 1, h]
    bias = jnp.zeros((bk, bq), F32)
    for b in range(NUM_BUCKETS - 1):
        bias = jnp.where(bucket == b, rb_ref[b, h] - far, bias)
    o_ref[0, 0] = jnp.where(dist >= 0, bias, NEG)


def _bias_tiles(rel_bias, bk, bq):
    n_heads = rel_bias.shape[1]
    return pl.pallas_call(
        _bias_tiles_kernel,
        grid=(n_heads, 2),
        in_specs=[pl.BlockSpec(memory_space=pltpu.SMEM)],
        out_specs=pl.BlockSpec((1, 1, bk, bq), lambda h, j: (h, j, 0, 0)),
        out_shape=jax.ShapeDtypeStruct((n_heads, 2, bk, bq), F32),
        compiler_params=_params(2),
        name="bias_tiles",
    )(rel_bias)


def _stack_halves(q):
    lane = lax.broadcasted_iota(jnp.int32, q.shape, 1)
    zero = jnp.zeros_like(q)
    return jnp.concatenate([jnp.where(lane < HEAD_DIM, q, zero),
                            jnp.where(lane < HEAD_DIM, zero, q)], axis=0)


def _key_block(ref, h, kb, bk):
    return ref[h, 0, pl.ds(pl.multiple_of(kb * bk, bk), bk), :]


def _q_spec(bq, cb):
    return pl.BlockSpec((N_CB, 1, bq, LANES), lambda b, i: (cb // N_CB, b, i, 0))


def _kv_spec(s, cb):
    return pl.BlockSpec((N_CB, 1, s, LANES), lambda b, i: (cb // N_CB, b, 0, 0))


def _gate_spec(bq, group):
    return pl.BlockSpec((N_CB, 1, bq, LANES), lambda b, i: (CB_GATE // N_CB + group, b, i, 0))


def _out_spec(bq):
    return pl.BlockSpec((1, bq, GROUP_W), lambda b, i: (b, i, 0))


def _store_gated(o_ref, gate_ref, h, o):
    o_ref[0, :, h * LANES:(h + 1) * LANES] = (o * _silu(gate_ref[h, 0])).astype(o_ref.dtype)


def _with_ones_rows(vt):
    return jnp.concatenate([vt, jnp.ones((BF16_ROWS, vt.shape[1]), BF16)], axis=0)


def _value_t(vts, p):
    n = p.shape[1] // 2
    return jnp.concatenate([jnp.dot(vts[0], p[:, :n], preferred_element_type=F32),
                            jnp.dot(vts[1], p[:, n:], preferred_element_type=F32)], axis=1)


def _scores_t(q_st, k_ref, h, kb, bk):
    return lax.dot_general(_key_block(k_ref, h, kb, bk), q_st, NT_DIMS,
                           preferred_element_type=F32)


VALUE_LAG = 2


def _softmax_t_blocks(items, carries):
    ms = [None if carries is None else carries[h][0] for h in range(N_CB)]
    accs = [None if carries is None else carries[h][1] for h in range(N_CB)]
    pending = {}

    def score(i):
        h, score_fn, _ = items[i]
        s = score_fn()
        m = jnp.max(s, axis=0, keepdims=True)
        m_new = m if ms[h] is None else jnp.maximum(ms[h], m)
        pending[i] = (ms[h], m_new, jnp.exp(s - m_new).astype(BF16))
        ms[h] = m_new

    def value(i):
        h, _, vt_fn = items[i]
        m_old, m_new, p = pending.pop(i)
        pv = _value_t(vt_fn(), p)
        accs[h] = pv if accs[h] is None else jnp.exp(m_old - m_new) * accs[h] + pv

    for i in range(len(items) + VALUE_LAG):
        if i < len(items):
            score(i)
        if i >= VALUE_LAG:
            value(i - VALUE_LAG)
    return tuple((ms[h], accs[h]) for h in range(N_CB))


LOOP_BLOCKS = 4


def _pair_loop(n_blocks, block_of, step, carries, state_refs):
    def load():
        return tuple(tuple(ref[h] for ref in state_refs) for h in range(N_CB))

    def store(carries):
        for h, chain in enumerate(carries):
            for ref, value in zip(state_refs, chain):
                ref[h] = value

    def run(first, width):
        store(step(tuple(block_of(first + j) for j in range(width)), load()))

    store(carries)

    def trip(i, _):
        run(LOOP_BLOCKS * i, LOOP_BLOCKS)
        return 0

    lax.fori_loop(0, n_blocks // LOOP_BLOCKS, trip, 0)
    done = n_blocks - n_blocks % LOOP_BLOCKS

    @pl.when(n_blocks - done >= 2)
    def _():
        run(done, 2)

    @pl.when(n_blocks % 2 == 1)
    def _():
        run(n_blocks - 1, 1)

    return load()


def _normalized_t(acc):
    d = acc.shape[0] - BF16_ROWS
    return acc[:d] * (1.0 / acc[d:d + 1])


def _diff_kernel(lam_init, q_ref, k_ref, vt_ref, gate_ref, tile_ref, lamp_ref, hn_ref, o_ref,
                 m_ref, acc_ref):
    bq = q_ref.shape[2]
    qi = pl.program_id(1)
    qs = [_stack_halves(q_ref[h, 0] * (HEAD_DIM ** -0.5)) for h in range(N_CB)]

    def item(h, kb, tile_fn=None, width=1):
        rows = width * bq

        def score():
            k = k_ref[h, 0, pl.ds(pl.multiple_of(kb * bq, bq), rows), :]
            s = lax.dot_general(k, qs[h], NT_DIMS, preferred_element_type=F32)
            if tile_fn is None:
                return s
            tile = tile_fn(h)
            return s + jnp.concatenate([tile, tile], axis=1)

        def values():
            vt = _with_ones_rows(vt_ref[h, :, pl.ds(pl.multiple_of(kb * bq, bq), rows)])
            return vt, vt
        return h, score, values

    def step(kbs, carries):
        if len(kbs) % 2 == 0:
            items = [item(h, kb, None, 2) for kb in kbs[::2] for h in range(N_CB)]
        else:
            items = [item(h, kb) for kb in kbs for h in range(N_CB)]
        return _softmax_t_blocks(items, carries)

    near = [item(h, jnp.maximum(qi - 1, 0), lambda h: tile_ref[h, jnp.minimum(qi, 1)], 2)
            for h in range(N_CB)]
    carries = _softmax_t_blocks(near, None)
    carries = _pair_loop(jnp.maximum(qi - 1, 0), lambda i: i, step, carries, (m_ref, acc_ref))

    lp = lamp_ref[...]
    lam = (jnp.exp(jnp.sum(lp[0:1] * lp[1:2], axis=-1, keepdims=True))
           - jnp.exp(jnp.sum(lp[2:3] * lp[3:4], axis=-1, keepdims=True)) + lam_init)
    head_gain = jnp.broadcast_to(hn_ref[...] * (1.0 - lam_init), (LANES, bq))
    for h in range(N_CB):
        a = _normalized_t(carries[h][1])
        o = a[:, :bq] - lam * a[:, bq:]
        o = o * lax.rsqrt(jnp.mean(o * o, axis=0, keepdims=True) + EPS)
        _store_gated(o_ref, gate_ref, h, (o * head_gain).T)


def _diff_attention(proj4, vt, tiles, lam_params, head_norm, lam_init, bq):
    _, b, s, _ = proj4.shape
    return pl.pallas_call(
        functools.partial(_diff_kernel, lam_init),
        grid=(b, s // bq),
        in_specs=[
            _q_spec(bq, CB_AQ), _kv_spec(s, CB_AK),
            pl.BlockSpec((N_CB, LANES, s), lambda b, i: (VT_A, 0, b)),
            _gate_spec(bq, 0),
            pl.BlockSpec((N_CB, 2, 2 * bq, bq), lambda b, i: (0, 0, 0, 0)),
            pl.BlockSpec((4, HEAD_DIM), lambda b, i: (0, 0)),
            pl.BlockSpec((LANES, 1), lambda b, i: (0, 0)),
        ],
        out_specs=_out_spec(bq),
        out_shape=jax.ShapeDtypeStruct((b, s, GROUP_W), BF16),
        scratch_shapes=[pltpu.VMEM((N_CB, 1, 2 * bq), F32),
                        pltpu.VMEM((N_CB, LANES + BF16_ROWS, 2 * bq), F32)],
        compiler_params=_params(2),
        name="diff_attention",
    )(proj4, proj4, vt, proj4, tiles, lam_params, head_norm)


def _stick_kernel(q_ref, k_ref, vt_ref, gate_ref, o_ref, rest_ref, acc_ref):
    bq = q_ref.shape[2]
    qi = pl.program_id(1)
    qs = [_stack_halves(q_ref[h, 0] * (HEAD_DIM ** -0.5)) for h in range(N_CB)]
    key = lax.broadcasted_iota(jnp.int32, (bq, 2 * bq), 0)
    query = lax.broadcasted_iota(jnp.int32, (bq, 2 * bq), 1) & (bq - 1)
    strict = key < query

    row = lax.broadcasted_iota(jnp.int32, (bq + BF16_ROWS, bq), 0)
    col = lax.broadcasted_iota(jnp.int32, (bq + BF16_ROWS, bq), 1)
    after = jnp.where((col > row) | (row >= bq), 1.0, 0.0).astype(BF16)

    def step(kbs, carries, mask=None):
        items = [(h, kb) for kb in kbs for h in range(N_CB)]
        rests = [carries[h][0] for h in range(N_CB)]
        accs = [carries[h][1] for h in range(N_CB)]
        pending = {}

        def score(i):
            h, kb = items[i]
            z = _scores_t(qs[h], k_ref, h, kb, bq)
            sp = jnp.log(1.0 + jnp.exp2(jnp.abs(z) * -LOG2E))
            ls_pos = jnp.minimum(z, 0.0) - sp
            ls_neg = ls_pos - z
            if mask is not None:
                ls_neg = jnp.where(mask, ls_neg, 0.0)
            pending[i] = (ls_pos, ls_neg.astype(BF16))

        def cumsum(i):
            h, _ = items[i]
            ls_pos, log_rest = pending[i]
            t = jnp.dot(after, log_rest, preferred_element_type=F32)
            a = jnp.exp2((ls_pos + t[:bq]) * LOG2E)
            if mask is not None:
                a = jnp.where(mask, a, 0.0)
            pending[i] = (a.astype(BF16), jnp.exp2(rests[h] * LOG2E))
            rests[h] = rests[h] + t[bq:bq + 1]

        def value(i):
            h, kb = items[i]
            a, scale = pending.pop(i)
            vt = vt_ref[h, :, pl.ds(pl.multiple_of(kb * bq, bq), bq)]
            accs[h] = accs[h] + scale * _value_t((vt[:HEAD_DIM], vt[HEAD_DIM:]), a)

        for i in range(len(items) + 2):
            if i < len(items):
                score(i)
            if 1 <= i <= len(items):
                cumsum(i - 1)
            if i >= 2:
                value(i - 2)
        return tuple((rests[h], accs[h]) for h in range(N_CB))

    zero = (jnp.zeros((1, 2 * bq), F32), jnp.zeros((HEAD_DIM, 2 * bq), F32))
    carries = step((qi,), (zero,) * N_CB, strict)
    carries = _pair_loop(qi, lambda i: qi - 1 - i, step, carries, (rest_ref, acc_ref))
    for h in range(N_CB):
        acc = carries[h][1]
        _store_gated(o_ref, gate_ref, h, jnp.concatenate([acc[:, :bq], acc[:, bq:]], axis=0).T)


def _stick_attention(proj4, vt, bq):
    _, b, s, _ = proj4.shape
    return pl.pallas_call(
        _stick_kernel,
        grid=(b, s // bq),
        in_specs=[
            _q_spec(bq, CB_SQ), _kv_spec(s, CB_SK),
            pl.BlockSpec((N_CB, LANES, s), lambda b, i: (VT_S, 0, b)),
            _gate_spec(bq, 1),
        ],
        out_specs=_out_spec(bq),
        out_shape=jax.ShapeDtypeStruct((b, s, GROUP_W), BF16),
        scratch_shapes=[pltpu.VMEM((N_CB, 1, 2 * bq), F32),
                        pltpu.VMEM((N_CB, HEAD_DIM, 2 * bq), F32)],
        compiler_params=_params(2),
        name="stick_attention",
    )(proj4, proj4, vt, proj4)


def _fox_kernel(q_ref, k_ref, vt_ref, gate_ref, negc_ref, o_ref, m_ref, acc_ref):
    bq = q_ref.shape[2]
    qi = pl.program_id(1)
    key = lax.broadcasted_iota(jnp.int32, (bq, 2 * bq), 0)
    query = lax.broadcasted_iota(jnp.int32, (bq, 2 * bq), 1) & (bq - 1)
    causal = key <= query

    lane = lax.broadcasted_iota(jnp.int32, (1, LANES), 1)
    low = lane < HEAD_DIM
    ones_low = jnp.where(lane < 3, 1.0, 0.0).astype(BF16)
    ones_high = jnp.where(low | (lane >= HEAD_DIM + 3), 0.0, 1.0).astype(BF16)
    qs = []
    for h in range(N_CB):
        q = q_ref[h, 0] * (HEAD_DIM ** -0.5)
        qs.append((jnp.where(low, q, ones_high), jnp.where(low, ones_low, q)))

    def item(h, kb, mask, width=1):
        rows = width * bq

        def score():
            start = pl.multiple_of(kb * bq, bq)
            k = k_ref[h, 0, pl.ds(start, rows), :]
            c = negc_ref[h, 0, pl.ds(start, rows), :]
            s = jnp.concatenate(
                [lax.dot_general(jnp.where(low, k, c), qs[h][0], NT_DIMS,
                                 preferred_element_type=F32),
                 lax.dot_general(jnp.where(low, c, k), qs[h][1], NT_DIMS,
                                 preferred_element_type=F32)], axis=1)
            return s if mask is None else jnp.where(mask, s, NEG)

        def values():
            vt = vt_ref[h, :, pl.ds(pl.multiple_of(kb * bq, bq), rows)]
            return _with_ones_rows(vt[:HEAD_DIM]), _with_ones_rows(vt[HEAD_DIM:])
        return h, score, values

    def step(kbs, carries, mask=None):
        if len(kbs) % 2 == 0:
            items = [item(h, kb, mask, 2) for kb in kbs[::2] for h in range(N_CB)]
        else:
            items = [item(h, kb, mask) for kb in kbs for h in range(N_CB)]
        return _softmax_t_blocks(items, carries)

    carries = _pair_loop(qi, lambda i: i, step, step((qi,), None, causal), (m_ref, acc_ref))
    for h in range(N_CB):
        a = _normalized_t(carries[h][1])
        _store_gated(o_ref, gate_ref, h, jnp.concatenate([a[:, :bq], a[:, bq:]], axis=0).T)


def _fox_attention(proj4, vt, negc, bq):
    _, b, s, _ = proj4.shape
    return pl.pallas_call(
        _fox_kernel,
        grid=(b, s // bq),
        in_specs=[
            _q_spec(bq, CB_FQ), _kv_spec(s, CB_FK),
            pl.BlockSpec((N_CB, LANES, s), lambda b, i: (VT_F, 0, b)),
            _gate_spec(bq, 2),
            pl.BlockSpec((N_CB, 1, s, LANES), lambda b, i: (0, b, 0, 0)),
        ],
        out_specs=_out_spec(bq),
        out_shape=jax.ShapeDtypeStruct((b, s, GROUP_W), BF16),
        scratch_shapes=[pltpu.VMEM((N_CB, 1, 2 * bq), F32),
                        pltpu.VMEM((N_CB, HEAD_DIM + BF16_ROWS, 2 * bq), F32)],
        compiler_params=_params(2),
        name="fox_attention",
    )(proj4, proj4, vt, proj4, negc)


def _mem_kernel(q_ref, km_ref, vmt_ref, gate_ref, o_ref):
    scale = LANES ** -0.5
    scores = [lax.dot_general(km_ref[h, 0], q_ref[h, 0], NT_DIMS,
                              preferred_element_type=F32) * scale for h in range(N_CB)]
    probs = [jnp.exp(s - jnp.max(s, axis=0, keepdims=True)).astype(BF16) for s in scores]
    for h in range(N_CB):
        acc = jnp.dot(_with_ones_rows(vmt_ref[h]), probs[h], preferred_element_type=F32)
        _store_gated(o_ref, gate_ref, h, _normalized_t(acc).T)


def _mem_attention(proj4, km4, vmt, bq):
    _, b, s, _ = proj4.shape
    mlen = km4.shape[2]
    return pl.pallas_call(
        _mem_kernel,
        grid=(b, s // bq),
        in_specs=[
            _q_spec(bq, CB_MQ),
            pl.BlockSpec((N_CB, 1, mlen, LANES), lambda b, i: (0, b, 0, 0)),
            pl.BlockSpec((N_CB, LANES, mlen), lambda b, i: (0, 0, b)),
            _gate_spec(bq, 3),
        ],
        out_specs=_out_spec(bq),
        out_shape=jax.ShapeDtypeStruct((b, s, GROUP_W), BF16),
        compiler_params=_params(2),
        name="mem_attention",
    )(proj4, km4, vmt, proj4)


def _out_proj_kernel(ya_ref, yb_ref, yc_ref, ym_ref, w_ref, g_ref, x_ref, o_ref):
    y = jnp.concatenate([ya_ref[...], yb_ref[...], yc_ref[...], ym_ref[...]], axis=-1)
    z = jnp.dot(y, w_ref[...], preferred_element_type=F32)
    z = z * lax.rsqrt(jnp.mean(z * z, axis=-1, keepdims=True) + EPS)
    o_ref[...] = x_ref[...] + z * g_ref[...]


def _out_proj(ys, w, g, x2d, tm):
    m, d = x2d.shape
    y_spec = pl.BlockSpec((tm, GROUP_W), lambda i: (i, 0))
    return pl.pallas_call(
        _out_proj_kernel,
        grid=(m // tm,),
        in_specs=[y_spec] * 4 + [
            pl.BlockSpec(w.shape, lambda i: (0, 0)),
            pl.BlockSpec((1, d), lambda i: (0, 0)),
            pl.BlockSpec((tm, d), lambda i: (i, 0)),
        ],
        out_specs=pl.BlockSpec((tm, d), lambda i: (i, 0)),
        out_shape=jax.ShapeDtypeStruct((m, d), F32),
        compiler_params=_params(1),
        name="out_proj",
    )(*ys, w, g, x2d)


def _pick(n, cands):
    for c in cands:
        if n % c == 0:
            return c
    raise ValueError(f"no tile in {cands} divides {n}")


def kernel(x, mem, w_in, b_forget, w_mem_kv, w_out, pre_norm, post_norm, mem_norm,
           diff_lambda, diff_head_norm, rel_bias):
    b, s, d = x.shape
    mlen = mem.shape[1]
    depth = w_in.shape[0]
    m = b * s
    bq = _pick(s, (256, 128))
    tm = _pick(m, (512, 256))

    gw = GROUP_W
    logit_lo, logit_hi = 9 * gw, 9 * gw + N_FOX_HEADS
    tiles = _bias_tiles(rel_bias, 2 * bq, bq)
    mem2d = mem.reshape(b * mlen, d)
    x2d = x.reshape(m, d)

    for l in range(depth):
        wl = w_in[l]
        ws = [wl[:, lo:hi].astype(BF16) for lo, hi in
              ((0, 2 * gw), (3 * gw, 5 * gw), (6 * gw, 8 * gw), (logit_hi, wl.shape[1]))]
        wvts = [wl[:, lo:lo + gw].T.astype(BF16) for lo in (2 * gw, 5 * gw, 8 * gw)]
        w_logit = jnp.pad(wl[:, logit_lo:logit_hi],
                          ((0, 0), (0, LANES - N_FOX_HEADS))).astype(BF16)
        b_pad = jnp.pad(b_forget[l], (0, LANES - N_FOX_HEADS)).reshape(1, LANES)

        proj, logit, vt = _in_proj(x2d, pre_norm[l].reshape(1, d), ws, w_logit, wvts, tm)
        proj4 = proj.reshape(N_PROJ_CB, b, s, LANES)
        negc = _forget_cumsum(logit.reshape(b, s, LANES), b_pad, _pick(s, (512, 256, 128)))

        w_kv = w_mem_kv[l].astype(BF16)
        km, vmt = _mem_kv(mem2d, mem_norm[l].reshape(1, d), w_kv[:, :gw], w_kv[:, gw:].T,
                          _pick(b * mlen, (1024, 512, 256)))
        km4 = km.reshape(N_CB, b, mlen, LANES)

        lam_init = 0.8 - 0.6 * math.exp(-0.3 * l)
        y_a = _diff_attention(proj4, vt, tiles, diff_lambda[l],
                              diff_head_norm[l].reshape(LANES, 1), lam_init, bq)
        y_b = _stick_attention(proj4, vt, bq)
        y_c = _fox_attention(proj4, vt, negc, bq)
        y_m = _mem_attention(proj4, km4, vmt, _pick(s, (1024, 512, 256, 128)))
        ys = [y.reshape(m, GROUP_W) for y in (y_a, y_b, y_c, y_m)]
        x2d = _out_proj(ys, w_out[l].astype(BF16), post_norm[l].reshape(1, d), x2d,
                        _pick(m, (1024, 512, 256)))
    return x2d.reshape(b, s, d)
```

```python
import functools
import math

import jax
import jax.numpy as jnp
import numpy as np
from jax import lax
from jax.experimental import pallas as pl
from jax.experimental.pallas import tpu as pltpu

F32 = jnp.float32
BF16 = jnp.bfloat16

LANES = 128
BF16_ROWS = 16
HEAD_DIM = 64
GROUP_W = 512
N_CB = GROUP_W // LANES
N_FOX_HEADS = GROUP_W // HEAD_DIM
NUM_BUCKETS = 32
MAX_DISTANCE = 128
EPS = 1e-6
NEG = -1e30
LOG2E = 1.4426950408889634
VMEM_LIMIT = 56 * 1024 * 1024

CB_AQ, CB_AK = 0, 4
CB_SQ, CB_SK = 8, 12
CB_FQ, CB_FK = 16, 20
CB_MQ, CB_GATE = 24, 28
N_PROJ_CB = 44
VT_A, VT_S, VT_F = 0, 1, 2

NT_DIMS = (((1,), (1,)), ((), ()))


def _params(n_axes):
    return pltpu.CompilerParams(dimension_semantics=("arbitrary",) * n_axes,
                                vmem_limit_bytes=VMEM_LIMIT)


def _split3(x):
    hi = x.astype(BF16)
    r = x - hi.astype(F32)
    mid = r.astype(BF16)
    lo = (r - mid.astype(F32)).astype(BF16)
    return hi, mid, lo


def _silu(g):
    g = g.astype(F32)
    return g / (1.0 + jnp.exp(-g))


def _rms_rows(x, g_ref):
    y = x * lax.rsqrt(jnp.mean(x * x, axis=-1, keepdims=True) + EPS)
    return (y * g_ref[...]).astype(BF16)


def _store_col_blocks(o_ref, res):
    for c in range(o_ref.shape[0]):
        o_ref[c] = res[:, c * LANES:(c + 1) * LANES]


MAX_DOT_COLUMNS = 1536


def _in_proj_kernel(n_w, x0_ref, xn_ref, g_ref, wl_ref, *refs):
    w_refs, wvt_refs = refs[:n_w], refs[n_w:-4]
    o_ref, ol_ref, ovt_ref, h_ref = refs[-4:]
    i = pl.program_id(0)

    @pl.when(i == 0)
    def _():
        h_ref[0] = _rms_rows(x0_ref[...], g_ref)

    slot = lax.rem(i, 2)
    h_ref[1 - slot] = _rms_rows(xn_ref[...], g_ref)
    h = h_ref[slot]
    ol_ref[...] = jnp.dot(h, wl_ref[...], preferred_element_type=F32)
    block = 0
    for wvt_ref in wvt_refs:
        vt = lax.dot_general(wvt_ref[...], h, NT_DIMS, preferred_element_type=F32).astype(BF16)
        for r in range(vt.shape[0] // LANES):
            ovt_ref[block + r] = vt[r * LANES:(r + 1) * LANES, :]
        block += vt.shape[0] // LANES
    block = 0
    for w_ref in w_refs:
        width = w_ref.shape[1]
        n_chunks = -(-width // MAX_DOT_COLUMNS)
        chunk = width // n_chunks
        for c in range(n_chunks):
            res = jnp.dot(h, w_ref[:, c * chunk:(c + 1) * chunk],
                          preferred_element_type=F32).astype(BF16)
            for r in range(chunk // LANES):
                o_ref[block + r] = res[:, r * LANES:(r + 1) * LANES]
            block += chunk // LANES


def _in_proj(x2d, g, ws, wl, wvts, tm):
    m, k = x2d.shape
    n = sum(w.shape[1] for w in ws)
    nv = sum(w.shape[0] for w in wvts)
    last = m // tm - 1
    resident = pl.Buffered(1)
    return pl.pallas_call(
        functools.partial(_in_proj_kernel, len(ws)),
        grid=(m // tm,),
        in_specs=[
            pl.BlockSpec((tm, k), lambda i: (0, 0), pipeline_mode=resident),
            pl.BlockSpec((tm, k), lambda i: (jnp.minimum(i + 1, last), 0)),
            pl.BlockSpec((1, k), lambda i: (0, 0)),
            pl.BlockSpec((k, LANES), lambda i: (0, 0)),
        ] + [pl.BlockSpec(w.shape, lambda i: (0, 0), pipeline_mode=resident) for w in ws + wvts],
        out_specs=[
            pl.BlockSpec((n // LANES, tm, LANES), lambda i: (0, i, 0)),
            pl.BlockSpec((tm, LANES), lambda i: (i, 0)),
            pl.BlockSpec((nv // LANES, LANES, tm), lambda i: (0, 0, i)),
        ],
        out_shape=[
            jax.ShapeDtypeStruct((n // LANES, m, LANES), BF16),
            jax.ShapeDtypeStruct((m, LANES), F32),
            jax.ShapeDtypeStruct((nv // LANES, LANES, m), BF16),
        ],
        scratch_shapes=[pltpu.VMEM((2, tm, k), BF16)],
        compiler_params=_params(1),
        name="in_proj",
    )(x2d, x2d, g, wl, *ws, *wvts)


def _mem_kv_kernel(x_ref, g_ref, wk_ref, wvt_ref, ok_ref, ovt_ref):
    h = _rms_rows(x_ref[...], g_ref)
    _store_col_blocks(ok_ref, jnp.dot(h, wk_ref[...], preferred_element_type=F32).astype(BF16))
    vt = lax.dot_general(wvt_ref[...], h, NT_DIMS, preferred_element_type=F32).astype(BF16)
    for c in range(ovt_ref.shape[0]):
        ovt_ref[c] = vt[c * LANES:(c + 1) * LANES, :]


def _mem_kv(x2d, g, wk, wvt, tm):
    m, k = x2d.shape
    n = wk.shape[1]
    return pl.pallas_call(
        _mem_kv_kernel,
        grid=(m // tm,),
        in_specs=[
            pl.BlockSpec((tm, k), lambda i: (i, 0)),
            pl.BlockSpec((1, k), lambda i: (0, 0)),
            pl.BlockSpec((k, n), lambda i: (0, 0)),
            pl.BlockSpec((n, k), lambda i: (0, 0)),
        ],
        out_specs=[
            pl.BlockSpec((n // LANES, tm, LANES), lambda i: (0, i, 0)),
            pl.BlockSpec((n // LANES, LANES, tm), lambda i: (0, 0, i)),
        ],
        out_shape=[
            jax.ShapeDtypeStruct((n // LANES, m, LANES), BF16),
            jax.ShapeDtypeStruct((n // LANES, LANES, m), BF16),
        ],
        compiler_params=_params(1),
        name="mem_kv",
    )(x2d, g, wk, wvt)


def _forget_cumsum_kernel(logit_ref, b_ref, tri_ref, place_ref, o_ref, carry_ref):
    @pl.when(pl.program_id(1) == 0)
    def _():
        carry_ref[...] = jnp.zeros_like(carry_ref)

    n_seq, tb = logit_ref.shape[0], logit_ref.shape[1]
    sums = []
    for r in range(n_seq):
        z = logit_ref[r] + b_ref[...]
        log_f = jnp.minimum(z, 0.0) - jnp.log1p(jnp.exp(-jnp.abs(z)))
        sums.append(jnp.dot(tri_ref[...], jnp.concatenate(_split3(log_f), axis=1),
                            preferred_element_type=F32))
    exts = []
    for r in range(n_seq):
        c = carry_ref[r] + (sums[r][:, :LANES] + sums[r][:, LANES:2 * LANES]
                            + sums[r][:, 2 * LANES:])
        carry_ref[r] = c[tb - 1:tb, :]
        exts.append(jnp.dot(jnp.concatenate(_split3(-c), axis=1), place_ref[...],
                            preferred_element_type=F32))
    for r in range(n_seq):
        for h in range(N_CB):
            o_ref[h, r] = exts[r][:, h * LANES:(h + 1) * LANES].astype(BF16)


def _placement():
    place = np.zeros((3 * LANES, N_CB * LANES), np.float32)
    for term in range(3):
        for h in range(N_CB):
            place[term * LANES + 2 * h, h * LANES + HEAD_DIM + term] = 1.0
            place[term * LANES + 2 * h + 1, h * LANES + term] = 1.0
    return jnp.asarray(place, BF16)


def _forget_cumsum(logit3, b_pad, tb):
    b, s, _ = logit3.shape
    n_seq = _pick(b, (4, 2, 1))
    tri = jnp.asarray(np.tril(np.ones((tb, tb), np.float32)), BF16)
    place = _placement()
    return pl.pallas_call(
        _forget_cumsum_kernel,
        grid=(b // n_seq, s // tb),
        in_specs=[
            pl.BlockSpec((n_seq, tb, LANES), lambda i, j: (i, j, 0)),
            pl.BlockSpec((1, LANES), lambda i, j: (0, 0)),
            pl.BlockSpec((tb, tb), lambda i, j: (0, 0)),
            pl.BlockSpec(place.shape, lambda i, j: (0, 0)),
        ],
        out_specs=pl.BlockSpec((N_CB, n_seq, tb, LANES), lambda i, j: (0, i, j, 0)),
        out_shape=jax.ShapeDtypeStruct((N_CB, b, s, LANES), BF16),
        scratch_shapes=[pltpu.VMEM((n_seq, 1, LANES), F32)],
        compiler_params=_params(2),
        name="forget_cumsum",
    )(logit3, b_pad, tri, place)


def _t5_buckets(bk, bq):
    key = np.arange(bk, dtype=np.int32)[:, None]
    query = np.arange(bq, dtype=np.int32)[None, :]
    dist = np.stack([query - key, query - key + bq])
    n = np.maximum(dist, 0)
    max_exact = NUM_BUCKETS // 2
    nf = np.maximum(n, 1).astype(np.float32)
    large = max_exact + (np.log(nf / np.float32(max_exact))
                         / np.float32(math.log(MAX_DISTANCE / max_exact))
                         * np.float32(NUM_BUCKETS - max_exact)).astype(np.int32)
    large = np.minimum(large, NUM_BUCKETS - 1)
    bucket = np.where(n < max_exact, n, large)
    return jnp.asarray(np.where(dist >= 0, bucket, -1), jnp.int32)


def _bias_tiles_kernel(rb_ref, bucket_ref, o_ref):
    h = pl.program_id(0)
    bucket = bucket_ref[0]
    far = rb_ref[NUM_BUCKETS - 1, h]
    bias = jnp.zeros(bucket.shape, F32)
    for b in range(NUM_BUCKETS - 1):
        bias = jnp.where(bucket == b, rb_ref[b, h] - far, bias)
    o_ref[0, 0] = jnp.where(bucket >= 0, bias, NEG)


def _bias_tiles(rel_bias, bk, bq):
    n_heads = rel_bias.shape[1]
    return pl.pallas_call(
        _bias_tiles_kernel,
        grid=(n_heads, 2),
        in_specs=[pl.BlockSpec(memory_space=pltpu.SMEM),
                  pl.BlockSpec((1, bk, bq), lambda h, j: (j, 0, 0))],
        out_specs=pl.BlockSpec((1, 1, bk, bq), lambda h, j: (h, j, 0, 0)),
        out_shape=jax.ShapeDtypeStruct((n_heads, 2, bk, bq), F32),
        compiler_params=_params(2),
        name="bias_tiles",
    )(rel_bias, _t5_buckets(bk, bq))


def _stack_halves(q):
    lane = lax.broadcasted_iota(jnp.int32, q.shape, 1)
    zero = jnp.zeros_like(q)
    return jnp.concatenate([jnp.where(lane < HEAD_DIM, q, zero),
                            jnp.where(lane < HEAD_DIM, zero, q)], axis=0)


def _key_block(ref, h, kb, bk):
    return ref[h, 0, pl.ds(pl.multiple_of(kb * bk, bk), bk), :]


def _q_spec(bq, cb):
    return pl.BlockSpec((N_CB, 1, bq, LANES), lambda b, i: (cb // N_CB, b, i, 0))


def _kv_spec(s, cb):
    return pl.BlockSpec((N_CB, 1, s, LANES), lambda b, i: (cb // N_CB, b, 0, 0))


def _gate_spec(bq, group):
    return pl.BlockSpec((N_CB, 1, bq, LANES), lambda b, i: (CB_GATE // N_CB + group, b, i, 0))


def _out_spec(bq):
    return pl.BlockSpec((1, bq, GROUP_W), lambda b, i: (b, i, 0))


def _store_gated(o_ref, gate_ref, h, o):
    o_ref[0, :, h * LANES:(h + 1) * LANES] = (o * _silu(gate_ref[h, 0])).astype(o_ref.dtype)


def _with_ones_rows(vt):
    return jnp.concatenate([vt, jnp.ones((BF16_ROWS, vt.shape[1]), BF16)], axis=0)


def _value_t(vts, p):
    n = p.shape[1] // 2
    return jnp.concatenate([jnp.dot(vts[0], p[:, :n], preferred_element_type=F32),
                            jnp.dot(vts[1], p[:, n:], preferred_element_type=F32)], axis=1)


def _scores_t(q_st, k_ref, h, kb, bk):
    return lax.dot_general(_key_block(k_ref, h, kb, bk), q_st, NT_DIMS,
                           preferred_element_type=F32)


VALUE_LAG = 2


def _softmax_t_blocks(items, carries):
    ms = [None if carries is None else carries[h][0] for h in range(N_CB)]
    accs = [None if carries is None else carries[h][1] for h in range(N_CB)]
    pending = {}

    def score(i):
        h, score_fn, _ = items[i]
        s = score_fn()
        m = jnp.max(s, axis=0, keepdims=True)
        m_new = m if ms[h] is None else jnp.maximum(ms[h], m)
        pending[i] = (ms[h], m_new, jnp.exp(s - m_new).astype(BF16))
        ms[h] = m_new

    def value(i):
        h, _, vt_fn = items[i]
        m_old, m_new, p = pending.pop(i)
        pv = _value_t(vt_fn(), p)
        accs[h] = pv if accs[h] is None else jnp.exp(m_old - m_new) * accs[h] + pv

    for i in range(len(items) + VALUE_LAG):
        if i < len(items):
            score(i)
        if i >= VALUE_LAG:
            value(i - VALUE_LAG)
    return tuple((ms[h], accs[h]) for h in range(N_CB))


LOOP_BLOCKS = 4


def _pair_loop(n_blocks, block_of, step, carries, state_refs):
    def load():
        return tuple(tuple(ref[h] for ref in state_refs) for h in range(N_CB))

    def store(carries):
        for h, chain in enumerate(carries):
            for ref, value in zip(state_refs, chain):
                ref[h] = value

    def run(first, width):
        store(step(tuple(block_of(first + j) for j in range(width)), load()))

    store(carries)

    def trip(i, _):
        run(LOOP_BLOCKS * i, LOOP_BLOCKS)
        return 0

    lax.fori_loop(0, n_blocks // LOOP_BLOCKS, trip, 0)
    done = n_blocks - n_blocks % LOOP_BLOCKS

    @pl.when(n_blocks - done >= 2)
    def _():
        run(done, 2)

    @pl.when(n_blocks % 2 == 1)
    def _():
        run(n_blocks - 1, 1)

    return load()


def _normalized_t(acc):
    d = acc.shape[0] - BF16_ROWS
    return acc[:d] * (1.0 / acc[d:d + 1])


def _diff_kernel(lam_init, q_ref, k_ref, vt_ref, gate_ref, tile_ref, lamp_ref, hn_ref, o_ref,
                 m_ref, acc_ref):
    bq = q_ref.shape[2]
    qi = pl.program_id(1)
    qs = [_stack_halves(q_ref[h, 0] * (HEAD_DIM ** -0.5)) for h in range(N_CB)]

    def item(h, kb, tile_fn=None, width=1):
        rows = width * bq

        def score():
            k = k_ref[h, 0, pl.ds(pl.multiple_of(kb * bq, bq), rows), :]
            s = lax.dot_general(k, qs[h], NT_DIMS, preferred_element_type=F32)
            if tile_fn is None:
                return s
            tile = tile_fn(h)
            return s + jnp.concatenate([tile, tile], axis=1)

        def values():
            vt = _with_ones_rows(vt_ref[h, :, pl.ds(pl.multiple_of(kb * bq, bq), rows)])
            return vt, vt
        return h, score, values

    def step(kbs, carries):
        if len(kbs) % 2 == 0:
            items = [item(h, kb, None, 2) for kb in kbs[::2] for h in range(N_CB)]
        else:
            items = [item(h, kb) for kb in kbs for h in range(N_CB)]
        return _softmax_t_blocks(items, carries)

    near = [item(h, jnp.maximum(qi - 1, 0), lambda h: tile_ref[h, jnp.minimum(qi, 1)], 2)
            for h in range(N_CB)]
    carries = _softmax_t_blocks(near, None)
    carries = _pair_loop(jnp.maximum(qi - 1, 0), lambda i: i, step, carries, (m_ref, acc_ref))

    lp = lamp_ref[...]
    lam = (jnp.exp(jnp.sum(lp[0:1] * lp[1:2], axis=-1, keepdims=True))
           - jnp.exp(jnp.sum(lp[2:3] * lp[3:4], axis=-1, keepdims=True)) + lam_init)
    head_gain = jnp.broadcast_to(hn_ref[...] * (1.0 - lam_init), (LANES, bq))
    for h in range(N_CB):
        a = _normalized_t(carries[h][1])
        o = a[:, :bq] - lam * a[:, bq:]
        o = o * lax.rsqrt(jnp.mean(o * o, axis=0, keepdims=True) + EPS)
        _store_gated(o_ref, gate_ref, h, (o * head_gain).T)


def _diff_attention(proj4, vt, tiles, lam_params, head_norm, lam_init, bq):
    _, b, s, _ = proj4.shape
    return pl.pallas_call(
        functools.partial(_diff_kernel, lam_init),
        grid=(b, s // bq),
        in_specs=[
            _q_spec(bq, CB_AQ), _kv_spec(s, CB_AK),
            pl.BlockSpec((N_CB, LANES, s), lambda b, i: (VT_A, 0, b)),
            _gate_spec(bq, 0),
            pl.BlockSpec((N_CB, 2, 2 * bq, bq), lambda b, i: (0, 0, 0, 0)),
            pl.BlockSpec((4, HEAD_DIM), lambda b, i: (0, 0)),
            pl.BlockSpec((LANES, 1), lambda b, i: (0, 0)),
        ],
        out_specs=_out_spec(bq),
        out_shape=jax.ShapeDtypeStruct((b, s, GROUP_W), BF16),
        scratch_shapes=[pltpu.VMEM((N_CB, 1, 2 * bq), F32),
                        pltpu.VMEM((N_CB, LANES + BF16_ROWS, 2 * bq), F32)],
        compiler_params=_params(2),
        name="diff_attention",
    )(proj4, proj4, vt, proj4, tiles, lam_params, head_norm)


def _stick_kernel(q_ref, k_ref, vt_ref, gate_ref, o_ref, rest_ref, acc_ref):
    bq = q_ref.shape[2]
    qi = pl.program_id(1)
    qs = [_stack_halves(q_ref[h, 0] * (HEAD_DIM ** -0.5)) for h in range(N_CB)]
    key = lax.broadcasted_iota(jnp.int32, (bq, 2 * bq), 0)
    query = lax.broadcasted_iota(jnp.int32, (bq, 2 * bq), 1) & (bq - 1)
    strict = key < query

    row = lax.broadcasted_iota(jnp.int32, (bq + BF16_ROWS, bq), 0)
    col = lax.broadcasted_iota(jnp.int32, (bq + BF16_ROWS, bq), 1)
    after = jnp.where((col > row) | (row >= bq), 1.0, 0.0).astype(BF16)

    def step(kbs, carries, mask=None):
        items = [(h, kb) for kb in kbs for h in range(N_CB)]
        rests = [carries[h][0] for h in range(N_CB)]
        accs = [carries[h][1] for h in range(N_CB)]
        pending = {}

        def score(i):
            h, kb = items[i]
            z = _scores_t(qs[h], k_ref, h, kb, bq)
            sp = jnp.log(1.0 + jnp.exp2(jnp.abs(z) * -LOG2E))
            ls_pos = jnp.minimum(z, 0.0) - sp
            ls_neg = ls_pos - z
            if mask is not None:
                ls_neg = jnp.where(mask, ls_neg, 0.0)
            pending[i] = (ls_pos, ls_neg.astype(BF16))

        def cumsum(i):
            h, _ = items[i]
            ls_pos, log_rest = pending[i]
            t = jnp.dot(after, log_rest, preferred_element_type=F32)
            a = jnp.exp2((ls_pos + t[:bq]) * LOG2E)
            if mask is not None:
                a = jnp.where(mask, a, 0.0)
            pending[i] = (a.astype(BF16), jnp.exp2(rests[h] * LOG2E))
            rests[h] = rests[h] + t[bq:bq + 1]

        def value(i):
            h, kb = items[i]
            a, scale = pending.pop(i)
            vt = vt_ref[h, :, pl.ds(pl.multiple_of(kb * bq, bq), bq)]
            accs[h] = accs[h] + scale * _value_t((vt[:HEAD_DIM], vt[HEAD_DIM:]), a)

        for i in range(len(items) + 2):
            if i < len(items):
                score(i)
            if 1 <= i <= len(items):
                cumsum(i - 1)
            if i >= 2:
                value(i - 2)
        return tuple((rests[h], accs[h]) for h in range(N_CB))

    zero = (jnp.zeros((1, 2 * bq), F32), jnp.zeros((HEAD_DIM, 2 * bq), F32))
    carries = step((qi,), (zero,) * N_CB, strict)
    carries = _pair_loop(qi, lambda i: qi - 1 - i, step, carries, (rest_ref, acc_ref))
    for h in range(N_CB):
        acc = carries[h][1]
        _store_gated(o_ref, gate_ref, h, jnp.concatenate([acc[:, :bq], acc[:, bq:]], axis=0).T)


def _stick_attention(proj4, vt, bq):
    _, b, s, _ = proj4.shape
    return pl.pallas_call(
        _stick_kernel,
        grid=(b, s // bq),
        in_specs=[
            _q_spec(bq, CB_SQ), _kv_spec(s, CB_SK),
            pl.BlockSpec((N_CB, LANES, s), lambda b, i: (VT_S, 0, b)),
            _gate_spec(bq, 1),
        ],
        out_specs=_out_spec(bq),
        out_shape=jax.ShapeDtypeStruct((b, s, GROUP_W), BF16),
        scratch_shapes=[pltpu.VMEM((N_CB, 1, 2 * bq), F32),
                        pltpu.VMEM((N_CB, HEAD_DIM, 2 * bq), F32)],
        compiler_params=_params(2),
        name="stick_attention",
    )(proj4, proj4, vt, proj4)


def _fox_kernel(q_ref, k_ref, vt_ref, gate_ref, negc_ref, o_ref, m_ref, acc_ref):
    bq = q_ref.shape[2]
    qi = pl.program_id(1)
    key = lax.broadcasted_iota(jnp.int32, (bq, 2 * bq), 0)
    query = lax.broadcasted_iota(jnp.int32, (bq, 2 * bq), 1) & (bq - 1)
    causal = key <= query

    lane = lax.broadcasted_iota(jnp.int32, (1, LANES), 1)
    low = lane < HEAD_DIM
    ones_low = jnp.where(lane < 3, 1.0, 0.0).astype(BF16)
    ones_high = jnp.where(low | (lane >= HEAD_DIM + 3), 0.0, 1.0).astype(BF16)
    qs = []
    for h in range(N_CB):
        q = q_ref[h, 0] * (HEAD_DIM ** -0.5)
        qs.append((jnp.where(low, q, ones_high), jnp.where(low, ones_low, q)))

    def item(h, kb, mask, width=1):
        rows = width * bq

        def score():
            start = pl.multiple_of(kb * bq, bq)
            k = k_ref[h, 0, pl.ds(start, rows), :]
            c = negc_ref[h, 0, pl.ds(start, rows), :]
            s = jnp.concatenate(
                [lax.dot_general(jnp.where(low, k, c), qs[h][0], NT_DIMS,
                                 preferred_element_type=F32),
                 lax.dot_general(jnp.where(low, c, k), qs[h][1], NT_DIMS,
                                 preferred_element_type=F32)], axis=1)
            return s if mask is None else jnp.where(mask, s, NEG)

        def values():
            vt = vt_ref[h, :, pl.ds(pl.multiple_of(kb * bq, bq), rows)]
            return _with_ones_rows(vt[:HEAD_DIM]), _with_ones_rows(vt[HEAD_DIM:])
        return h, score, values

    def step(kbs, carries, mask=None):
        if len(kbs) % 2 == 0:
            items = [item(h, kb, mask, 2) for kb in kbs[::2] for h in range(N_CB)]
        else:
            items = [item(h, kb, mask) for kb in kbs for h in range(N_CB)]
        return _softmax_t_blocks(items, carries)

    carries = _pair_loop(qi, lambda i: i, step, step((qi,), None, causal), (m_ref, acc_ref))
    for h in range(N_CB):
        a = _normalized_t(carries[h][1])
        _store_gated(o_ref, gate_ref, h, jnp.concatenate([a[:, :bq], a[:, bq:]], axis=0).T)


def _fox_attention(proj4, vt, negc, bq):
    _, b, s, _ = proj4.shape
    return pl.pallas_call(
        _fox_kernel,
        grid=(b, s // bq),
        in_specs=[
            _q_spec(bq, CB_FQ), _kv_spec(s, CB_FK),
            pl.BlockSpec((N_CB, LANES, s), lambda b, i: (VT_F, 0, b)),
            _gate_spec(bq, 2),
            pl.BlockSpec((N_CB, 1, s, LANES), lambda b, i: (0, b, 0, 0)),
        ],
        out_specs=_out_spec(bq),
        out_shape=jax.ShapeDtypeStruct((b, s, GROUP_W), BF16),
        scratch_shapes=[pltpu.VMEM((N_CB, 1, 2 * bq), F32),
                        pltpu.VMEM((N_CB, HEAD_DIM + BF16_ROWS, 2 * bq), F32)],
        compiler_params=_params(2),
        name="fox_attention",
    )(proj4, proj4, vt, proj4, negc)


def _mem_kernel(q_ref, km_ref, vmt_ref, gate_ref, o_ref):
    scale = LANES ** -0.5
    scores = [lax.dot_general(km_ref[h, 0], q_ref[h, 0], NT_DIMS,
                              preferred_element_type=F32) * scale for h in range(N_CB)]
    probs = [jnp.exp(s - jnp.max(s, axis=0, keepdims=True)).astype(BF16) for s in scores]
    for h in range(N_CB):
        acc = jnp.dot(_with_ones_rows(vmt_ref[h]), probs[h], preferred_element_type=F32)
        _store_gated(o_ref, gate_ref, h, _normalized_t(acc).T)


def _mem_attention(proj4, km4, vmt, bq):
    _, b, s, _ = proj4.shape
    mlen = km4.shape[2]
    return pl.pallas_call(
        _mem_kernel,
        grid=(b, s // bq),
        in_specs=[
            _q_spec(bq, CB_MQ),
            pl.BlockSpec((N_CB, 1, mlen, LANES), lambda b, i: (0, b, 0, 0)),
            pl.BlockSpec((N_CB, LANES, mlen), lambda b, i: (0, 0, b)),
            _gate_spec(bq, 3),
        ],
        out_specs=_out_spec(bq),
        out_shape=jax.ShapeDtypeStruct((b, s, GROUP_W), BF16),
        compiler_params=_params(2),
        name="mem_attention",
    )(proj4, km4, vmt, proj4)


def _out_proj_kernel(ya_ref, yb_ref, yc_ref, ym_ref, w_ref, g_ref, x_ref, o_ref):
    y = jnp.concatenate([ya_ref[...], yb_ref[...], yc_ref[...], ym_ref[...]], axis=-1)
    z = jnp.dot(y, w_ref[...], preferred_element_type=F32)
    z = z * lax.rsqrt(jnp.mean(z * z, axis=-1, keepdims=True) + EPS)
    o_ref[...] = x_ref[...] + z * g_ref[...]


def _out_proj(ys, w, g, x2d, tm):
    m, d = x2d.shape
    y_spec = pl.BlockSpec((tm, GROUP_W), lambda i: (i, 0))
    return pl.pallas_call(
        _out_proj_kernel,
        grid=(m // tm,),
        in_specs=[y_spec] * 4 + [
            pl.BlockSpec(w.shape, lambda i: (0, 0)),
            pl.BlockSpec((1, d), lambda i: (0, 0)),
            pl.BlockSpec((tm, d), lambda i: (i, 0)),
        ],
        out_specs=pl.BlockSpec((tm, d), lambda i: (i, 0)),
        out_shape=jax.ShapeDtypeStruct((m, d), F32),
        compiler_params=_params(1),
        name="out_proj",
    )(*ys, w, g, x2d)


def _pick(n, cands):
    for c in cands:
        if n % c == 0:
            return c
    raise ValueError(f"no tile in {cands} divides {n}")


def kernel(x, mem, w_in, b_forget, w_mem_kv, w_out, pre_norm, post_norm, mem_norm,
           diff_lambda, diff_head_norm, rel_bias):
    b, s, d = x.shape
    mlen = mem.shape[1]
    depth = w_in.shape[0]
    m = b * s
    bq = _pick(s, (256, 128))
    tm = _pick(m, (512, 256))

    gw = GROUP_W
    logit_lo, logit_hi = 9 * gw, 9 * gw + N_FOX_HEADS
    tiles = _bias_tiles(rel_bias, 2 * bq, bq)
    mem2d = mem.reshape(b * mlen, d)
    x2d = x.reshape(m, d)

    for l in range(depth):
        wl = w_in[l]
        ws = [wl[:, lo:hi].astype(BF16) for lo, hi in
              ((0, 2 * gw), (3 * gw, 5 * gw), (6 * gw, 8 * gw), (logit_hi, wl.shape[1]))]
        wvts = [wl[:, lo:lo + gw].T.astype(BF16) for lo in (2 * gw, 5 * gw, 8 * gw)]
        w_logit = jnp.pad(wl[:, logit_lo:logit_hi],
                          ((0, 0), (0, LANES - N_FOX_HEADS))).astype(BF16)
        b_pad = jnp.pad(b_forget[l], (0, LANES - N_FOX_HEADS)).reshape(1, LANES)

        proj, logit, vt = _in_proj(x2d, pre_norm[l].reshape(1, d), ws, w_logit, wvts, tm)
        proj4 = proj.reshape(N_PROJ_CB, b, s, LANES)
        negc = _forget_cumsum(logit.reshape(b, s, LANES), b_pad, _pick(s, (512, 256, 128)))

        w_kv = w_mem_kv[l].astype(BF16)
        km, vmt = _mem_kv(mem2d, mem_norm[l].reshape(1, d), w_kv[:, :gw], w_kv[:, gw:].T,
                          _pick(b * mlen, (1024, 512, 256)))
        km4 = km.reshape(N_CB, b, mlen, LANES)

        lam_init = 0.8 - 0.6 * math.exp(-0.3 * l)
        y_a = _diff_attention(proj4, vt, tiles, diff_lambda[l],
                              diff_head_norm[l].reshape(LANES, 1), lam_init, bq)
        y_b = _stick_attention(proj4, vt, bq)
        y_c = _fox_attention(proj4, vt, negc, bq)
        y_m = _mem_attention(proj4, km4, vmt, _pick(s, (1024, 512, 256, 128)))
        ys = [y.reshape(m, GROUP_W) for y in (y_a, y_b, y_c, y_m)]
        x2d = _out_proj(ys, w_out[l].astype(BF16), post_norm[l].reshape(1, d), x2d,
                        _pick(m, (1024, 512, 256)))
    return x2d.reshape(b, s, d)
```

```python
import functools
import math

import jax
import jax.numpy as jnp
import numpy as np
from jax import lax
from jax.experimental import pallas as pl
from jax.experimental.pallas import tpu as pltpu

F32 = jnp.float32
BF16 = jnp.bfloat16

LANES = 128
BF16_ROWS = 16
HEAD_DIM = 64
GROUP_W = 512
N_CB = GROUP_W // LANES
N_FOX_HEADS = GROUP_W // HEAD_DIM
NUM_BUCKETS = 32
MAX_DISTANCE = 128
EPS = 1e-6
NEG = -1e30
LOG2E = 1.4426950408889634
VMEM_LIMIT = 56 * 1024 * 1024

CB_AQ, CB_AK = 0, 4
CB_SQ, CB_SK = 8, 12
CB_FQ, CB_FK = 16, 20
CB_MQ, CB_GATE = 24, 28
N_PROJ_CB = 44
VT_A, VT_S, VT_F = 0, 1, 2

NT_DIMS = (((1,), (1,)), ((), ()))


def _params(n_axes):
    return pltpu.CompilerParams(dimension_semantics=("arbitrary",) * n_axes,
                                vmem_limit_bytes=VMEM_LIMIT)


def _split3(x):
    hi = x.astype(BF16)
    r = x - hi.astype(F32)
    mid = r.astype(BF16)
    lo = (r - mid.astype(F32)).astype(BF16)
    return hi, mid, lo


def _silu(g):
    g = g.astype(F32)
    return g / (1.0 + jnp.exp(-g))


def _rms_rows(x, g_ref):
    y = x * lax.rsqrt(jnp.mean(x * x, axis=-1, keepdims=True) + EPS)
    return (y * g_ref[...]).astype(BF16)


def _store_col_blocks(o_ref, res):
    for c in range(o_ref.shape[0]):
        o_ref[c] = res[:, c * LANES:(c + 1) * LANES]


MAX_DOT_COLUMNS = 1536


def _in_proj_kernel(n_w, x0_ref, xn_ref, g_ref, wl_ref, *refs):
    w_refs, wvt_refs = refs[:n_w], refs[n_w:-4]
    o_ref, ol_ref, ovt_ref, h_ref = refs[-4:]
    i = pl.program_id(0)

    @pl.when(i == 0)
    def _():
        h_ref[0] = _rms_rows(x0_ref[...], g_ref)

    slot = lax.rem(i, 2)
    h_ref[1 - slot] = _rms_rows(xn_ref[...], g_ref)
    h = h_ref[slot]
    ol_ref[...] = jnp.dot(h, wl_ref[...], preferred_element_type=F32)
    block = 0
    for wvt_ref in wvt_refs:
        vt = lax.dot_general(wvt_ref[...], h, NT_DIMS, preferred_element_type=F32).astype(BF16)
        for r in range(vt.shape[0] // LANES):
            ovt_ref[block + r] = vt[r * LANES:(r + 1) * LANES, :]
        block += vt.shape[0] // LANES
    block = 0
    for w_ref in w_refs:
        width = w_ref.shape[1]
        n_chunks = -(-width // MAX_DOT_COLUMNS)
        chunk = width // n_chunks
        for c in range(n_chunks):
            res = jnp.dot(h, w_ref[:, c * chunk:(c + 1) * chunk],
                          preferred_element_type=F32).astype(BF16)
            for r in range(chunk // LANES):
                o_ref[block + r] = res[:, r * LANES:(r + 1) * LANES]
            block += chunk // LANES


def _in_proj(x2d, g, ws, wl, wvts, tm):
    m, k = x2d.shape
    n = sum(w.shape[1] for w in ws)
    nv = sum(w.shape[0] for w in wvts)
    last = m // tm - 1
    resident = pl.Buffered(1)
    return pl.pallas_call(
        functools.partial(_in_proj_kernel, len(ws)),
        grid=(m // tm,),
        in_specs=[
            pl.BlockSpec((tm, k), lambda i: (0, 0), pipeline_mode=resident),
            pl.BlockSpec((tm, k), lambda i: (jnp.minimum(i + 1, last), 0)),
            pl.BlockSpec((1, k), lambda i: (0, 0)),
            pl.BlockSpec((k, LANES), lambda i: (0, 0)),
        ] + [pl.BlockSpec(w.shape, lambda i: (0, 0), pipeline_mode=resident) for w in ws + wvts],
        out_specs=[
            pl.BlockSpec((n // LANES, tm, LANES), lambda i: (0, i, 0)),
            pl.BlockSpec((tm, LANES), lambda i: (i, 0)),
            pl.BlockSpec((nv // LANES, LANES, tm), lambda i: (0, 0, i)),
        ],
        out_shape=[
            jax.ShapeDtypeStruct((n // LANES, m, LANES), BF16),
            jax.ShapeDtypeStruct((m, LANES), F32),
            jax.ShapeDtypeStruct((nv // LANES, LANES, m), BF16),
        ],
        scratch_shapes=[pltpu.VMEM((2, tm, k), BF16)],
        compiler_params=_params(1),
        name="in_proj",
    )(x2d, x2d, g, wl, *ws, *wvts)


def _mem_kv_kernel(x_ref, g_ref, wk_ref, wvt_ref, ok_ref, ovt_ref):
    h = _rms_rows(x_ref[...], g_ref)
    _store_col_blocks(ok_ref, jnp.dot(h, wk_ref[...], preferred_element_type=F32).astype(BF16))
    vt = lax.dot_general(wvt_ref[...], h, NT_DIMS, preferred_element_type=F32).astype(BF16)
    for c in range(ovt_ref.shape[0]):
        ovt_ref[c] = vt[c * LANES:(c + 1) * LANES, :]


def _mem_kv(x2d, g, wk, wvt, tm):
    m, k = x2d.shape
    n = wk.shape[1]
    return pl.pallas_call(
        _mem_kv_kernel,
        grid=(m // tm,),
        in_specs=[
            pl.BlockSpec((tm, k), lambda i: (i, 0)),
            pl.BlockSpec((1, k), lambda i: (0, 0)),
            pl.BlockSpec((k, n), lambda i: (0, 0)),
            pl.BlockSpec((n, k), lambda i: (0, 0)),
        ],
        out_specs=[
            pl.BlockSpec((n // LANES, tm, LANES), lambda i: (0, i, 0)),
            pl.BlockSpec((n // LANES, LANES, tm), lambda i: (0, 0, i)),
        ],
        out_shape=[
            jax.ShapeDtypeStruct((n // LANES, m, LANES), BF16),
            jax.ShapeDtypeStruct((n // LANES, LANES, m), BF16),
        ],
        compiler_params=_params(1),
        name="mem_kv",
    )(x2d, g, wk, wvt)


def _forget_cumsum_kernel(logit_ref, b_ref, tri_ref, place_ref, o_ref, carry_ref):
    @pl.when(pl.program_id(1) == 0)
    def _():
        carry_ref[...] = jnp.zeros_like(carry_ref)

    n_seq, tb = logit_ref.shape[0], logit_ref.shape[1]
    sums = []
    for r in range(n_seq):
        z = logit_ref[r] + b_ref[...]
        log_f = jnp.minimum(z, 0.0) - jnp.log1p(jnp.exp(-jnp.abs(z)))
        sums.append(jnp.dot(tri_ref[...], jnp.concatenate(_split3(log_f), axis=1),
                            preferred_element_type=F32))
    exts = []
    for r in range(n_seq):
        c = carry_ref[r] + (sums[r][:, :LANES] + sums[r][:, LANES:2 * LANES]
                            + sums[r][:, 2 * LANES:])
        carry_ref[r] = c[tb - 1:tb, :]
        exts.append(jnp.dot(jnp.concatenate(_split3(-c), axis=1), place_ref[...],
                            preferred_element_type=F32))
    for r in range(n_seq):
        for h in range(N_CB):
            o_ref[h, r] = exts[r][:, h * LANES:(h + 1) * LANES].astype(BF16)


def _placement():
    place = np.zeros((3 * LANES, N_CB * LANES), np.float32)
    for term in range(3):
        for h in range(N_CB):
            place[term * LANES + 2 * h, h * LANES + HEAD_DIM + term] = 1.0
            place[term * LANES + 2 * h + 1, h * LANES + term] = 1.0
    return jnp.asarray(place, BF16)


def _forget_cumsum(logit3, b_pad, tb):
    b, s, _ = logit3.shape
    n_seq = _pick(b, (4, 2, 1))
    tri = jnp.asarray(np.tril(np.ones((tb, tb), np.float32)), BF16)
    place = _placement()
    return pl.pallas_call(
        _forget_cumsum_kernel,
        grid=(b // n_seq, s // tb),
        in_specs=[
            pl.BlockSpec((n_seq, tb, LANES), lambda i, j: (i, j, 0)),
            pl.BlockSpec((1, LANES), lambda i, j: (0, 0)),
            pl.BlockSpec((tb, tb), lambda i, j: (0, 0)),
            pl.BlockSpec(place.shape, lambda i, j: (0, 0)),
        ],
        out_specs=pl.BlockSpec((N_CB, n_seq, tb, LANES), lambda i, j: (0, i, j, 0)),
        out_shape=jax.ShapeDtypeStruct((N_CB, b, s, LANES), BF16),
        scratch_shapes=[pltpu.VMEM((n_seq, 1, LANES), F32)],
        compiler_params=_params(2),
        name="forget_cumsum",
    )(logit3, b_pad, tri, place)


def _t5_buckets(bk, bq):
    key = np.arange(bk, dtype=np.int32)[:, None]
    query = np.arange(bq, dtype=np.int32)[None, :]
    dist = np.stack([query - key, query - key + bq])
    n = np.maximum(dist, 0)
    max_exact = NUM_BUCKETS // 2
    nf = np.maximum(n, 1).astype(np.float32)
    large = max_exact + (np.log(nf / np.float32(max_exact))
                         / np.float32(math.log(MAX_DISTANCE / max_exact))
                         * np.float32(NUM_BUCKETS - max_exact)).astype(np.int32)
    large = np.minimum(large, NUM_BUCKETS - 1)
    bucket = np.where(n < max_exact, n, large)
    return jnp.asarray(np.where(dist >= 0, bucket, -1), jnp.int32)


def _bias_tiles_kernel(rb_ref, bucket_ref, o_ref):
    h = pl.program_id(0)
    bucket = bucket_ref[0]
    far = rb_ref[NUM_BUCKETS - 1, h]
    bias = jnp.zeros(bucket.shape, F32)
    for b in range(NUM_BUCKETS - 1):
        bias = jnp.where(bucket == b, rb_ref[b, h] - far, bias)
    o_ref[0, 0] = jnp.where(bucket >= 0, bias, NEG)


def _bias_tiles(rel_bias, bk, bq):
    n_heads = rel_bias.shape[1]
    return pl.pallas_call(
        _bias_tiles_kernel,
        grid=(n_heads, 2),
        in_specs=[pl.BlockSpec(memory_space=pltpu.SMEM),
                  pl.BlockSpec((1, bk, bq), lambda h, j: (j, 0, 0))],
        out_specs=pl.BlockSpec((1, 1, bk, bq), lambda h, j: (h, j, 0, 0)),
        out_shape=jax.ShapeDtypeStruct((n_heads, 2, bk, bq), F32),
        compiler_params=_params(2),
        name="bias_tiles",
    )(rel_bias, _t5_buckets(bk, bq))


def _stack_halves(q):
    lane = lax.broadcasted_iota(jnp.int32, q.shape, 1)
    zero = jnp.zeros_like(q)
    return jnp.concatenate([jnp.where(lane < HEAD_DIM, q, zero),
                            jnp.where(lane < HEAD_DIM, zero, q)], axis=0)


def _key_block(ref, h, kb, bk):
    return ref[h, 0, pl.ds(pl.multiple_of(kb * bk, bk), bk), :]


def _q_spec(bq, cb):
    return pl.BlockSpec((N_CB, 1, bq, LANES), lambda b, i: (cb // N_CB, b, i, 0))


def _kv_spec(s, cb):
    return pl.BlockSpec((N_CB, 1, s, LANES), lambda b, i: (cb // N_CB, b, 0, 0))


def _gate_spec(bq, group):
    return pl.BlockSpec((N_CB, 1, bq, LANES), lambda b, i: (CB_GATE // N_CB + group, b, i, 0))


def _out_spec(bq):
    return pl.BlockSpec((1, bq, GROUP_W), lambda b, i: (b, i, 0))


def _store_gated(o_ref, gate_ref, h, o):
    o_ref[0, :, h * LANES:(h + 1) * LANES] = (o * _silu(gate_ref[h, 0])).astype(o_ref.dtype)


def _with_ones_rows(vt):
    return jnp.concatenate([vt, jnp.ones((BF16_ROWS, vt.shape[1]), BF16)], axis=0)


def _value_t(vts, p):
    n = p.shape[1] // 2
    return jnp.concatenate([jnp.dot(vts[0], p[:, :n], preferred_element_type=F32),
                            jnp.dot(vts[1], p[:, n:], preferred_element_type=F32)], axis=1)


def _scores_t(q_st, k_ref, h, kb, bk):
    return lax.dot_general(_key_block(k_ref, h, kb, bk), q_st, NT_DIMS,
                           preferred_element_type=F32)


VALUE_LAG = 2


def _softmax_t_blocks(items, carries):
    ms = [None if carries is None else carries[h][0] for h in range(N_CB)]
    accs = [None if carries is None else carries[h][1] for h in range(N_CB)]
    pending = {}

    def score(i):
        h, score_fn, _ = items[i]
        s = score_fn()
        m = jnp.max(s, axis=0, keepdims=True)
        m_new = m if ms[h] is None else jnp.maximum(ms[h], m)
        pending[i] = (ms[h], m_new, jnp.exp(s - m_new).astype(BF16))
        ms[h] = m_new

    def value(i):
        h, _, vt_fn = items[i]
        m_old, m_new, p = pending.pop(i)
        pv = _value_t(vt_fn(), p)
        accs[h] = pv if accs[h] is None else jnp.exp(m_old - m_new) * accs[h] + pv

    for i in range(len(items) + VALUE_LAG):
        if i < len(items):
            score(i)
        if i >= VALUE_LAG:
            value(i - VALUE_LAG)
    return tuple((ms[h], accs[h]) for h in range(N_CB))


LOOP_BLOCKS = 4


def _store_state(state_refs, carries):
    for h, chain in enumerate(carries):
        for ref, value in zip(state_refs, chain):
            ref[h] = value


def _pair_loop(n_blocks, block_of, step, carries, state_refs, always_even=False):
    def load():
        return tuple(tuple(ref[h] for ref in state_refs) for h in range(N_CB))

    def run(first, width):
        _store_state(state_refs, step(tuple(block_of(first + j) for j in range(width)), load()))

    if carries is not None:
        _store_state(state_refs, carries)

    def trip(i, _):
        run(LOOP_BLOCKS * i, LOOP_BLOCKS)
        return 0

    lax.fori_loop(0, n_blocks // LOOP_BLOCKS, trip, 0)
    done = n_blocks - n_blocks % LOOP_BLOCKS

    @pl.when(n_blocks - done >= 2)
    def _():
        run(done, 2)

    if not always_even:
        @pl.when(n_blocks % 2 == 1)
        def _():
            run(n_blocks - 1, 1)

    return load()


def _normalized_t(acc):
    d = acc.shape[0] - BF16_ROWS
    return acc[:d] * (1.0 / acc[d:d + 1])


def _diff_kernel(lam_init, q_ref, k_ref, vt_ref, gate_ref, tile_ref, lamp_ref, hn_ref, o_ref,
                 m_ref, acc_ref):
    bq = q_ref.shape[2]
    qi = pl.program_id(1)
    qs = [_stack_halves(q_ref[h, 0] * (HEAD_DIM ** -0.5)) for h in range(N_CB)]

    def item(h, kb, tile_fn=None, width=1):
        rows = width * bq

        def score():
            k = k_ref[h, 0, pl.ds(pl.multiple_of(kb * bq, bq), rows), :]
            s = lax.dot_general(k, qs[h], NT_DIMS, preferred_element_type=F32)
            if tile_fn is None:
                return s
            tile = tile_fn(h)
            return s + jnp.concatenate([tile, tile], axis=1)

        def values():
            vt = _with_ones_rows(vt_ref[h, :, pl.ds(pl.multiple_of(kb * bq, bq), rows)])
            return vt, vt
        return h, score, values

    def step(kbs, carries):
        if len(kbs) % 2 == 0:
            items = [item(h, kb, None, 2) for kb in kbs[::2] for h in range(N_CB)]
        else:
            items = [item(h, kb) for kb in kbs for h in range(N_CB)]
        return _softmax_t_blocks(items, carries)

    near = [item(h, jnp.maximum(qi - 1, 0), lambda h: tile_ref[h, jnp.minimum(qi, 1)], 2)
            for h in range(N_CB)]
    state = (m_ref, acc_ref)
    n_far = jnp.maximum(qi - 1, 0)
    odd = n_far % 2

    @pl.when(odd == 1)
    def _():
        _store_state(state, _softmax_t_blocks([item(h, qi - 2) for h in range(N_CB)] + near, None))

    @pl.when(odd == 0)
    def _():
        _store_state(state, _softmax_t_blocks(near, None))

    carries = _pair_loop(n_far - odd, lambda i: i, step, None, state, always_even=True)

    lp = lamp_ref[...]
    lam = (jnp.exp(jnp.sum(lp[0:1] * lp[1:2], axis=-1, keepdims=True))
           - jnp.exp(jnp.sum(lp[2:3] * lp[3:4], axis=-1, keepdims=True)) + lam_init)
    head_gain = jnp.broadcast_to(hn_ref[...] * (1.0 - lam_init), (LANES, bq))
    for h in range(N_CB):
        a = _normalized_t(carries[h][1])
        o = a[:, :bq] - lam * a[:, bq:]
        o = o * lax.rsqrt(jnp.mean(o * o, axis=0, keepdims=True) + EPS)
        _store_gated(o_ref, gate_ref, h, (o * head_gain).T)


def _diff_attention(proj4, vt, tiles, lam_params, head_norm, lam_init, bq):
    _, b, s, _ = proj4.shape
    return pl.pallas_call(
        functools.partial(_diff_kernel, lam_init),
        grid=(b, s // bq),
        in_specs=[
            _q_spec(bq, CB_AQ), _kv_spec(s, CB_AK),
            pl.BlockSpec((N_CB, LANES, s), lambda b, i: (VT_A, 0, b)),
            _gate_spec(bq, 0),
            pl.BlockSpec((N_CB, 2, 2 * bq, bq), lambda b, i: (0, 0, 0, 0)),
            pl.BlockSpec((4, HEAD_DIM), lambda b, i: (0, 0)),
            pl.BlockSpec((LANES, 1), lambda b, i: (0, 0)),
        ],
        out_specs=_out_spec(bq),
        out_shape=jax.ShapeDtypeStruct((b, s, GROUP_W), BF16),
        scratch_shapes=[pltpu.VMEM((N_CB, 1, 2 * bq), F32),
                        pltpu.VMEM((N_CB, LANES + BF16_ROWS, 2 * bq), F32)],
        compiler_params=_params(2),
        name="diff_attention",
    )(proj4, proj4, vt, proj4, tiles, lam_params, head_norm)


def _stick_kernel(q_ref, k_ref, vt_ref, gate_ref, o_ref, rest_ref, acc_ref):
    bq = q_ref.shape[2]
    qi = pl.program_id(1)
    qs = [_stack_halves(q_ref[h, 0] * (HEAD_DIM ** -0.5)) for h in range(N_CB)]
    key = lax.broadcasted_iota(jnp.int32, (bq, 2 * bq), 0)
    query = lax.broadcasted_iota(jnp.int32, (bq, 2 * bq), 1) & (bq - 1)
    strict = key < query

    row = lax.broadcasted_iota(jnp.int32, (bq + BF16_ROWS, bq), 0)
    col = lax.broadcasted_iota(jnp.int32, (bq + BF16_ROWS, bq), 1)
    after = jnp.where((col > row) | (row >= bq), 1.0, 0.0).astype(BF16)

    def step(kbs, carries, mask=None):
        items = [(h, kb) for kb in kbs for h in range(N_CB)]
        rests = [carries[h][0] for h in range(N_CB)]
        accs = [carries[h][1] for h in range(N_CB)]
        pending = {}

        def score(i):
            h, kb = items[i]
            z = _scores_t(qs[h], k_ref, h, kb, bq)
            sp = jnp.log(1.0 + jnp.exp2(jnp.abs(z) * -LOG2E))
            ls_pos = jnp.minimum(z, 0.0) - sp
            ls_neg = ls_pos - z
            if mask is not None:
                ls_neg = jnp.where(mask, ls_neg, 0.0)
            pending[i] = (ls_pos, ls_neg.astype(BF16))

        def cumsum(i):
            h, _ = items[i]
            ls_pos, log_rest = pending[i]
            t = jnp.dot(after, log_rest, preferred_element_type=F32)
            a = jnp.exp2((ls_pos + t[:bq]) * LOG2E)
            if mask is not None:
                a = jnp.where(mask, a, 0.0)
            pending[i] = (a.astype(BF16), jnp.exp2(rests[h] * LOG2E))
            rests[h] = rests[h] + t[bq:bq + 1]

        def value(i):
            h, kb = items[i]
            a, scale = pending.pop(i)
            vt = vt_ref[h, :, pl.ds(pl.multiple_of(kb * bq, bq), bq)]
            accs[h] = accs[h] + scale * _value_t((vt[:HEAD_DIM], vt[HEAD_DIM:]), a)

        for i in range(len(items) + 2):
            if i < len(items):
                score(i)
            if 1 <= i <= len(items):
                cumsum(i - 1)
            if i >= 2:
                value(i - 2)
        return tuple((rests[h], accs[h]) for h in range(N_CB))

    zero = (jnp.zeros((1, 2 * bq), F32), jnp.zeros((HEAD_DIM, 2 * bq), F32))
    carries = step((qi,), (zero,) * N_CB, strict)
    carries = _pair_loop(qi, lambda i: qi - 1 - i, step, carries, (rest_ref, acc_ref))
    for h in range(N_CB):
        acc = carries[h][1]
        _store_gated(o_ref, gate_ref, h, jnp.concatenate([acc[:, :bq], acc[:, bq:]], axis=0).T)


def _stick_attention(proj4, vt, bq):
    _, b, s, _ = proj4.shape
    return pl.pallas_call(
        _stick_kernel,
        grid=(b, s // bq),
        in_specs=[
            _q_spec(bq, CB_SQ), _kv_spec(s, CB_SK),
            pl.BlockSpec((N_CB, LANES, s), lambda b, i: (VT_S, 0, b)),
            _gate_spec(bq, 1),
        ],
        out_specs=_out_spec(bq),
        out_shape=jax.ShapeDtypeStruct((b, s, GROUP_W), BF16),
        scratch_shapes=[pltpu.VMEM((N_CB, 1, 2 * bq), F32),
                        pltpu.VMEM((N_CB, HEAD_DIM, 2 * bq), F32)],
        compiler_params=_params(2),
        name="stick_attention",
    )(proj4, proj4, vt, proj4)


def _fox_kernel(q_ref, k_ref, vt_ref, gate_ref, negc_ref, o_ref, m_ref, acc_ref):
    bq = q_ref.shape[2]
    qi = pl.program_id(1)
    key = lax.broadcasted_iota(jnp.int32, (bq, 2 * bq), 0)
    query = lax.broadcasted_iota(jnp.int32, (bq, 2 * bq), 1) & (bq - 1)
    causal = key <= query

    lane = lax.broadcasted_iota(jnp.int32, (1, LANES), 1)
    low = lane < HEAD_DIM
    ones_low = jnp.where(lane < 3, 1.0, 0.0).astype(BF16)
    ones_high = jnp.where(low | (lane >= HEAD_DIM + 3), 0.0, 1.0).astype(BF16)
    qs = []
    for h in range(N_CB):
        q = q_ref[h, 0] * (HEAD_DIM ** -0.5)
        qs.append((jnp.where(low, q, ones_high), jnp.where(low, ones_low, q)))

    def item(h, kb, mask, width=1):
        rows = width * bq

        def score():
            start = pl.multiple_of(kb * bq, bq)
            k = k_ref[h, 0, pl.ds(start, rows), :]
            c = negc_ref[h, 0, pl.ds(start, rows), :]
            s = jnp.concatenate(
                [lax.dot_general(jnp.where(low, k, c), qs[h][0], NT_DIMS,
                                 preferred_element_type=F32),
                 lax.dot_general(jnp.where(low, c, k), qs[h][1], NT_DIMS,
                                 preferred_element_type=F32)], axis=1)
            return s if mask is None else jnp.where(mask, s, NEG)

        def values():
            vt = vt_ref[h, :, pl.ds(pl.multiple_of(kb * bq, bq), rows)]
            return _with_ones_rows(vt[:HEAD_DIM]), _with_ones_rows(vt[HEAD_DIM:])
        return h, score, values

    def step(kbs, carries, mask=None):
        if len(kbs) % 2 == 0:
            items = [item(h, kb, mask, 2) for kb in kbs[::2] for h in range(N_CB)]
        else:
            items = [item(h, kb, mask) for kb in kbs for h in range(N_CB)]
        return _softmax_t_blocks(items, carries)

    state = (m_ref, acc_ref)
    odd = qi % 2

    @pl.when(odd == 1)
    def _():
        key2 = lax.broadcasted_iota(jnp.int32, (2 * bq, 2 * bq), 0) - bq
        query2 = lax.broadcasted_iota(jnp.int32, (2 * bq, 2 * bq), 1) & (bq - 1)
        wide = [item(h, qi - 1, key2 <= query2, 2) for h in range(N_CB)]
        _store_state(state, _softmax_t_blocks(wide, None))

    @pl.when(odd == 0)
    def _():
        _store_state(state, step((qi,), None, causal))

    carries = _pair_loop(qi - odd, lambda i: i, step, None, state, always_even=True)
    for h in range(N_CB):
        a = _normalized_t(carries[h][1])
        _store_gated(o_ref, gate_ref, h, jnp.concatenate([a[:, :bq], a[:, bq:]], axis=0).T)


def _fox_attention(proj4, vt, negc, bq):
    _, b, s, _ = proj4.shape
    return pl.pallas_call(
        _fox_kernel,
        grid=(b, s // bq),
        in_specs=[
            _q_spec(bq, CB_FQ), _kv_spec(s, CB_FK),
            pl.BlockSpec((N_CB, LANES, s), lambda b, i: (VT_F, 0, b)),
            _gate_spec(bq, 2),
            pl.BlockSpec((N_CB, 1, s, LANES), lambda b, i: (0, b, 0, 0)),
        ],
        out_specs=_out_spec(bq),
        out_shape=jax.ShapeDtypeStruct((b, s, GROUP_W), BF16),
        scratch_shapes=[pltpu.VMEM((N_CB, 1, 2 * bq), F32),
                        pltpu.VMEM((N_CB, HEAD_DIM + BF16_ROWS, 2 * bq), F32)],
        compiler_params=_params(2),
        name="fox_attention",
    )(proj4, proj4, vt, proj4, negc)


def _mem_kernel(q_ref, km_ref, vmt_ref, gate_ref, o_ref):
    scale = LANES ** -0.5
    scores = [lax.dot_general(km_ref[h, 0], q_ref[h, 0], NT_DIMS,
                              preferred_element_type=F32) * scale for h in range(N_CB)]
    probs = [jnp.exp(s - jnp.max(s, axis=0, keepdims=True)).astype(BF16) for s in scores]
    for h in range(N_CB):
        acc = jnp.dot(_with_ones_rows(vmt_ref[h]), probs[h], preferred_element_type=F32)
        _store_gated(o_ref, gate_ref, h, _normalized_t(acc).T)


def _mem_attention(proj4, km4, vmt, bq):
    _, b, s, _ = proj4.shape
    mlen = km4.shape[2]
    return pl.pallas_call(
        _mem_kernel,
        grid=(b, s // bq),
        in_specs=[
            _q_spec(bq, CB_MQ),
            pl.BlockSpec((N_CB, 1, mlen, LANES), lambda b, i: (0, b, 0, 0)),
            pl.BlockSpec((N_CB, LANES, mlen), lambda b, i: (0, 0, b)),
            _gate_spec(bq, 3),
        ],
        out_specs=_out_spec(bq),
        out_shape=jax.ShapeDtypeStruct((b, s, GROUP_W), BF16),
        compiler_params=_params(2),
        name="mem_attention",
    )(proj4, km4, vmt, proj4)


def _out_proj_kernel(ya_ref, yb_ref, yc_ref, ym_ref, w_ref, g_ref, x_ref, o_ref):
    y = jnp.concatenate([ya_ref[...], yb_ref[...], yc_ref[...], ym_ref[...]], axis=-1)
    z = jnp.dot(y, w_ref[...], preferred_element_type=F32)
    z = z * lax.rsqrt(jnp.mean(z * z, axis=-1, keepdims=True) + EPS)
    o_ref[...] = x_ref[...] + z * g_ref[...]


def _out_proj(ys, w, g, x2d, tm):
    m, d = x2d.shape
    y_spec = pl.BlockSpec((tm, GROUP_W), lambda i: (i, 0))
    return pl.pallas_call(
        _out_proj_kernel,
        grid=(m // tm,),
        in_specs=[y_spec] * 4 + [
            pl.BlockSpec(w.shape, lambda i: (0, 0)),
            pl.BlockSpec((1, d), lambda i: (0, 0)),
            pl.BlockSpec((tm, d), lambda i: (i, 0)),
        ],
        out_specs=pl.BlockSpec((tm, d), lambda i: (i, 0)),
        out_shape=jax.ShapeDtypeStruct((m, d), F32),
        compiler_params=_params(1),
        name="out_proj",
    )(*ys, w, g, x2d)


def _pick(n, cands):
    for c in cands:
        if n % c == 0:
            return c
    raise ValueError(f"no tile in {cands} divides {n}")


def kernel(x, mem, w_in, b_forget, w_mem_kv, w_out, pre_norm, post_norm, mem_norm,
           diff_lambda, diff_head_norm, rel_bias):
    b, s, d = x.shape
    mlen = mem.shape[1]
    depth = w_in.shape[0]
    m = b * s
    bq = _pick(s, (256, 128))
    tm = _pick(m, (512, 256))

    gw = GROUP_W
    logit_lo, logit_hi = 9 * gw, 9 * gw + N_FOX_HEADS
    tiles = _bias_tiles(rel_bias, 2 * bq, bq)
    mem2d = mem.reshape(b * mlen, d)
    x2d = x.reshape(m, d)

    for l in range(depth):
        wl = w_in[l]
        ws = [wl[:, lo:hi].astype(BF16) for lo, hi in
              ((0, 2 * gw), (3 * gw, 5 * gw), (6 * gw, 8 * gw), (logit_hi, wl.shape[1]))]
        wvts = [wl[:, lo:lo + gw].T.astype(BF16) for lo in (2 * gw, 5 * gw, 8 * gw)]
        w_logit = jnp.pad(wl[:, logit_lo:logit_hi],
                          ((0, 0), (0, LANES - N_FOX_HEADS))).astype(BF16)
        b_pad = jnp.pad(b_forget[l], (0, LANES - N_FOX_HEADS)).reshape(1, LANES)

        proj, logit, vt = _in_proj(x2d, pre_norm[l].reshape(1, d), ws, w_logit, wvts, tm)
        proj4 = proj.reshape(N_PROJ_CB, b, s, LANES)
        negc = _forget_cumsum(logit.reshape(b, s, LANES), b_pad, _pick(s, (512, 256, 128)))

        w_kv = w_mem_kv[l].astype(BF16)
        km, vmt = _mem_kv(mem2d, mem_norm[l].reshape(1, d), w_kv[:, :gw], w_kv[:, gw:].T,
                          _pick(b * mlen, (1024, 512, 256)))
        km4 = km.reshape(N_CB, b, mlen, LANES)

        lam_init = 0.8 - 0.6 * math.exp(-0.3 * l)
        y_a = _diff_attention(proj4, vt, tiles, diff_lambda[l],
                              diff_head_norm[l].reshape(LANES, 1), lam_init, bq)
        y_b = _stick_attention(proj4, vt, bq)
        y_c = _fox_attention(proj4, vt, negc, bq)
        y_m = _mem_attention(proj4, km4, vmt, _pick(s, (1024, 512, 256, 128)))
        ys = [y.reshape(m, GROUP_W) for y in (y_a, y_b, y_c, y_m)]
        x2d = _out_proj(ys, w_out[l].astype(BF16), post_norm[l].reshape(1, d), x2d,
                        _pick(m, (1024, 512, 256)))
    return x2d.reshape(b, s, d)
```

```python
import functools
import math

import jax
import jax.numpy as jnp
import numpy as np
from jax import lax
from jax.experimental import pallas as pl
from jax.experimental.pallas import tpu as pltpu

F32 = jnp.float32
BF16 = jnp.bfloat16

LANES = 128
BF16_ROWS = 16
HEAD_DIM = 64
GROUP_W = 512
N_CB = GROUP_W // LANES
N_FOX_HEADS = GROUP_W // HEAD_DIM
NUM_BUCKETS = 32
MAX_DISTANCE = 128
EPS = 1e-6
NEG = -1e30
LOG2E = 1.4426950408889634
VMEM_LIMIT = 56 * 1024 * 1024

CB_AQ, CB_AK = 0, 4
CB_SQ, CB_SK = 8, 12
CB_FQ, CB_FK = 16, 20
CB_MQ, CB_GATE = 24, 28
N_PROJ_CB = 44
VT_A, VT_S, VT_F = 0, 1, 2

NT_DIMS = (((1,), (1,)), ((), ()))


def _params(n_axes):
    return pltpu.CompilerParams(dimension_semantics=("arbitrary",) * n_axes,
                                vmem_limit_bytes=VMEM_LIMIT)


def _split3(x):
    hi = x.astype(BF16)
    r = x - hi.astype(F32)
    mid = r.astype(BF16)
    lo = (r - mid.astype(F32)).astype(BF16)
    return hi, mid, lo


def _silu(g):
    g = g.astype(F32)
    return g / (1.0 + jnp.exp(-g))


def _rms_rows(x, g_ref):
    y = x * lax.rsqrt(jnp.mean(x * x, axis=-1, keepdims=True) + EPS)
    return (y * g_ref[...]).astype(BF16)


def _store_col_blocks(o_ref, res):
    for c in range(o_ref.shape[0]):
        o_ref[c] = res[:, c * LANES:(c + 1) * LANES]


MAX_DOT_COLUMNS = 1536


def _in_proj_kernel(n_w, x0_ref, xn_ref, g_ref, wl_ref, *refs):
    w_refs, wvt_refs = refs[:n_w], refs[n_w:-4]
    o_ref, ol_ref, ovt_ref, h_ref = refs[-4:]
    i = pl.program_id(0)

    @pl.when(i == 0)
    def _():
        h_ref[0] = _rms_rows(x0_ref[...], g_ref)

    slot = lax.rem(i, 2)
    h_ref[1 - slot] = _rms_rows(xn_ref[...], g_ref)
    h = h_ref[slot]
    ol_ref[...] = jnp.dot(h, wl_ref[...], preferred_element_type=F32)
    block = 0
    for wvt_ref in wvt_refs:
        vt = lax.dot_general(wvt_ref[...], h, NT_DIMS, preferred_element_type=F32).astype(BF16)
        for r in range(vt.shape[0] // LANES):
            ovt_ref[block + r] = vt[r * LANES:(r + 1) * LANES, :]
        block += vt.shape[0] // LANES
    block = 0
    for w_ref in w_refs:
        width = w_ref.shape[1]
        n_chunks = -(-width // MAX_DOT_COLUMNS)
        chunk = width // n_chunks
        for c in range(n_chunks):
            res = jnp.dot(h, w_ref[:, c * chunk:(c + 1) * chunk],
                          preferred_element_type=F32).astype(BF16)
            for r in range(chunk // LANES):
                o_ref[block + r] = res[:, r * LANES:(r + 1) * LANES]
            block += chunk // LANES


def _in_proj(x2d, g, ws, wl, wvts, tm):
    m, k = x2d.shape
    n = sum(w.shape[1] for w in ws)
    nv = sum(w.shape[0] for w in wvts)
    last = m // tm - 1
    resident = pl.Buffered(1)
    return pl.pallas_call(
        functools.partial(_in_proj_kernel, len(ws)),
        grid=(m // tm,),
        in_specs=[
            pl.BlockSpec((tm, k), lambda i: (0, 0), pipeline_mode=resident),
            pl.BlockSpec((tm, k), lambda i: (jnp.minimum(i + 1, last), 0)),
            pl.BlockSpec((1, k), lambda i: (0, 0)),
            pl.BlockSpec((k, LANES), lambda i: (0, 0)),
        ] + [pl.BlockSpec(w.shape, lambda i: (0, 0), pipeline_mode=resident) for w in ws + wvts],
        out_specs=[
            pl.BlockSpec((n // LANES, tm, LANES), lambda i: (0, i, 0)),
            pl.BlockSpec((tm, LANES), lambda i: (i, 0)),
            pl.BlockSpec((nv // LANES, LANES, tm), lambda i: (0, 0, i)),
        ],
        out_shape=[
            jax.ShapeDtypeStruct((n // LANES, m, LANES), BF16),
            jax.ShapeDtypeStruct((m, LANES), F32),
            jax.ShapeDtypeStruct((nv // LANES, LANES, m), BF16),
        ],
        scratch_shapes=[pltpu.VMEM((2, tm, k), BF16)],
        compiler_params=_params(1),
        name="in_proj",
    )(x2d, x2d, g, wl, *ws, *wvts)


def _mem_kv_kernel(x_ref, g_ref, wk_ref, wvt_ref, ok_ref, ovt_ref):
    h = _rms_rows(x_ref[...], g_ref)
    _store_col_blocks(ok_ref, jnp.dot(h, wk_ref[...], preferred_element_type=F32).astype(BF16))
    vt = lax.dot_general(wvt_ref[...], h, NT_DIMS, preferred_element_type=F32).astype(BF16)
    for c in range(ovt_ref.shape[0]):
        ovt_ref[c] = vt[c * LANES:(c + 1) * LANES, :]


def _mem_kv(x2d, g, wk, wvt, tm):
    m, k = x2d.shape
    n = wk.shape[1]
    return pl.pallas_call(
        _mem_kv_kernel,
        grid=(m // tm,),
        in_specs=[
            pl.BlockSpec((tm, k), lambda i: (i, 0)),
            pl.BlockSpec((1, k), lambda i: (0, 0)),
            pl.BlockSpec((k, n), lambda i: (0, 0)),
            pl.BlockSpec((n, k), lambda i: (0, 0)),
        ],
        out_specs=[
            pl.BlockSpec((n // LANES, tm, LANES), lambda i: (0, i, 0)),
            pl.BlockSpec((n // LANES, LANES, tm), lambda i: (0, 0, i)),
        ],
        out_shape=[
            jax.ShapeDtypeStruct((n // LANES, m, LANES), BF16),
            jax.ShapeDtypeStruct((n // LANES, LANES, m), BF16),
        ],
        compiler_params=_params(1),
        name="mem_kv",
    )(x2d, g, wk, wvt)


def _forget_cumsum_kernel(logit_ref, b_ref, tri_ref, place_ref, o_ref, carry_ref):
    @pl.when(pl.program_id(1) == 0)
    def _():
        carry_ref[...] = jnp.zeros_like(carry_ref)

    n_seq, tb = logit_ref.shape[0], logit_ref.shape[1]
    sums = []
    for r in range(n_seq):
        z = logit_ref[r] + b_ref[...]
        log_f = jnp.minimum(z, 0.0) - jnp.log1p(jnp.exp(-jnp.abs(z)))
        sums.append(jnp.dot(tri_ref[...], jnp.concatenate(_split3(log_f), axis=1),
                            preferred_element_type=F32))
    exts = []
    for r in range(n_seq):
        c = carry_ref[r] + (sums[r][:, :LANES] + sums[r][:, LANES:2 * LANES]
                            + sums[r][:, 2 * LANES:])
        carry_ref[r] = c[tb - 1:tb, :]
        exts.append(jnp.dot(jnp.concatenate(_split3(-c), axis=1), place_ref[...],
                            preferred_element_type=F32))
    for r in range(n_seq):
        for h in range(N_CB):
            o_ref[h, r] = exts[r][:, h * LANES:(h + 1) * LANES].astype(BF16)


def _placement():
    place = np.zeros((3 * LANES, N_CB * LANES), np.float32)
    for term in range(3):
        for h in range(N_CB):
            place[term * LANES + 2 * h, h * LANES + HEAD_DIM + term] = 1.0
            place[term * LANES + 2 * h + 1, h * LANES + term] = 1.0
    return jnp.asarray(place, BF16)


def _forget_cumsum(logit3, b_pad, tb):
    b, s, _ = logit3.shape
    n_seq = _pick(b, (4, 2, 1))
    tri = jnp.asarray(np.tril(np.ones((tb, tb), np.float32)), BF16)
    place = _placement()
    return pl.pallas_call(
        _forget_cumsum_kernel,
        grid=(b // n_seq, s // tb),
        in_specs=[
            pl.BlockSpec((n_seq, tb, LANES), lambda i, j: (i, j, 0)),
            pl.BlockSpec((1, LANES), lambda i, j: (0, 0)),
            pl.BlockSpec((tb, tb), lambda i, j: (0, 0)),
            pl.BlockSpec(place.shape, lambda i, j: (0, 0)),
        ],
        out_specs=pl.BlockSpec((N_CB, n_seq, tb, LANES), lambda i, j: (0, i, j, 0)),
        out_shape=jax.ShapeDtypeStruct((N_CB, b, s, LANES), BF16),
        scratch_shapes=[pltpu.VMEM((n_seq, 1, LANES), F32)],
        compiler_params=_params(2),
        name="forget_cumsum",
    )(logit3, b_pad, tri, place)


def _t5_buckets(bk, bq):
    key = np.arange(bk, dtype=np.int32)[:, None]
    query = np.arange(bq, dtype=np.int32)[None, :]
    dist = np.stack([query - key, query - key + bq])
    n = np.maximum(dist, 0)
    max_exact = NUM_BUCKETS // 2
    nf = np.maximum(n, 1).astype(np.float32)
    large = max_exact + (np.log(nf / np.float32(max_exact))
                         / np.float32(math.log(MAX_DISTANCE / max_exact))
                         * np.float32(NUM_BUCKETS - max_exact)).astype(np.int32)
    large = np.minimum(large, NUM_BUCKETS - 1)
    bucket = np.where(n < max_exact, n, large)
    return jnp.asarray(np.where(dist >= 0, bucket, -1), jnp.int32)


def _bias_tiles_kernel(rb_ref, bucket_ref, o_ref):
    h = pl.program_id(0)
    bucket = bucket_ref[0]
    far = rb_ref[NUM_BUCKETS - 1, h]
    bias = jnp.zeros(bucket.shape, F32)
    for b in range(NUM_BUCKETS - 1):
        bias = jnp.where(bucket == b, rb_ref[b, h] - far, bias)
    o_ref[0, 0] = jnp.where(bucket >= 0, bias, NEG)


def _bias_tiles(rel_bias, bk, bq):
    n_heads = rel_bias.shape[1]
    return pl.pallas_call(
        _bias_tiles_kernel,
        grid=(n_heads, 2),
        in_specs=[pl.BlockSpec(memory_space=pltpu.SMEM),
                  pl.BlockSpec((1, bk, bq), lambda h, j: (j, 0, 0))],
        out_specs=pl.BlockSpec((1, 1, bk, bq), lambda h, j: (h, j, 0, 0)),
        out_shape=jax.ShapeDtypeStruct((n_heads, 2, bk, bq), F32),
        compiler_params=_params(2),
        name="bias_tiles",
    )(rel_bias, _t5_buckets(bk, bq))


def _stack_halves(q):
    lane = lax.broadcasted_iota(jnp.int32, q.shape, 1)
    zero = jnp.zeros_like(q)
    return jnp.concatenate([jnp.where(lane < HEAD_DIM, q, zero),
                            jnp.where(lane < HEAD_DIM, zero, q)], axis=0)


def _key_block(ref, h, kb, bk):
    return ref[h, 0, pl.ds(pl.multiple_of(kb * bk, bk), bk), :]


def _q_spec(bq, cb):
    return pl.BlockSpec((N_CB, 1, bq, LANES), lambda b, i: (cb // N_CB, b, i, 0))


def _kv_spec(s, cb):
    return pl.BlockSpec((N_CB, 1, s, LANES), lambda b, i: (cb // N_CB, b, 0, 0))


def _gate_spec(bq, group):
    return pl.BlockSpec((N_CB, 1, bq, LANES), lambda b, i: (CB_GATE // N_CB + group, b, i, 0))


def _out_spec(bq):
    return pl.BlockSpec((1, bq, GROUP_W), lambda b, i: (b, i, 0))


def _store_gated(o_ref, gate_ref, h, o):
    o_ref[0, :, h * LANES:(h + 1) * LANES] = (o * _silu(gate_ref[h, 0])).astype(o_ref.dtype)


def _with_ones_rows(vt):
    return jnp.concatenate([vt, jnp.ones((BF16_ROWS, vt.shape[1]), BF16)], axis=0)


def _value_t(vts, p):
    n = p.shape[1] // 2
    return jnp.concatenate([jnp.dot(vts[0], p[:, :n], preferred_element_type=F32),
                            jnp.dot(vts[1], p[:, n:], preferred_element_type=F32)], axis=1)


def _scores_t(q_st, k_ref, h, kb, bk):
    return lax.dot_general(_key_block(k_ref, h, kb, bk), q_st, NT_DIMS,
                           preferred_element_type=F32)


VALUE_LAG = 2


def _softmax_t_blocks(items, carries):
    ms = [None if carries is None else carries[h][0] for h in range(N_CB)]
    accs = [None if carries is None else carries[h][1] for h in range(N_CB)]
    pending = {}

    def score(i):
        h, score_fn, _ = items[i]
        s = score_fn()
        m = jnp.max(s, axis=0, keepdims=True)
        m_new = m if ms[h] is None else jnp.maximum(ms[h], m)
        pending[i] = (ms[h], m_new, jnp.exp(s - m_new).astype(BF16))
        ms[h] = m_new

    def value(i):
        h, _, vt_fn = items[i]
        m_old, m_new, p = pending.pop(i)
        pv = _value_t(vt_fn(), p)
        accs[h] = pv if accs[h] is None else jnp.exp(m_old - m_new) * accs[h] + pv

    for i in range(len(items) + VALUE_LAG):
        if i < len(items):
            score(i)
        if i >= VALUE_LAG:
            value(i - VALUE_LAG)
    return tuple((ms[h], accs[h]) for h in range(N_CB))


LOOP_BLOCKS = 4


def _store_state(state_refs, carries):
    for h, chain in enumerate(carries):
        for ref, value in zip(state_refs, chain):
            ref[h] = value


def _pair_loop(n_blocks, block_of, step, carries, state_refs, always_even=False):
    def load():
        return tuple(tuple(ref[h] for ref in state_refs) for h in range(N_CB))

    def run(first, width):
        _store_state(state_refs, step(tuple(block_of(first + j) for j in range(width)), load()))

    if carries is not None:
        _store_state(state_refs, carries)

    def trip(i, _):
        run(LOOP_BLOCKS * i, LOOP_BLOCKS)
        return 0

    lax.fori_loop(0, n_blocks // LOOP_BLOCKS, trip, 0)
    done = n_blocks - n_blocks % LOOP_BLOCKS

    @pl.when(n_blocks - done >= 2)
    def _():
        run(done, 2)

    if not always_even:
        @pl.when(n_blocks % 2 == 1)
        def _():
            run(n_blocks - 1, 1)

    return load()


def _normalized_t(acc):
    d = acc.shape[0] - BF16_ROWS
    return acc[:d] * (1.0 / acc[d:d + 1])


def _diff_kernel(lam_init, q_ref, k_ref, vt_ref, gate_ref, tile_ref, lamp_ref, hn_ref, o_ref,
                 m_ref, acc_ref):
    bq = q_ref.shape[2]
    qi = pl.program_id(1)
    qs = [_stack_halves(q_ref[h, 0] * (HEAD_DIM ** -0.5)) for h in range(N_CB)]

    def item(h, kb, tile_fn=None, width=1):
        rows = width * bq

        def score():
            k = k_ref[h, 0, pl.ds(pl.multiple_of(kb * bq, bq), rows), :]
            s = lax.dot_general(k, qs[h], NT_DIMS, preferred_element_type=F32)
            if tile_fn is None:
                return s
            tile = tile_fn(h)
            return s + jnp.concatenate([tile, tile], axis=1)

        def values():
            vt = _with_ones_rows(vt_ref[h, :, pl.ds(pl.multiple_of(kb * bq, bq), rows)])
            return vt, vt
        return h, score, values

    def step(kbs, carries):
        if len(kbs) % 2 == 0:
            items = [item(h, kb, None, 2) for kb in kbs[::2] for h in range(N_CB)]
        else:
            items = [item(h, kb) for kb in kbs for h in range(N_CB)]
        return _softmax_t_blocks(items, carries)

    near = [item(h, jnp.maximum(qi - 1, 0), lambda h: tile_ref[h, jnp.minimum(qi, 1)], 2)
            for h in range(N_CB)]
    state = (m_ref, acc_ref)
    n_far = jnp.maximum(qi - 1, 0)
    odd = n_far % 2

    @pl.when(odd == 1)
    def _():
        _store_state(state, _softmax_t_blocks([item(h, qi - 2) for h in range(N_CB)] + near, None))

    @pl.when(odd == 0)
    def _():
        _store_state(state, _softmax_t_blocks(near, None))

    carries = _pair_loop(n_far - odd, lambda i: i, step, None, state, always_even=True)

    lp = lamp_ref[...]
    lam = (jnp.exp(jnp.sum(lp[0:1] * lp[1:2], axis=-1, keepdims=True))
           - jnp.exp(jnp.sum(lp[2:3] * lp[3:4], axis=-1, keepdims=True)) + lam_init)
    head_gain = jnp.broadcast_to(hn_ref[...] * (1.0 - lam_init), (LANES, bq))
    for h in range(N_CB):
        a = _normalized_t(carries[h][1])
        o = a[:, :bq] - lam * a[:, bq:]
        o = o * lax.rsqrt(jnp.mean(o * o, axis=0, keepdims=True) + EPS)
        _store_gated(o_ref, gate_ref, h, (o * head_gain).T)


def _diff_attention(proj4, vt, tiles, lam_params, head_norm, lam_init, bq):
    _, b, s, _ = proj4.shape
    return pl.pallas_call(
        functools.partial(_diff_kernel, lam_init),
        grid=(b, s // bq),
        in_specs=[
            _q_spec(bq, CB_AQ), _kv_spec(s, CB_AK),
            pl.BlockSpec((N_CB, LANES, s), lambda b, i: (VT_A, 0, b)),
            _gate_spec(bq, 0),
            pl.BlockSpec((N_CB, 2, 2 * bq, bq), lambda b, i: (0, 0, 0, 0)),
            pl.BlockSpec((4, HEAD_DIM), lambda b, i: (0, 0)),
            pl.BlockSpec((LANES, 1), lambda b, i: (0, 0)),
        ],
        out_specs=_out_spec(bq),
        out_shape=jax.ShapeDtypeStruct((b, s, GROUP_W), BF16),
        scratch_shapes=[pltpu.VMEM((N_CB, 1, 2 * bq), F32),
                        pltpu.VMEM((N_CB, LANES + BF16_ROWS, 2 * bq), F32)],
        compiler_params=_params(2),
        name="diff_attention",
    )(proj4, proj4, vt, proj4, tiles, lam_params, head_norm)


def _stick_kernel(q_ref, k_ref, vt_ref, gate_ref, o_ref, rest_ref, acc_ref):
    bq = q_ref.shape[2]
    qi = pl.program_id(1)
    qs = [_stack_halves(q_ref[h, 0] * (HEAD_DIM ** -0.5)) for h in range(N_CB)]
    key = lax.broadcasted_iota(jnp.int32, (bq, 2 * bq), 0)
    query = lax.broadcasted_iota(jnp.int32, (bq, 2 * bq), 1) & (bq - 1)
    strict = key < query

    row = lax.broadcasted_iota(jnp.int32, (bq + BF16_ROWS, bq), 0)
    col = lax.broadcasted_iota(jnp.int32, (bq + BF16_ROWS, bq), 1)
    after = jnp.where((col > row) | (row >= bq), 1.0, 0.0).astype(BF16)

    def step(kbs, carries, first_mask=None):
        items = [(h, kb) for kb in kbs for h in range(N_CB)]
        masks = [first_mask if i < N_CB else None for i in range(len(items))]
        rests = [carries[h][0] for h in range(N_CB)]
        accs = [carries[h][1] for h in range(N_CB)]
        pending = {}

        def score(i):
            h, kb = items[i]
            z = _scores_t(qs[h], k_ref, h, kb, bq)
            sp = jnp.log(1.0 + jnp.exp2(jnp.abs(z) * -LOG2E))
            ls_pos = jnp.minimum(z, 0.0) - sp
            ls_neg = ls_pos - z
            if masks[i] is not None:
                ls_neg = jnp.where(masks[i], ls_neg, 0.0)
            pending[i] = (ls_pos, ls_neg.astype(BF16))

        def cumsum(i):
            h, _ = items[i]
            ls_pos, log_rest = pending[i]
            t = jnp.dot(after, log_rest, preferred_element_type=F32)
            a = jnp.exp2((ls_pos + t[:bq]) * LOG2E)
            if masks[i] is not None:
                a = jnp.where(masks[i], a, 0.0)
            pending[i] = (a.astype(BF16), jnp.exp2(rests[h] * LOG2E))
            rests[h] = rests[h] + t[bq:bq + 1]

        def value(i):
            h, kb = items[i]
            a, scale = pending.pop(i)
            vt = vt_ref[h, :, pl.ds(pl.multiple_of(kb * bq, bq), bq)]
            accs[h] = accs[h] + scale * _value_t((vt[:HEAD_DIM], vt[HEAD_DIM:]), a)

        for i in range(len(items) + 2):
            if i < len(items):
                score(i)
            if 1 <= i <= len(items):
                cumsum(i - 1)
            if i >= 2:
                value(i - 2)
        return tuple((rests[h], accs[h]) for h in range(N_CB))

    zero = ((jnp.zeros((1, 2 * bq), F32), jnp.zeros((HEAD_DIM, 2 * bq), F32)),) * N_CB
    state = (rest_ref, acc_ref)
    odd = qi % 2

    @pl.when(odd == 1)
    def _():
        _store_state(state, step((qi, qi - 1), zero, strict))

    @pl.when(odd == 0)
    def _():
        _store_state(state, step((qi,), zero, strict))

    carries = _pair_loop(qi - odd, lambda i: qi - 1 - odd - i, step, None, state,
                         always_even=True)
    for h in range(N_CB):
        acc = carries[h][1]
        _store_gated(o_ref, gate_ref, h, jnp.concatenate([acc[:, :bq], acc[:, bq:]], axis=0).T)


def _stick_attention(proj4, vt, bq):
    _, b, s, _ = proj4.shape
    return pl.pallas_call(
        _stick_kernel,
        grid=(b, s // bq),
        in_specs=[
            _q_spec(bq, CB_SQ), _kv_spec(s, CB_SK),
            pl.BlockSpec((N_CB, LANES, s), lambda b, i: (VT_S, 0, b)),
            _gate_spec(bq, 1),
        ],
        out_specs=_out_spec(bq),
        out_shape=jax.ShapeDtypeStruct((b, s, GROUP_W), BF16),
        scratch_shapes=[pltpu.VMEM((N_CB, 1, 2 * bq), F32),
                        pltpu.VMEM((N_CB, HEAD_DIM, 2 * bq), F32)],
        compiler_params=_params(2),
        name="stick_attention",
    )(proj4, proj4, vt, proj4)


def _fox_kernel(q_ref, k_ref, vt_ref, gate_ref, negc_ref, o_ref, m_ref, acc_ref):
    bq = q_ref.shape[2]
    qi = pl.program_id(1)
    key = lax.broadcasted_iota(jnp.int32, (bq, 2 * bq), 0)
    query = lax.broadcasted_iota(jnp.int32, (bq, 2 * bq), 1) & (bq - 1)
    causal = key <= query

    lane = lax.broadcasted_iota(jnp.int32, (1, LANES), 1)
    low = lane < HEAD_DIM
    ones_low = jnp.where(lane < 3, 1.0, 0.0).astype(BF16)
    ones_high = jnp.where(low | (lane >= HEAD_DIM + 3), 0.0, 1.0).astype(BF16)
    qs = []
    for h in range(N_CB):
        q = q_ref[h, 0] * (HEAD_DIM ** -0.5)
        qs.append((jnp.where(low, q, ones_high), jnp.where(low, ones_low, q)))

    def item(h, kb, mask, width=1):
        rows = width * bq

        def score():
            start = pl.multiple_of(kb * bq, bq)
            k = k_ref[h, 0, pl.ds(start, rows), :]
            c = negc_ref[h, 0, pl.ds(start, rows), :]
            s = jnp.concatenate(
                [lax.dot_general(jnp.where(low, k, c), qs[h][0], NT_DIMS,
                                 preferred_element_type=F32),
                 lax.dot_general(jnp.where(low, c, k), qs[h][1], NT_DIMS,
                                 preferred_element_type=F32)], axis=1)
            return s if mask is None else jnp.where(mask, s, NEG)

        def values():
            vt = vt_ref[h, :, pl.ds(pl.multiple_of(kb * bq, bq), rows)]
            return _with_ones_rows(vt[:HEAD_DIM]), _with_ones_rows(vt[HEAD_DIM:])
        return h, score, values

    def step(kbs, carries, mask=None):
        if len(kbs) % 2 == 0:
            items = [item(h, kb, mask, 2) for kb in kbs[::2] for h in range(N_CB)]
        else:
            items = [item(h, kb, mask) for kb in kbs for h in range(N_CB)]
        return _softmax_t_blocks(items, carries)

    state = (m_ref, acc_ref)
    odd = qi % 2

    @pl.when(odd == 1)
    def _():
        key2 = lax.broadcasted_iota(jnp.int32, (2 * bq, 2 * bq), 0) - bq
        query2 = lax.broadcasted_iota(jnp.int32, (2 * bq, 2 * bq), 1) & (bq - 1)
        wide = [item(h, qi - 1, key2 <= query2, 2) for h in range(N_CB)]
        _store_state(state, _softmax_t_blocks(wide, None))

    @pl.when(odd == 0)
    def _():
        _store_state(state, step((qi,), None, causal))

    carries = _pair_loop(qi - odd, lambda i: i, step, None, state, always_even=True)
    for h in range(N_CB):
        a = _normalized_t(carries[h][1])
        _store_gated(o_ref, gate_ref, h, jnp.concatenate([a[:, :bq], a[:, bq:]], axis=0).T)


def _fox_attention(proj4, vt, negc, bq):
    _, b, s, _ = proj4.shape
    return pl.pallas_call(
        _fox_kernel,
        grid=(b, s // bq),
        in_specs=[
            _q_spec(bq, CB_FQ), _kv_spec(s, CB_FK),
            pl.BlockSpec((N_CB, LANES, s), lambda b, i: (VT_F, 0, b)),
            _gate_spec(bq, 2),
            pl.BlockSpec((N_CB, 1, s, LANES), lambda b, i: (0, b, 0, 0)),
        ],
        out_specs=_out_spec(bq),
        out_shape=jax.ShapeDtypeStruct((b, s, GROUP_W), BF16),
        scratch_shapes=[pltpu.VMEM((N_CB, 1, 2 * bq), F32),
                        pltpu.VMEM((N_CB, HEAD_DIM + BF16_ROWS, 2 * bq), F32)],
        compiler_params=_params(2),
        name="fox_attention",
    )(proj4, proj4, vt, proj4, negc)


def _mem_kernel(q_ref, km_ref, vmt_ref, gate_ref, o_ref):
    scale = LANES ** -0.5
    scores = [lax.dot_general(km_ref[h, 0], q_ref[h, 0], NT_DIMS,
                              preferred_element_type=F32) * scale for h in range(N_CB)]
    probs = [jnp.exp(s - jnp.max(s, axis=0, keepdims=True)).astype(BF16) for s in scores]
    for h in range(N_CB):
        acc = jnp.dot(_with_ones_rows(vmt_ref[h]), probs[h], preferred_element_type=F32)
        _store_gated(o_ref, gate_ref, h, _normalized_t(acc).T)


def _mem_attention(proj4, km4, vmt, bq):
    _, b, s, _ = proj4.shape
    mlen = km4.shape[2]
    return pl.pallas_call(
        _mem_kernel,
        grid=(b, s // bq),
        in_specs=[
            _q_spec(bq, CB_MQ),
            pl.BlockSpec((N_CB, 1, mlen, LANES), lambda b, i: (0, b, 0, 0)),
            pl.BlockSpec((N_CB, LANES, mlen), lambda b, i: (0, 0, b)),
            _gate_spec(bq, 3),
        ],
        out_specs=_out_spec(bq),
        out_shape=jax.ShapeDtypeStruct((b, s, GROUP_W), BF16),
        compiler_params=_params(2),
        name="mem_attention",
    )(proj4, km4, vmt, proj4)


def _out_proj_kernel(ya_ref, yb_ref, yc_ref, ym_ref, w_ref, g_ref, x_ref, o_ref):
    y = jnp.concatenate([ya_ref[...], yb_ref[...], yc_ref[...], ym_ref[...]], axis=-1)
    z = jnp.dot(y, w_ref[...], preferred_element_type=F32)
    z = z * lax.rsqrt(jnp.mean(z * z, axis=-1, keepdims=True) + EPS)
    o_ref[...] = x_ref[...] + z * g_ref[...]


def _out_proj(ys, w, g, x2d, tm):
    m, d = x2d.shape
    y_spec = pl.BlockSpec((tm, GROUP_W), lambda i: (i, 0))
    return pl.pallas_call(
        _out_proj_kernel,
        grid=(m // tm,),
        in_specs=[y_spec] * 4 + [
            pl.BlockSpec(w.shape, lambda i: (0, 0)),
            pl.BlockSpec((1, d), lambda i: (0, 0)),
            pl.BlockSpec((tm, d), lambda i: (i, 0)),
        ],
        out_specs=pl.BlockSpec((tm, d), lambda i: (i, 0)),
        out_shape=jax.ShapeDtypeStruct((m, d), F32),
        compiler_params=_params(1),
        name="out_proj",
    )(*ys, w, g, x2d)


def _pick(n, cands):
    for c in cands:
        if n % c == 0:
            return c
    raise ValueError(f"no tile in {cands} divides {n}")


def kernel(x, mem, w_in, b_forget, w_mem_kv, w_out, pre_norm, post_norm, mem_norm,
           diff_lambda, diff_head_norm, rel_bias):
    b, s, d = x.shape
    mlen = mem.shape[1]
    depth = w_in.shape[0]
    m = b * s
    bq = _pick(s, (256, 128))
    tm = _pick(m, (512, 256))

    gw = GROUP_W
    logit_lo, logit_hi = 9 * gw, 9 * gw + N_FOX_HEADS
    tiles = _bias_tiles(rel_bias, 2 * bq, bq)
    mem2d = mem.reshape(b * mlen, d)
    x2d = x.reshape(m, d)

    for l in range(depth):
        wl = w_in[l]
        ws = [wl[:, lo:hi].astype(BF16) for lo, hi in
              ((0, 2 * gw), (3 * gw, 5 * gw), (6 * gw, 8 * gw), (logit_hi, wl.shape[1]))]
        wvts = [wl[:, lo:lo + gw].T.astype(BF16) for lo in (2 * gw, 5 * gw, 8 * gw)]
        w_logit = jnp.pad(wl[:, logit_lo:logit_hi],
                          ((0, 0), (0, LANES - N_FOX_HEADS))).astype(BF16)
        b_pad = jnp.pad(b_forget[l], (0, LANES - N_FOX_HEADS)).reshape(1, LANES)

        proj, logit, vt = _in_proj(x2d, pre_norm[l].reshape(1, d), ws, w_logit, wvts, tm)
        proj4 = proj.reshape(N_PROJ_CB, b, s, LANES)
        negc = _forget_cumsum(logit.reshape(b, s, LANES), b_pad, _pick(s, (512, 256, 128)))

        w_kv = w_mem_kv[l].astype(BF16)
        km, vmt = _mem_kv(mem2d, mem_norm[l].reshape(1, d), w_kv[:, :gw], w_kv[:, gw:].T,
                          _pick(b * mlen, (1024, 512, 256)))
        km4 = km.reshape(N_CB, b, mlen, LANES)

        lam_init = 0.8 - 0.6 * math.exp(-0.3 * l)
        y_a = _diff_attention(proj4, vt, tiles, diff_lambda[l],
                              diff_head_norm[l].reshape(LANES, 1), lam_init, bq)
        y_b = _stick_attention(proj4, vt, bq)
        y_c = _fox_attention(proj4, vt, negc, bq)
        y_m = _mem_attention(proj4, km4, vmt, _pick(s, (1024, 512, 256, 128)))
        ys = [y.reshape(m, GROUP_W) for y in (y_a, y_b, y_c, y_m)]
        x2d = _out_proj(ys, w_out[l].astype(BF16), post_norm[l].reshape(1, d), x2d,
                        _pick(m, (1024, 512, 256)))
    return x2d.reshape(b, s, d)
```

```python
import functools
import math

import jax
import jax.numpy as jnp
import numpy as np
from jax import lax
from jax.experimental import pallas as pl
from jax.experimental.pallas import tpu as pltpu

F32 = jnp.float32
BF16 = jnp.bfloat16

LANES = 128
BF16_ROWS = 16
HEAD_DIM = 64
GROUP_W = 512
N_CB = GROUP_W // LANES
N_FOX_HEADS = GROUP_W // HEAD_DIM
NUM_BUCKETS = 32
MAX_DISTANCE = 128
EPS = 1e-6
NEG = -1e30
LOG2E = 1.4426950408889634
VMEM_LIMIT = 56 * 1024 * 1024

CB_AQ, CB_AK = 0, 4
CB_SQ, CB_SK = 8, 12
CB_FQ, CB_FK = 16, 20
CB_MQ, CB_GATE = 24, 28
N_PROJ_CB = 44
VT_A, VT_S, VT_F = 0, 1, 2

NT_DIMS = (((1,), (1,)), ((), ()))


def _params(n_axes):
    return pltpu.CompilerParams(dimension_semantics=("arbitrary",) * n_axes,
                                vmem_limit_bytes=VMEM_LIMIT)


def _split3(x):
    hi = x.astype(BF16)
    r = x - hi.astype(F32)
    mid = r.astype(BF16)
    lo = (r - mid.astype(F32)).astype(BF16)
    return hi, mid, lo


def _silu(g):
    g = g.astype(F32)
    return g / (1.0 + jnp.exp(-g))


def _rms_rows(x, g_ref):
    y = x * lax.rsqrt(jnp.mean(x * x, axis=-1, keepdims=True) + EPS)
    return (y * g_ref[...]).astype(BF16)


def _store_col_blocks(o_ref, res):
    for c in range(o_ref.shape[0]):
        o_ref[c] = res[:, c * LANES:(c + 1) * LANES]


MAX_DOT_COLUMNS = 1536


def _in_proj_kernel(n_w, x0_ref, xn_ref, g_ref, wl_ref, *refs):
    w_refs, wvt_refs = refs[:n_w], refs[n_w:-4]
    o_ref, ol_ref, ovt_ref, h_ref = refs[-4:]
    i = pl.program_id(0)

    @pl.when(i == 0)
    def _():
        h_ref[0] = _rms_rows(x0_ref[...], g_ref)

    slot = lax.rem(i, 2)
    h_ref[1 - slot] = _rms_rows(xn_ref[...], g_ref)
    h = h_ref[slot]
    ol_ref[...] = jnp.dot(h, wl_ref[...], preferred_element_type=F32)
    block = 0
    for wvt_ref in wvt_refs:
        vt = lax.dot_general(wvt_ref[...], h, NT_DIMS, preferred_element_type=F32).astype(BF16)
        for r in range(vt.shape[0] // LANES):
            ovt_ref[block + r] = vt[r * LANES:(r + 1) * LANES, :]
        block += vt.shape[0] // LANES
    block = 0
    for w_ref in w_refs:
        width = w_ref.shape[1]
        n_chunks = -(-width // MAX_DOT_COLUMNS)
        chunk = width // n_chunks
        for c in range(n_chunks):
            res = jnp.dot(h, w_ref[:, c * chunk:(c + 1) * chunk],
                          preferred_element_type=F32).astype(BF16)
            for r in range(chunk // LANES):
                o_ref[block + r] = res[:, r * LANES:(r + 1) * LANES]
            block += chunk // LANES


def _in_proj(x2d, g, ws, wl, wvts, tm):
    m, k = x2d.shape
    n = sum(w.shape[1] for w in ws)
    nv = sum(w.shape[0] for w in wvts)
    last = m // tm - 1
    resident = pl.Buffered(1)
    return pl.pallas_call(
        functools.partial(_in_proj_kernel, len(ws)),
        grid=(m // tm,),
        in_specs=[
            pl.BlockSpec((tm, k), lambda i: (0, 0), pipeline_mode=resident),
            pl.BlockSpec((tm, k), lambda i: (jnp.minimum(i + 1, last), 0)),
            pl.BlockSpec((1, k), lambda i: (0, 0)),
            pl.BlockSpec((k, LANES), lambda i: (0, 0)),
        ] + [pl.BlockSpec(w.shape, lambda i: (0, 0), pipeline_mode=resident) for w in ws + wvts],
        out_specs=[
            pl.BlockSpec((n // LANES, tm, LANES), lambda i: (0, i, 0)),
            pl.BlockSpec((tm, LANES), lambda i: (i, 0)),
            pl.BlockSpec((nv // LANES, LANES, tm), lambda i: (0, 0, i)),
        ],
        out_shape=[
            jax.ShapeDtypeStruct((n // LANES, m, LANES), BF16),
            jax.ShapeDtypeStruct((m, LANES), F32),
            jax.ShapeDtypeStruct((nv // LANES, LANES, m), BF16),
        ],
        scratch_shapes=[pltpu.VMEM((2, tm, k), BF16)],
        compiler_params=_params(1),
        name="in_proj",
    )(x2d, x2d, g, wl, *ws, *wvts)


def _mem_kv_kernel(x_ref, g_ref, wk_ref, wvt_ref, ok_ref, ovt_ref):
    h = _rms_rows(x_ref[...], g_ref)
    _store_col_blocks(ok_ref, jnp.dot(h, wk_ref[...], preferred_element_type=F32).astype(BF16))
    vt = lax.dot_general(wvt_ref[...], h, NT_DIMS, preferred_element_type=F32).astype(BF16)
    for c in range(ovt_ref.shape[0]):
        ovt_ref[c] = vt[c * LANES:(c + 1) * LANES, :]


def _mem_kv(x2d, g, wk, wvt, tm):
    m, k = x2d.shape
    n = wk.shape[1]
    return pl.pallas_call(
        _mem_kv_kernel,
        grid=(m // tm,),
        in_specs=[
            pl.BlockSpec((tm, k), lambda i: (i, 0)),
            pl.BlockSpec((1, k), lambda i: (0, 0)),
            pl.BlockSpec((k, n), lambda i: (0, 0)),
            pl.BlockSpec((n, k), lambda i: (0, 0)),
        ],
        out_specs=[
            pl.BlockSpec((n // LANES, tm, LANES), lambda i: (0, i, 0)),
            pl.BlockSpec((n // LANES, LANES, tm), lambda i: (0, 0, i)),
        ],
        out_shape=[
            jax.ShapeDtypeStruct((n // LANES, m, LANES), BF16),
            jax.ShapeDtypeStruct((n // LANES, LANES, m), BF16),
        ],
        compiler_params=_params(1),
        name="mem_kv",
    )(x2d, g, wk, wvt)


def _forget_cumsum_kernel(logit_ref, b_ref, tri_ref, place_ref, o_ref, carry_ref):
    @pl.when(pl.program_id(1) == 0)
    def _():
        carry_ref[...] = jnp.zeros_like(carry_ref)

    n_seq, tb = logit_ref.shape[0], logit_ref.shape[1]
    sums = []
    for r in range(n_seq):
        z = logit_ref[r] + b_ref[...]
        log_f = jnp.minimum(z, 0.0) - jnp.log1p(jnp.exp(-jnp.abs(z)))
        sums.append(jnp.dot(tri_ref[...], jnp.concatenate(_split3(log_f), axis=1),
                            preferred_element_type=F32))
    exts = []
    for r in range(n_seq):
        c = carry_ref[r] + (sums[r][:, :LANES] + sums[r][:, LANES:2 * LANES]
                            + sums[r][:, 2 * LANES:])
        carry_ref[r] = c[tb - 1:tb, :]
        exts.append(jnp.dot(jnp.concatenate(_split3(-c), axis=1), place_ref[...],
                            preferred_element_type=F32))
    for r in range(n_seq):
        for h in range(N_CB):
            o_ref[h, r] = exts[r][:, h * LANES:(h + 1) * LANES].astype(BF16)


def _placement():
    place = np.zeros((3 * LANES, N_CB * LANES), np.float32)
    for term in range(3):
        for h in range(N_CB):
            place[term * LANES + 2 * h, h * LANES + HEAD_DIM + term] = 1.0
            place[term * LANES + 2 * h + 1, h * LANES + term] = 1.0
    return jnp.asarray(place, BF16)


def _forget_cumsum(logit3, b_pad, tb):
    b, s, _ = logit3.shape
    n_seq = _pick(b, (4, 2, 1))
    tri = jnp.asarray(np.tril(np.ones((tb, tb), np.float32)), BF16)
    place = _placement()
    return pl.pallas_call(
        _forget_cumsum_kernel,
        grid=(b // n_seq, s // tb),
        in_specs=[
            pl.BlockSpec((n_seq, tb, LANES), lambda i, j: (i, j, 0)),
            pl.BlockSpec((1, LANES), lambda i, j: (0, 0)),
            pl.BlockSpec((tb, tb), lambda i, j: (0, 0)),
            pl.BlockSpec(place.shape, lambda i, j: (0, 0)),
        ],
        out_specs=pl.BlockSpec((N_CB, n_seq, tb, LANES), lambda i, j: (0, i, j, 0)),
        out_shape=jax.ShapeDtypeStruct((N_CB, b, s, LANES), BF16),
        scratch_shapes=[pltpu.VMEM((n_seq, 1, LANES), F32)],
        compiler_params=_params(2),
        name="forget_cumsum",
    )(logit3, b_pad, tri, place)


def _t5_buckets(bk, bq):
    key = np.arange(bk, dtype=np.int32)[:, None]
    query = np.arange(bq, dtype=np.int32)[None, :]
    dist = np.stack([query - key, query - key + bq])
    n = np.maximum(dist, 0)
    max_exact = NUM_BUCKETS // 2
    nf = np.maximum(n, 1).astype(np.float32)
    large = max_exact + (np.log(nf / np.float32(max_exact))
                         / np.float32(math.log(MAX_DISTANCE / max_exact))
                         * np.float32(NUM_BUCKETS - max_exact)).astype(np.int32)
    large = np.minimum(large, NUM_BUCKETS - 1)
    bucket = np.where(n < max_exact, n, large)
    return jnp.asarray(np.where(dist >= 0, bucket, -1), jnp.int32)


def _bias_tiles_kernel(rb_ref, bucket_ref, o_ref):
    h = pl.program_id(0)
    bucket = bucket_ref[0]
    far = rb_ref[NUM_BUCKETS - 1, h]
    bias = jnp.zeros(bucket.shape, F32)
    for b in range(NUM_BUCKETS - 1):
        bias = jnp.where(bucket == b, rb_ref[b, h] - far, bias)
    o_ref[0, 0] = jnp.where(bucket >= 0, bias, NEG)


def _bias_tiles(rel_bias, bk, bq):
    n_heads = rel_bias.shape[1]
    return pl.pallas_call(
        _bias_tiles_kernel,
        grid=(n_heads, 2),
        in_specs=[pl.BlockSpec(memory_space=pltpu.SMEM),
                  pl.BlockSpec((1, bk, bq), lambda h, j: (j, 0, 0))],
        out_specs=pl.BlockSpec((1, 1, bk, bq), lambda h, j: (h, j, 0, 0)),
        out_shape=jax.ShapeDtypeStruct((n_heads, 2, bk, bq), F32),
        compiler_params=_params(2),
        name="bias_tiles",
    )(rel_bias, _t5_buckets(bk, bq))


def _stack_halves(q):
    lane = lax.broadcasted_iota(jnp.int32, q.shape, 1)
    zero = jnp.zeros_like(q)
    return jnp.concatenate([jnp.where(lane < HEAD_DIM, q, zero),
                            jnp.where(lane < HEAD_DIM, zero, q)], axis=0)


def _key_block(ref, h, kb, bk):
    return ref[h, 0, pl.ds(pl.multiple_of(kb * bk, bk), bk), :]


def _q_spec(bq, cb):
    return pl.BlockSpec((N_CB, 1, bq, LANES), lambda b, i: (cb // N_CB, b, i, 0))


def _kv_spec(s, cb):
    return pl.BlockSpec((N_CB, 1, s, LANES), lambda b, i: (cb // N_CB, b, 0, 0))


def _gate_spec(bq, group):
    return pl.BlockSpec((N_CB, 1, bq, LANES), lambda b, i: (CB_GATE // N_CB + group, b, i, 0))


def _out_spec(bq):
    return pl.BlockSpec((1, bq, GROUP_W), lambda b, i: (b, i, 0))


def _store_gated(o_ref, gate_ref, h, o):
    o_ref[0, :, h * LANES:(h + 1) * LANES] = (o * _silu(gate_ref[h, 0])).astype(o_ref.dtype)


def _with_ones_rows(vt):
    return jnp.concatenate([vt, jnp.ones((BF16_ROWS, vt.shape[1]), BF16)], axis=0)


def _value_t(vts, p):
    n = p.shape[1] // 2
    return jnp.concatenate([jnp.dot(vts[0], p[:, :n], preferred_element_type=F32),
                            jnp.dot(vts[1], p[:, n:], preferred_element_type=F32)], axis=1)


def _scores_t(q_st, k_ref, h, kb, bk):
    return lax.dot_general(_key_block(k_ref, h, kb, bk), q_st, NT_DIMS,
                           preferred_element_type=F32)


VALUE_LAG = 2


def _softmax_t_blocks(items, carries):
    ms = [None if carries is None else carries[h][0] for h in range(N_CB)]
    accs = [None if carries is None else carries[h][1] for h in range(N_CB)]
    pending = {}

    def score(i):
        h, score_fn, _ = items[i]
        s = score_fn()
        m = jnp.max(s, axis=0, keepdims=True)
        m_new = m if ms[h] is None else jnp.maximum(ms[h], m)
        pending[i] = (ms[h], m_new, jnp.exp(s - m_new).astype(BF16))
        ms[h] = m_new

    def value(i):
        h, _, vt_fn = items[i]
        m_old, m_new, p = pending.pop(i)
        pv = _value_t(vt_fn(), p)
        accs[h] = pv if accs[h] is None else jnp.exp(m_old - m_new) * accs[h] + pv

    for i in range(len(items) + VALUE_LAG):
        if i < len(items):
            score(i)
        if i >= VALUE_LAG:
            value(i - VALUE_LAG)
    return tuple((ms[h], accs[h]) for h in range(N_CB))


LOOP_BLOCKS = 4


def _store_state(state_refs, carries):
    for h, chain in enumerate(carries):
        for ref, value in zip(state_refs, chain):
            ref[h] = value


def _pair_loop(n_blocks, block_of, step, carries, state_refs, always_even=False):
    def load():
        return tuple(tuple(ref[h] for ref in state_refs) for h in range(N_CB))

    def run(first, width):
        _store_state(state_refs, step(tuple(block_of(first + j) for j in range(width)), load()))

    if carries is not None:
        _store_state(state_refs, carries)

    def trip(i, _):
        run(LOOP_BLOCKS * i, LOOP_BLOCKS)
        return 0

    lax.fori_loop(0, n_blocks // LOOP_BLOCKS, trip, 0)
    done = n_blocks - n_blocks % LOOP_BLOCKS

    @pl.when(n_blocks - done >= 2)
    def _():
        run(done, 2)

    if not always_even:
        @pl.when(n_blocks % 2 == 1)
        def _():
            run(n_blocks - 1, 1)

    return load()


def _normalized_t(acc):
    d = acc.shape[0] - BF16_ROWS
    return acc[:d] * (1.0 / acc[d:d + 1])


def _diff_kernel(lam_init, q_ref, k_ref, vt_ref, gate_ref, tile_ref, lamp_ref, hn_ref, o_ref,
                 m_ref, acc_ref):
    bq = q_ref.shape[2]
    qi = pl.program_id(1)
    qs = [_stack_halves(q_ref[h, 0] * (HEAD_DIM ** -0.5)) for h in range(N_CB)]

    def item(h, kb, tile_fn=None, width=1):
        rows = width * bq

        def score():
            k = k_ref[h, 0, pl.ds(pl.multiple_of(kb * bq, bq), rows), :]
            s = lax.dot_general(k, qs[h], NT_DIMS, preferred_element_type=F32)
            if tile_fn is None:
                return s
            tile = tile_fn(h)
            return s + jnp.concatenate([tile, tile], axis=1)

        def values():
            vt = _with_ones_rows(vt_ref[h, :, pl.ds(pl.multiple_of(kb * bq, bq), rows)])
            return vt, vt
        return h, score, values

    def step(kbs, carries):
        if len(kbs) % 2 == 0:
            items = [item(h, kb, None, 2) for kb in kbs[::2] for h in range(N_CB)]
        else:
            items = [item(h, kb) for kb in kbs for h in range(N_CB)]
        return _softmax_t_blocks(items, carries)

    near = [item(h, jnp.maximum(qi - 1, 0), lambda h: tile_ref[h, 1], 2) for h in range(N_CB)]
    first = [item(h, 0, lambda h: tile_ref[h, 0, :bq]) for h in range(N_CB)]
    state = (m_ref, acc_ref)
    n_far = jnp.maximum(qi - 1, 0)
    odd = n_far % 2

    @pl.when(qi == 0)
    def _():
        _store_state(state, _softmax_t_blocks(first, None))

    @pl.when(odd == 1)
    def _():
        _store_state(state, _softmax_t_blocks([item(h, qi - 2) for h in range(N_CB)] + near, None))

    @pl.when((odd == 0) & (qi > 0))
    def _():
        _store_state(state, _softmax_t_blocks(near, None))

    carries = _pair_loop(n_far - odd, lambda i: i, step, None, state, always_even=True)

    lp = lamp_ref[...]
    lam = (jnp.exp(jnp.sum(lp[0:1] * lp[1:2], axis=-1, keepdims=True))
           - jnp.exp(jnp.sum(lp[2:3] * lp[3:4], axis=-1, keepdims=True)) + lam_init)
    head_gain = jnp.broadcast_to(hn_ref[...] * (1.0 - lam_init), (LANES, bq))
    for h in range(N_CB):
        a = _normalized_t(carries[h][1])
        o = a[:, :bq] - lam * a[:, bq:]
        o = o * lax.rsqrt(jnp.mean(o * o, axis=0, keepdims=True) + EPS)
        _store_gated(o_ref, gate_ref, h, (o * head_gain).T)


def _diff_attention(proj4, vt, tiles, lam_params, head_norm, lam_init, bq):
    _, b, s, _ = proj4.shape
    return pl.pallas_call(
        functools.partial(_diff_kernel, lam_init),
        grid=(b, s // bq),
        in_specs=[
            _q_spec(bq, CB_AQ), _kv_spec(s, CB_AK),
            pl.BlockSpec((N_CB, LANES, s), lambda b, i: (VT_A, 0, b)),
            _gate_spec(bq, 0),
            pl.BlockSpec((N_CB, 2, 2 * bq, bq), lambda b, i: (0, 0, 0, 0)),
            pl.BlockSpec((4, HEAD_DIM), lambda b, i: (0, 0)),
            pl.BlockSpec((LANES, 1), lambda b, i: (0, 0)),
        ],
        out_specs=_out_spec(bq),
        out_shape=jax.ShapeDtypeStruct((b, s, GROUP_W), BF16),
        scratch_shapes=[pltpu.VMEM((N_CB, 1, 2 * bq), F32),
                        pltpu.VMEM((N_CB, LANES + BF16_ROWS, 2 * bq), F32)],
        compiler_params=_params(2),
        name="diff_attention",
    )(proj4, proj4, vt, proj4, tiles, lam_params, head_norm)


def _stick_kernel(q_ref, k_ref, vt_ref, gate_ref, o_ref, rest_ref, acc_ref):
    bq = q_ref.shape[2]
    qi = pl.program_id(1)
    qs = [_stack_halves(q_ref[h, 0] * (HEAD_DIM ** -0.5)) for h in range(N_CB)]
    key = lax.broadcasted_iota(jnp.int32, (bq, 2 * bq), 0)
    query = lax.broadcasted_iota(jnp.int32, (bq, 2 * bq), 1) & (bq - 1)
    strict = key < query

    row = lax.broadcasted_iota(jnp.int32, (bq + BF16_ROWS, bq), 0)
    col = lax.broadcasted_iota(jnp.int32, (bq + BF16_ROWS, bq), 1)
    after = jnp.where((col > row) | (row >= bq), 1.0, 0.0).astype(BF16)

    def step(kbs, carries, first_mask=None):
        items = [(h, kb) for kb in kbs for h in range(N_CB)]
        masks = [first_mask if i < N_CB else None for i in range(len(items))]
        rests = [carries[h][0] for h in range(N_CB)]
        accs = [carries[h][1] for h in range(N_CB)]
        pending = {}

        def score(i):
            h, kb = items[i]
            z = _scores_t(qs[h], k_ref, h, kb, bq)
            sp = jnp.log(1.0 + jnp.exp2(jnp.abs(z) * -LOG2E))
            ls_pos = jnp.minimum(z, 0.0) - sp
            ls_neg = ls_pos - z
            if masks[i] is not None:
                ls_neg = jnp.where(masks[i], ls_neg, 0.0)
            pending[i] = (ls_pos, ls_neg.astype(BF16))

        def cumsum(i):
            h, _ = items[i]
            ls_pos, log_rest = pending[i]
            t = jnp.dot(after, log_rest, preferred_element_type=F32)
            a = jnp.exp2((ls_pos + t[:bq]) * LOG2E)
            if masks[i] is not None:
                a = jnp.where(masks[i], a, 0.0)
            pending[i] = (a.astype(BF16), jnp.exp2(rests[h] * LOG2E))
            rests[h] = rests[h] + t[bq:bq + 1]

        def value(i):
            h, kb = items[i]
            a, scale = pending.pop(i)
            vt = vt_ref[h, :, pl.ds(pl.multiple_of(kb * bq, bq), bq)]
            accs[h] = accs[h] + scale * _value_t((vt[:HEAD_DIM], vt[HEAD_DIM:]), a)

        for i in range(len(items) + 2):
            if i < len(items):
                score(i)
            if 1 <= i <= len(items):
                cumsum(i - 1)
            if i >= 2:
                value(i - 2)
        return tuple((rests[h], accs[h]) for h in range(N_CB))

    zero = ((jnp.zeros((1, 2 * bq), F32), jnp.zeros((HEAD_DIM, 2 * bq), F32)),) * N_CB
    state = (rest_ref, acc_ref)
    odd = qi % 2

    @pl.when(odd == 1)
    def _():
        _store_state(state, step((qi, qi - 1), zero, strict))

    @pl.when(odd == 0)
    def _():
        _store_state(state, step((qi,), zero, strict))

    carries = _pair_loop(qi - odd, lambda i: qi - 1 - odd - i, step, None, state,
                         always_even=True)
    for h in range(N_CB):
        acc = carries[h][1]
        _store_gated(o_ref, gate_ref, h, jnp.concatenate([acc[:, :bq], acc[:, bq:]], axis=0).T)


def _stick_attention(proj4, vt, bq):
    _, b, s, _ = proj4.shape
    return pl.pallas_call(
        _stick_kernel,
        grid=(b, s // bq),
        in_specs=[
            _q_spec(bq, CB_SQ), _kv_spec(s, CB_SK),
            pl.BlockSpec((N_CB, LANES, s), lambda b, i: (VT_S, 0, b)),
            _gate_spec(bq, 1),
        ],
        out_specs=_out_spec(bq),
        out_shape=jax.ShapeDtypeStruct((b, s, GROUP_W), BF16),
        scratch_shapes=[pltpu.VMEM((N_CB, 1, 2 * bq), F32),
                        pltpu.VMEM((N_CB, HEAD_DIM, 2 * bq), F32)],
        compiler_params=_params(2),
        name="stick_attention",
    )(proj4, proj4, vt, proj4)


def _fox_kernel(q_ref, k_ref, vt_ref, gate_ref, negc_ref, o_ref, m_ref, acc_ref):
    bq = q_ref.shape[2]
    qi = pl.program_id(1)
    key = lax.broadcasted_iota(jnp.int32, (bq, 2 * bq), 0)
    query = lax.broadcasted_iota(jnp.int32, (bq, 2 * bq), 1) & (bq - 1)
    causal = key <= query

    lane = lax.broadcasted_iota(jnp.int32, (1, LANES), 1)
    low = lane < HEAD_DIM
    ones_low = jnp.where(lane < 3, 1.0, 0.0).astype(BF16)
    ones_high = jnp.where(low | (lane >= HEAD_DIM + 3), 0.0, 1.0).astype(BF16)
    qs = []
    for h in range(N_CB):
        q = q_ref[h, 0] * (HEAD_DIM ** -0.5)
        qs.append((jnp.where(low, q, ones_high), jnp.where(low, ones_low, q)))

    def item(h, kb, mask, width=1):
        rows = width * bq

        def score():
            start = pl.multiple_of(kb * bq, bq)
            k = k_ref[h, 0, pl.ds(start, rows), :]
            c = negc_ref[h, 0, pl.ds(start, rows), :]
            s = jnp.concatenate(
                [lax.dot_general(jnp.where(low, k, c), qs[h][0], NT_DIMS,
                                 preferred_element_type=F32),
                 lax.dot_general(jnp.where(low, c, k), qs[h][1], NT_DIMS,
                                 preferred_element_type=F32)], axis=1)
            return s if mask is None else jnp.where(mask, s, NEG)

        def values():
            vt = vt_ref[h, :, pl.ds(pl.multiple_of(kb * bq, bq), rows)]
            return _with_ones_rows(vt[:HEAD_DIM]), _with_ones_rows(vt[HEAD_DIM:])
        return h, score, values

    def step(kbs, carries, mask=None):
        if len(kbs) % 2 == 0:
            items = [item(h, kb, mask, 2) for kb in kbs[::2] for h in range(N_CB)]
        else:
            items = [item(h, kb, mask) for kb in kbs for h in range(N_CB)]
        return _softmax_t_blocks(items, carries)

    state = (m_ref, acc_ref)
    odd = qi % 2

    @pl.when(odd == 1)
    def _():
        key2 = lax.broadcasted_iota(jnp.int32, (2 * bq, 2 * bq), 0) - bq
        query2 = lax.broadcasted_iota(jnp.int32, (2 * bq, 2 * bq), 1) & (bq - 1)
        wide = [item(h, qi - 1, key2 <= query2, 2) for h in range(N_CB)]
        _store_state(state, _softmax_t_blocks(wide, None))

    @pl.when(odd == 0)
    def _():
        _store_state(state, step((qi,), None, causal))

    carries = _pair_loop(qi - odd, lambda i: i, step, None, state, always_even=True)
    for h in range(N_CB):
        a = _normalized_t(carries[h][1])
        _store_gated(o_ref, gate_ref, h, jnp.concatenate([a[:, :bq], a[:, bq:]], axis=0).T)


def _fox_attention(proj4, vt, negc, bq):
    _, b, s, _ = proj4.shape
    return pl.pallas_call(
        _fox_kernel,
        grid=(b, s // bq),
        in_specs=[
            _q_spec(bq, CB_FQ), _kv_spec(s, CB_FK),
            pl.BlockSpec((N_CB, LANES, s), lambda b, i: (VT_F, 0, b)),
            _gate_spec(bq, 2),
            pl.BlockSpec((N_CB, 1, s, LANES), lambda b, i: (0, b, 0, 0)),
        ],
        out_specs=_out_spec(bq),
        out_shape=jax.ShapeDtypeStruct((b, s, GROUP_W), BF16),
        scratch_shapes=[pltpu.VMEM((N_CB, 1, 2 * bq), F32),
                        pltpu.VMEM((N_CB, HEAD_DIM + BF16_ROWS, 2 * bq), F32)],
        compiler_params=_params(2),
        name="fox_attention",
    )(proj4, proj4, vt, proj4, negc)


def _mem_kernel(q_ref, km_ref, vmt_ref, gate_ref, o_ref):
    scale = LANES ** -0.5
    scores = [lax.dot_general(km_ref[h, 0], q_ref[h, 0], NT_DIMS,
                              preferred_element_type=F32) * scale for h in range(N_CB)]
    probs = [jnp.exp(s - jnp.max(s, axis=0, keepdims=True)).astype(BF16) for s in scores]
    for h in range(N_CB):
        acc = jnp.dot(_with_ones_rows(vmt_ref[h]), probs[h], preferred_element_type=F32)
        _store_gated(o_ref, gate_ref, h, _normalized_t(acc).T)


def _mem_attention(proj4, km4, vmt, bq):
    _, b, s, _ = proj4.shape
    mlen = km4.shape[2]
    return pl.pallas_call(
        _mem_kernel,
        grid=(b, s // bq),
        in_specs=[
            _q_spec(bq, CB_MQ),
            pl.BlockSpec((N_CB, 1, mlen, LANES), lambda b, i: (0, b, 0, 0)),
            pl.BlockSpec((N_CB, LANES, mlen), lambda b, i: (0, 0, b)),
            _gate_spec(bq, 3),
        ],
        out_specs=_out_spec(bq),
        out_shape=jax.ShapeDtypeStruct((b, s, GROUP_W), BF16),
        compiler_params=_params(2),
        name="mem_attention",
    )(proj4, km4, vmt, proj4)


def _out_proj_kernel(ya_ref, yb_ref, yc_ref, ym_ref, w_ref, g_ref, x_ref, o_ref):
    y = jnp.concatenate([ya_ref[...], yb_ref[...], yc_ref[...], ym_ref[...]], axis=-1)
    z = jnp.dot(y, w_ref[...], preferred_element_type=F32)
    z = z * lax.rsqrt(jnp.mean(z * z, axis=-1, keepdims=True) + EPS)
    o_ref[...] = x_ref[...] + z * g_ref[...]


def _out_proj(ys, w, g, x2d, tm):
    m, d = x2d.shape
    y_spec = pl.BlockSpec((tm, GROUP_W), lambda i: (i, 0))
    return pl.pallas_call(
        _out_proj_kernel,
        grid=(m // tm,),
        in_specs=[y_spec] * 4 + [
            pl.BlockSpec(w.shape, lambda i: (0, 0)),
            pl.BlockSpec((1, d), lambda i: (0, 0)),
            pl.BlockSpec((tm, d), lambda i: (i, 0)),
        ],
        out_specs=pl.BlockSpec((tm, d), lambda i: (i, 0)),
        out_shape=jax.ShapeDtypeStruct((m, d), F32),
        compiler_params=_params(1),
        name="out_proj",
    )(*ys, w, g, x2d)


def _pick(n, cands):
    for c in cands:
        if n % c == 0:
            return c
    raise ValueError(f"no tile in {cands} divides {n}")


def kernel(x, mem, w_in, b_forget, w_mem_kv, w_out, pre_norm, post_norm, mem_norm,
           diff_lambda, diff_head_norm, rel_bias):
    b, s, d = x.shape
    mlen = mem.shape[1]
    depth = w_in.shape[0]
    m = b * s
    bq = _pick(s, (256, 128))
    tm = _pick(m, (512, 256))

    gw = GROUP_W
    logit_lo, logit_hi = 9 * gw, 9 * gw + N_FOX_HEADS
    tiles = _bias_tiles(rel_bias, 2 * bq, bq)
    mem2d = mem.reshape(b * mlen, d)
    x2d = x.reshape(m, d)

    for l in range(depth):
        wl = w_in[l]
        ws = [wl[:, lo:hi].astype(BF16) for lo, hi in
              ((0, 2 * gw), (3 * gw, 5 * gw), (6 * gw, 8 * gw), (logit_hi, wl.shape[1]))]
        wvts = [wl[:, lo:lo + gw].T.astype(BF16) for lo in (2 * gw, 5 * gw, 8 * gw)]
        w_logit = jnp.pad(wl[:, logit_lo:logit_hi],
                          ((0, 0), (0, LANES - N_FOX_HEADS))).astype(BF16)
        b_pad = jnp.pad(b_forget[l], (0, LANES - N_FOX_HEADS)).reshape(1, LANES)

        proj, logit, vt = _in_proj(x2d, pre_norm[l].reshape(1, d), ws, w_logit, wvts, tm)
        proj4 = proj.reshape(N_PROJ_CB, b, s, LANES)
        negc = _forget_cumsum(logit.reshape(b, s, LANES), b_pad, _pick(s, (512, 256, 128)))

        w_kv = w_mem_kv[l].astype(BF16)
        km, vmt = _mem_kv(mem2d, mem_norm[l].reshape(1, d), w_kv[:, :gw], w_kv[:, gw:].T,
                          _pick(b * mlen, (1024, 512, 256)))
        km4 = km.reshape(N_CB, b, mlen, LANES)

        lam_init = 0.8 - 0.6 * math.exp(-0.3 * l)
        y_a = _diff_attention(proj4, vt, tiles, diff_lambda[l],
                              diff_head_norm[l].reshape(LANES, 1), lam_init, bq)
        y_b = _stick_attention(proj4, vt, bq)
        y_c = _fox_attention(proj4, vt, negc, bq)
        y_m = _mem_attention(proj4, km4, vmt, _pick(s, (1024, 512, 256, 128)))
        ys = [y.reshape(m, GROUP_W) for y in (y_a, y_b, y_c, y_m)]
        x2d = _out_proj(ys, w_out[l].astype(BF16), post_norm[l].reshape(1, d), x2d,
                        _pick(m, (1024, 512, 256)))
    return x2d.reshape(b, s, d)
```
